```python
import jax, jax.numpy as jnp
from jax import lax
import numpy as np

D_MODEL = 1024
BATCH = 4
SEQ = 8192
DEPTH = 2

GRID_W = 64
NA_HEAD_DIM = 64
NA_WIDTH = D_MODEL // 2
NA_HEADS = NA_WIDTH // NA_HEAD_DIM
NA_WIN_ROWS = 8
NA_WIN_COLS = 16
LRU_WIDTH = D_MODEL // 2
LRU_BLOCKS = 8
LRU_BLOCK_W = LRU_WIDTH // LRU_BLOCKS
LRU_C = 8.0
CONV_W = 4
FNET_WIDTH = D_MODEL // 2
FNET_GROUPS = 4
FNET_GROUP_W = FNET_WIDTH // FNET_GROUPS
N_BRANCH = 3
IN_COLS = 3 * NA_WIDTH + 2 * LRU_WIDTH + FNET_WIDTH
N_EXPERTS = 32
N_EXPERT_GROUPS = 8
EXPERTS_PER_GROUP = N_EXPERTS // N_EXPERT_GROUPS
TOP_K = 2
D_EXPERT = D_MODEL // 2
MOE_BLOCK = 128
EPS = 1e-6

kernel_name = "hybrid_na_rglru_fnet_moe_encoder"


def rmsnorm(x, g):
    xf = x.astype(jnp.float32)
    y = xf * lax.rsqrt(jnp.mean(xf * xf, axis=-1, keepdims=True) + EPS)
    return (y * g.astype(jnp.float32)).astype(x.dtype)


def modulate(h, shift, scale):
    return h * (1.0 + scale[:, None, :]) + shift[:, None, :]


def neighbourhood_attention(q, k, v, rpb):
    B, T, H, dh = q.shape
    rows = T // GRID_W
    kr = min(NA_WIN_ROWS, rows)
    kc = NA_WIN_COLS
    qg = q.reshape(B, rows, GRID_W, H, dh)
    kg = k.reshape(B, rows, GRID_W, H, dh)
    vg = v.reshape(B, rows, GRID_W, H, dh)
    cols = jnp.arange(GRID_W)
    col_start = jnp.clip(cols - kc // 2, 0, GRID_W - kc)
    col_idx = col_start[:, None] + jnp.arange(kc)[None, :]
    col_bias_idx = col_idx - cols[:, None] + (NA_WIN_COLS - 1)
    scale = dh ** -0.5

    def row_block(r):
        rs = jnp.clip(r - kr // 2, 0, rows - kr)
        q_r = lax.dynamic_index_in_dim(qg, r, axis=1, keepdims=False)
        k_band = lax.dynamic_slice_in_dim(kg, rs, kr, axis=1)
        v_band = lax.dynamic_slice_in_dim(vg, rs, kr, axis=1)
        k_win = k_band[:, :, col_idx]
        v_win = v_band[:, :, col_idx]
        row_bias_idx = rs + jnp.arange(kr) - r + (NA_WIN_ROWS - 1)
        bias = rpb[:, row_bias_idx[:, None, None], col_bias_idx[None, :, :]]
        bias = bias.transpose(0, 2, 1, 3).astype(jnp.float32)
        s = jnp.einsum('bwhd,brwjhd->bhwrj', q_r, k_win).astype(jnp.float32) * scale + bias[None]
        p = jax.nn.softmax(s.reshape(B, H, GRID_W, kr * kc), axis=-1).reshape(B, H, GRID_W, kr, kc)
        return jnp.einsum('bhwrj,brwjhd->bwhd', p.astype(v.dtype), v_win)

    out = lax.map(row_block, jnp.arange(rows))
    return out.transpose(1, 0, 2, 3, 4).reshape(B, T, H * dh)


def centred_dwconv(x, w, b):
    T = x.shape[1]
    left = CONV_W // 2
    right = CONV_W - 1 - left
    xp = jnp.pad(x, ((0, 0), (left, right), (0, 0)))
    y = xp[:, 0:T] * w[0]
    for j in range(1, CONV_W):
        y = y + xp[:, j:j + T] * w[j]
    return y + b


def block_diag(x, w, b):
    B, T, _ = x.shape
    xb = x.reshape(B, T, LRU_BLOCKS, LRU_BLOCK_W)
    return jnp.einsum('btni,nij->btnj', xb, w).reshape(B, T, LRU_WIDTH) + b


def rglru_scan(x, w_r, b_r, w_i, b_i, lam, reverse):
    xf = x.astype(jnp.float32)
    r = jax.nn.sigmoid(block_diag(x, w_r, b_r).astype(jnp.float32))
    i = jax.nn.sigmoid(block_diag(x, w_i, b_i).astype(jnp.float32))
    log_a = -LRU_C * r * jax.nn.softplus(-lam.astype(jnp.float32))
    a = jnp.exp(log_a)
    bx = jnp.sqrt(-jnp.expm1(2.0 * log_a)) * (i * xf)

    def combine(c1, c2):
        a1, b1 = c1
        a2, b2 = c2
        return a1 * a2, a2 * b1 + b2

    _, h = lax.associative_scan(combine, (a, bx), reverse=reverse, axis=1)
    return h


def fourier_mix(f):
    B, T, _ = f.shape
    fg = f.astype(jnp.float32).reshape(B, T, FNET_GROUPS, FNET_GROUP_W)
    y = jnp.fft.fft2(fg, axes=(1, 3), norm='ortho').real
    return y.reshape(B, T, FNET_WIDTH).astype(f.dtype)


def route(u, w_router, router_bias):
    N = u.shape[0]
    scores = jax.nn.sigmoid(jnp.dot(u.astype(jnp.float32), w_router.astype(jnp.float32)))
    sel = scores + router_bias.astype(jnp.float32)
    sel_g = sel.reshape(N, N_EXPERT_GROUPS, EXPERTS_PER_GROUP)
    group_score = lax.top_k(sel_g, TOP_K)[0].sum(axis=-1)
    _, g_idx = lax.top_k(group_score, 1)
    cand = jnp.take_along_axis(sel_g, g_idx[:, :, None], axis=1)[:, 0]
    _, local = lax.top_k(cand, TOP_K)
    e_idx = g_idx * EXPERTS_PER_GROUP + local
    w = jnp.take_along_axis(scores, e_idx, axis=1)
    w = w / jnp.sum(w, axis=-1, keepdims=True)
    return e_idx, w


def moe_ffn(u, w_router, router_bias, w_gate, w_up, w_down):
    B, T, D = u.shape
    N = B * T
    uf = u.reshape(N, D)
    e_idx, gw = route(uf, w_router, router_bias)
    A = N * TOP_K
    flat_e = e_idx.reshape(A)
    flat_tok = jnp.repeat(jnp.arange(N, dtype=jnp.int32), TOP_K)
    flat_w = gw.reshape(A)
    order = jnp.argsort(flat_e)
    sorted_e = flat_e[order]
    counts = jnp.bincount(flat_e, length=N_EXPERTS)
    starts = jnp.cumsum(counts) - counts
    padded = (counts + MOE_BLOCK - 1) // MOE_BLOCK * MOE_BLOCK
    padded_end = jnp.cumsum(padded)
    padded_start = padded_end - padded
    dest = padded_start[sorted_e] + (jnp.arange(A) - starts[sorted_e])
    n_blocks = -(-(A + N_EXPERTS * (MOE_BLOCK - 1)) // MOE_BLOCK)
    P = n_blocks * MOE_BLOCK
    buf_tok = jnp.zeros((P,), jnp.int32).at[dest].set(flat_tok[order])
    buf_w = jnp.zeros((P,), jnp.float32).at[dest].set(flat_w[order])
    block_e = jnp.minimum(
        jnp.searchsorted(padded_end, jnp.arange(n_blocks) * MOE_BLOCK, side='right'), N_EXPERTS - 1)
    xb = uf[buf_tok].reshape(n_blocks, MOE_BLOCK, D)

    def expert_block(args):
        xblk, e = args
        h = jax.nn.silu(xblk @ w_gate[e]) * (xblk @ w_up[e])
        return h @ w_down[e]

    yb = lax.map(expert_block, (xb, block_e)).reshape(P, D)
    y = jax.ops.segment_sum(yb * buf_w[:, None].astype(yb.dtype), buf_tok, num_segments=N)
    return y.reshape(B, T, D)


def setup_inputs(seed: int = 0) -> dict:
    key = jax.random.key(seed)
    ks = jax.random.split(key, 32)
    L, D = DEPTH, D_MODEL
    f32 = jnp.float32

    def nrm(k, shape, fan_in, s=1.0):
        return jax.random.normal(k, shape, f32) * (s * fan_in ** -0.5)

    def small(k, shape, s=0.02):
        return jax.random.normal(k, shape, f32) * s

    u_lam = jax.random.uniform(ks[16], (L, 2, LRU_WIDTH), f32, minval=0.9, maxval=0.999)
    a_lam = u_lam ** (1.0 / LRU_C)
    lru_lambda = jnp.log(a_lam) - jnp.log1p(-a_lam)
    return {
        "x": jax.random.normal(ks[0], (BATCH, SEQ, D), f32),
        "c": jax.random.normal(ks[1], (BATCH, D), f32),
        "ada_w": nrm(ks[2], (L, D, 6 * D), D, 0.5),
        "ada_b": small(ks[3], (L, 6 * D)),
        "norm_mix_g": 1.0 + small(ks[4], (L, D), 0.05),
        "norm_ffn_g": 1.0 + small(ks[5], (L, D), 0.05),
        "w_in": nrm(ks[6], (L, D, IN_COLS), D),
        "w_branch_gate": nrm(ks[7], (L, D, N_BRANCH * D), D),
        "b_branch_gate": small(ks[8], (L, N_BRANCH * D)),
        "na_rpb": small(ks[9], (L, NA_HEADS, 2 * NA_WIN_ROWS - 1, 2 * NA_WIN_COLS - 1), 0.1),
        "lru_conv_w": nrm(ks[10], (L, CONV_W, LRU_WIDTH), CONV_W),
        "lru_conv_b": small(ks[11], (L, LRU_WIDTH)),
        "lru_w_r": nrm(ks[12], (L, 2, LRU_BLOCKS, LRU_BLOCK_W, LRU_BLOCK_W), LRU_BLOCK_W),
        "lru_b_r": small(ks[13], (L, 2, LRU_WIDTH)),
        "lru_w_i": nrm(ks[14], (L, 2, LRU_BLOCKS, LRU_BLOCK_W, LRU_BLOCK_W), LRU_BLOCK_W),
        "lru_b_i": small(ks[15], (L, 2, LRU_WIDTH)),
        "lru_lambda": lru_lambda,
        "w_proj_na": nrm(ks[17], (L, NA_WIDTH, D), NA_WIDTH),
        "w_proj_lru": nrm(ks[18], (L, LRU_WIDTH, D), LRU_WIDTH),
        "w_proj_fnet": nrm(ks[19], (L, FNET_WIDTH, D), FNET_WIDTH),
        "w_out": nrm(ks[20], (L, D, D), D),
        "w_router": nrm(ks[21], (D, N_EXPERTS), D),
        "router_bias": small(ks[22], (N_EXPERTS,), 0.01),
        "w_exp_gate": nrm(ks[23], (L, N_EXPERTS, D, D_EXPERT), D),
        "w_exp_up": nrm(ks[24], (L, N_EXPERTS, D, D_EXPERT), D),
        "w_exp_down": nrm(ks[25], (L, N_EXPERTS, D_EXPERT, D), D_EXPERT),
        "final_g": 1.0 + small(ks[26], (D,), 0.05),
    }


def reference(x, c, ada_w, ada_b, norm_mix_g, norm_ffn_g, w_in, w_branch_gate, b_branch_gate,
              na_rpb, lru_conv_w, lru_conv_b, lru_w_r, lru_b_r, lru_w_i, lru_b_i, lru_lambda,
              w_proj_na, w_proj_lru, w_proj_fnet, w_out, w_router, router_bias,
              w_exp_gate, w_exp_up, w_exp_down, final_g):
    B, T, D = x.shape
    c_act = jax.nn.silu(c)
    splits = [NA_WIDTH, 2 * NA_WIDTH, 3 * NA_WIDTH, 3 * NA_WIDTH + LRU_WIDTH, 3 * NA_WIDTH + 2 * LRU_WIDTH]
    for l in range(DEPTH):
        mod = c_act @ ada_w[l] + ada_b[l]
        sh_mix, sc_mix, gt_mix, sh_ffn, sc_ffn, gt_ffn = jnp.split(mod, 6, axis=-1)

        u = modulate(rmsnorm(x, norm_mix_g[l]), sh_mix, sc_mix)
        proj = u @ w_in[l]
        q, k, v, xr, yr, fr = jnp.split(proj, splits, axis=-1)

        hs = (B, T, NA_HEADS, NA_HEAD_DIM)
        y_na = neighbourhood_attention(q.reshape(hs), k.reshape(hs), v.reshape(hs), na_rpb[l])

        xc = centred_dwconv(xr, lru_conv_w[l], lru_conv_b[l])
        h = (rglru_scan(xc, lru_w_r[l, 0], lru_b_r[l, 0], lru_w_i[l, 0], lru_b_i[l, 0], lru_lambda[l, 0], False)
             + rglru_scan(xc, lru_w_r[l, 1], lru_b_r[l, 1], lru_w_i[l, 1], lru_b_i[l, 1], lru_lambda[l, 1], True))
        y_lru = (h * jax.nn.gelu(yr.astype(jnp.float32))).astype(x.dtype)

        y_f = fourier_mix(fr)

        gates = jax.nn.sigmoid(u @ w_branch_gate[l] + b_branch_gate[l])
        g_na, g_lru, g_f = jnp.split(gates, N_BRANCH, axis=-1)
        merged = (g_na * (y_na @ w_proj_na[l])
                  + g_lru * (y_lru @ w_proj_lru[l])
                  + g_f * (y_f @ w_proj_fnet[l]))
        x = x + gt_mix[:, None, :] * (merged @ w_out[l])

        u2 = modulate(rmsnorm(x, norm_ffn_g[l]), sh_ffn, sc_ffn)
        x = x + gt_ffn[:, None, :] * moe_ffn(u2, w_router, router_bias, w_exp_gate[l], w_exp_up[l], w_exp_down[l])
    return rmsnorm(x, final_g)
```

```python
import jax, jax.numpy as jnp
from jax import lax
import numpy as np
from jax.experimental import pallas as pl

D_MODEL = 1024
BATCH = 4
SEQ = 8192
DEPTH = 2
GRID_W = 64
NA_HEAD_DIM = 64
NA_WIDTH = D_MODEL // 2
NA_HEADS = NA_WIDTH // NA_HEAD_DIM
NA_WIN_ROWS = 8
NA_WIN_COLS = 16
LRU_WIDTH = D_MODEL // 2
LRU_BLOCKS = 8
LRU_BLOCK_W = LRU_WIDTH // LRU_BLOCKS
LRU_C = 8.0
CONV_W = 4
FNET_WIDTH = D_MODEL // 2
FNET_GROUPS = 4
FNET_GROUP_W = FNET_WIDTH // FNET_GROUPS
N_BRANCH = 3
IN_COLS = 3 * NA_WIDTH + 2 * LRU_WIDTH + FNET_WIDTH
N_EXPERTS = 32
N_EXPERT_GROUPS = 8
EXPERTS_PER_GROUP = N_EXPERTS // N_EXPERT_GROUPS
TOP_K = 2
D_EXPERT = D_MODEL // 2
MOE_BLOCK = 128
EPS = 1e-6


def rmsnorm(x, g):
    xf = x.astype(jnp.float32)
    y = xf * lax.rsqrt(jnp.mean(xf * xf, axis=-1, keepdims=True) + EPS)
    return (y * g.astype(jnp.float32)).astype(x.dtype)


def modulate(h, shift, scale):
    return h * (1.0 + scale[:, None, :]) + shift[:, None, :]


def neighbourhood_attention(q, k, v, rpb):
    B, T, H, dh = q.shape
    rows = T // GRID_W
    kr = min(NA_WIN_ROWS, rows)
    kc = NA_WIN_COLS
    qg = q.reshape(B, rows, GRID_W, H, dh)
    kg = k.reshape(B, rows, GRID_W, H, dh)
    vg = v.reshape(B, rows, GRID_W, H, dh)
    cols = jnp.arange(GRID_W)
    col_start = jnp.clip(cols - kc // 2, 0, GRID_W - kc)
    col_idx = col_start[:, None] + jnp.arange(kc)[None, :]
    col_bias_idx = col_idx - cols[:, None] + (NA_WIN_COLS - 1)
    scale = dh ** -0.5

    def row_block(r):
        rs = jnp.clip(r - kr // 2, 0, rows - kr)
        q_r = lax.dynamic_index_in_dim(qg, r, axis=1, keepdims=False)
        k_band = lax.dynamic_slice_in_dim(kg, rs, kr, axis=1)
        v_band = lax.dynamic_slice_in_dim(vg, rs, kr, axis=1)
        k_win = k_band[:, :, col_idx]
        v_win = v_band[:, :, col_idx]
        row_bias_idx = rs + jnp.arange(kr) - r + (NA_WIN_ROWS - 1)
        bias = rpb[:, row_bias_idx[:, None, None], col_bias_idx[None, :, :]]
        bias = bias.transpose(0, 2, 1, 3).astype(jnp.float32)
        s = jnp.einsum('bwhd,brwjhd->bhwrj', q_r, k_win).astype(jnp.float32) * scale + bias[None]
        p = jax.nn.softmax(s.reshape(B, H, GRID_W, kr * kc), axis=-1).reshape(B, H, GRID_W, kr, kc)
        return jnp.einsum('bhwrj,brwjhd->bwhd', p.astype(v.dtype), v_win)

    out = lax.map(row_block, jnp.arange(rows))
    return out.transpose(1, 0, 2, 3, 4).reshape(B, T, H * dh)


def centred_dwconv(x, w, b):
    T = x.shape[1]
    left = CONV_W // 2
    right = CONV_W - 1 - left
    xp = jnp.pad(x, ((0, 0), (left, right), (0, 0)))
    y = xp[:, 0:T] * w[0]
    for j in range(1, CONV_W):
        y = y + xp[:, j:j + T] * w[j]
    return y + b


def block_diag(x, w, b):
    B, T, _ = x.shape
    xb = x.reshape(B, T, LRU_BLOCKS, LRU_BLOCK_W)
    return jnp.einsum('btni,nij->btnj', xb, w).reshape(B, T, LRU_WIDTH) + b


def rglru_scan(x, w_r, b_r, w_i, b_i, lam, reverse):
    xf = x.astype(jnp.float32)
    r = jax.nn.sigmoid(block_diag(x, w_r, b_r).astype(jnp.float32))
    i = jax.nn.sigmoid(block_diag(x, w_i, b_i).astype(jnp.float32))
    log_a = -LRU_C * r * jax.nn.softplus(-lam.astype(jnp.float32))
    a = jnp.exp(log_a)
    bx = jnp.sqrt(-jnp.expm1(2.0 * log_a)) * (i * xf)

    def combine(c1, c2):
        a1, b1 = c1
        a2, b2 = c2
        return a1 * a2, a2 * b1 + b2

    _, h = lax.associative_scan(combine, (a, bx), reverse=reverse, axis=1)
    return h


def fourier_mix(f):
    B, T, _ = f.shape
    fg = f.astype(jnp.float32).reshape(B, T, FNET_GROUPS, FNET_GROUP_W)
    y = jnp.fft.fft2(fg, axes=(1, 3), norm='ortho').real
    return y.reshape(B, T, FNET_WIDTH).astype(f.dtype)


def route(u, w_router, router_bias):
    N = u.shape[0]
    scores = jax.nn.sigmoid(jnp.dot(u.astype(jnp.float32), w_router.astype(jnp.float32)))
    sel = scores + router_bias.astype(jnp.float32)
    sel_g = sel.reshape(N, N_EXPERT_GROUPS, EXPERTS_PER_GROUP)
    group_score = lax.top_k(sel_g, TOP_K)[0].sum(axis=-1)
    _, g_idx = lax.top_k(group_score, 1)
    cand = jnp.take_along_axis(sel_g, g_idx[:, :, None], axis=1)[:, 0]
    _, local = lax.top_k(cand, TOP_K)
    e_idx = g_idx * EXPERTS_PER_GROUP + local
    w = jnp.take_along_axis(scores, e_idx, axis=1)
    w = w / jnp.sum(w, axis=-1, keepdims=True)
    return e_idx, w


def moe_ffn(u, w_router, router_bias, w_gate, w_up, w_down):
    B, T, D = u.shape
    N = B * T
    uf = u.reshape(N, D)
    e_idx, gw = route(uf, w_router, router_bias)
    A = N * TOP_K
    flat_e = e_idx.reshape(A)
    flat_tok = jnp.repeat(jnp.arange(N, dtype=jnp.int32), TOP_K)
    flat_w = gw.reshape(A)
    order = jnp.argsort(flat_e)
    sorted_e = flat_e[order]
    counts = jnp.bincount(flat_e, length=N_EXPERTS)
    starts = jnp.cumsum(counts) - counts
    padded = (counts + MOE_BLOCK - 1) // MOE_BLOCK * MOE_BLOCK
    padded_end = jnp.cumsum(padded)
    padded_start = padded_end - padded
    dest = padded_start[sorted_e] + (jnp.arange(A) - starts[sorted_e])
    n_blocks = -(-(A + N_EXPERTS * (MOE_BLOCK - 1)) // MOE_BLOCK)
    P = n_blocks * MOE_BLOCK
    buf_tok = jnp.zeros((P,), jnp.int32).at[dest].set(flat_tok[order])
    buf_w = jnp.zeros((P,), jnp.float32).at[dest].set(flat_w[order])
    block_e = jnp.minimum(
        jnp.searchsorted(padded_end, jnp.arange(n_blocks) * MOE_BLOCK, side='right'), N_EXPERTS - 1)
    xb = uf[buf_tok].reshape(n_blocks, MOE_BLOCK, D)

    def expert_block(args):
        xblk, e = args
        h = jax.nn.silu(xblk @ w_gate[e]) * (xblk @ w_up[e])
        return h @ w_down[e]

    yb = lax.map(expert_block, (xb, block_e)).reshape(P, D)
    y = jax.ops.segment_sum(yb * buf_w[:, None].astype(yb.dtype), buf_tok, num_segments=N)
    return y.reshape(B, T, D)


def _final_norm_kernel(x_ref, g_ref, o_ref):
    xf = x_ref[...]
    y = xf * lax.rsqrt(jnp.mean(xf * xf, axis=-1, keepdims=True) + EPS)
    o_ref[...] = y * g_ref[...]


def _final_rmsnorm(x, g):
    B, T, D = x.shape
    xf = x.reshape(B * T, D)
    tm = 1024
    out = pl.pallas_call(
        _final_norm_kernel,
        grid=(B * T // tm,),
        in_specs=[pl.BlockSpec((tm, D), lambda i: (i, 0)), pl.BlockSpec((1, D), lambda i: (0, 0))],
        out_specs=pl.BlockSpec((tm, D), lambda i: (i, 0)),
        out_shape=jax.ShapeDtypeStruct((B * T, D), x.dtype),
    )(xf, g.reshape(1, D))
    return out.reshape(B, T, D)


def kernel(x, c, ada_w, ada_b, norm_mix_g, norm_ffn_g, w_in, w_branch_gate, b_branch_gate,
           na_rpb, lru_conv_w, lru_conv_b, lru_w_r, lru_b_r, lru_w_i, lru_b_i, lru_lambda,
           w_proj_na, w_proj_lru, w_proj_fnet, w_out, w_router, router_bias,
           w_exp_gate, w_exp_up, w_exp_down, final_g):
    B, T, D = x.shape
    c_act = jax.nn.silu(c)
    splits = [NA_WIDTH, 2 * NA_WIDTH, 3 * NA_WIDTH, 3 * NA_WIDTH + LRU_WIDTH, 3 * NA_WIDTH + 2 * LRU_WIDTH]
    for l in range(DEPTH):
        mod = c_act @ ada_w[l] + ada_b[l]
        sh_mix, sc_mix, gt_mix, sh_ffn, sc_ffn, gt_ffn = jnp.split(mod, 6, axis=-1)
        u = modulate(rmsnorm(x, norm_mix_g[l]), sh_mix, sc_mix)
        proj = u @ w_in[l]
        q, k, v, xr, yr, fr = jnp.split(proj, splits, axis=-1)
        hs = (B, T, NA_HEADS, NA_HEAD_DIM)
        y_na = neighbourhood_attention(q.reshape(hs), k.reshape(hs), v.reshape(hs), na_rpb[l])
        xc = centred_dwconv(xr, lru_conv_w[l], lru_conv_b[l])
        h = (rglru_scan(xc, lru_w_r[l, 0], lru_b_r[l, 0], lru_w_i[l, 0], lru_b_i[l, 0], lru_lambda[l, 0], False)
             + rglru_scan(xc, lru_w_r[l, 1], lru_b_r[l, 1], lru_w_i[l, 1], lru_b_i[l, 1], lru_lambda[l, 1], True))
        y_lru = (h * jax.nn.gelu(yr.astype(jnp.float32))).astype(x.dtype)
        y_f = fourier_mix(fr)
        gates = jax.nn.sigmoid(u @ w_branch_gate[l] + b_branch_gate[l])
        g_na, g_lru, g_f = jnp.split(gates, N_BRANCH, axis=-1)
        merged = (g_na * (y_na @ w_proj_na[l])
                  + g_lru * (y_lru @ w_proj_lru[l])
                  + g_f * (y_f @ w_proj_fnet[l]))
        x = x + gt_mix[:, None, :] * (merged @ w_out[l])
        u2 = modulate(rmsnorm(x, norm_ffn_g[l]), sh_ffn, sc_ffn)
        x = x + gt_ffn[:, None, :] * moe_ffn(u2, w_router, router_bias, w_exp_gate[l], w_exp_up[l], w_exp_down[l])
    return _final_rmsnorm(x, final_g)
```

```python
import functools

import numpy as np
import jax
import jax.numpy as jnp
from jax import lax
from jax.experimental import pallas as pl
from jax.experimental.pallas import tpu as pltpu

F32 = jnp.float32
BF16 = jnp.bfloat16
I32 = jnp.int32

GRID_W = 64
NA_HEAD_DIM = 64
NA_WIN_ROWS = 8
NA_WIN_COLS = 16
NA_HEADS_PER_TILE = 4
LRU_BLOCKS = 8
LRU_C = 8.0
CONV_W = 4
FNET_GROUP_W = 128
N_EXPERTS = 32
N_EXPERT_GROUPS = 8
EXPERTS_PER_GROUP = 4
TOP_K = 2
EPS = 1e-6
MASK_VALUE = -1e30

V7X_VMEM_BYTES = 64 * 1024 * 1024
LANES = 128
SUBLANES = 8

TM_MIX = 1024
TN_MIX = 512
NA_ROWS_PER_STEP = 8
TM_LRU = 512
T_SCAN = 512
TM_MERGE = 512
TM_MOE = 512
MOE_ROWS = 256


def _cparams(semantics, vmem_mb):
    assert vmem_mb * 1024 * 1024 < V7X_VMEM_BYTES
    return pltpu.CompilerParams(dimension_semantics=semantics, vmem_limit_bytes=vmem_mb * 1024 * 1024)


def _split2(a):
    hi = a.astype(BF16)
    lo = (a - hi.astype(F32)).astype(BF16)
    return hi, lo


def _dot_hi(a, b, dims):
    a_hi, a_lo = _split2(a)
    b_hi, b_lo = _split2(b)
    dn = (dims, ((), ()))
    out = lax.dot_general(a_hi, b_hi, dn, preferred_element_type=F32)
    out = out + lax.dot_general(a_hi, b_lo, dn, preferred_element_type=F32)
    out = out + lax.dot_general(a_lo, b_hi, dn, preferred_element_type=F32)
    return out


def _rmsnorm_mod(x, g, sh, sc):
    y = x * lax.rsqrt(jnp.mean(x * x, axis=-1, keepdims=True) + EPS)
    return (y * g) * (1.0 + sc) + sh


def _ada_kernel(c_ref, w_ref, b_ref, o_ref):
    c = c_ref[...]
    ca = c * jax.nn.sigmoid(c)
    o_ref[0] = _dot_hi(ca, w_ref[0], ((1,), (0,))) + b_ref[0]


def _ada_mod(c, ada_w, ada_b):
    L, D, D6 = ada_w.shape
    B = c.shape[0]
    bp = -(-B // SUBLANES) * SUBLANES
    cp = jnp.pad(c, ((0, bp - B), (0, 0)))
    tn = D6 // 4
    out = pl.pallas_call(
        _ada_kernel,
        grid=(L, D6 // tn),
        in_specs=[pl.BlockSpec((bp, D), lambda l, j: (0, 0)),
                  pl.BlockSpec((1, D, tn), lambda l, j: (l, 0, j)),
                  pl.BlockSpec((1, 1, tn), lambda l, j: (l, 0, j))],
        out_specs=pl.BlockSpec((1, bp, tn), lambda l, j: (l, 0, j)),
        out_shape=jax.ShapeDtypeStruct((L, bp, D6), F32),
        compiler_params=_cparams(("parallel", "parallel"), 40),
        name="ada_mod",
    )(cp, ada_w, ada_b.reshape(L, 1, D6))
    return out[:, :B]


def _mixin_kernel(x_ref, g_ref, sh_ref, sc_ref, w_ref, bg_ref,
                  q_ref, k_ref, v_ref, xr_ref, gy_ref, fr_ref, gate_ref, u_sc):
    j = pl.program_id(1)

    @pl.when(j == 0)
    def _():
        u_sc[...] = _rmsnorm_mod(x_ref[...], g_ref[...], sh_ref[0], sc_ref[0]).astype(BF16)

    acc = jnp.dot(u_sc[...], w_ref[...], preferred_element_type=F32)

    @pl.when(j == 0)
    def _():
        q_ref[...] = acc.astype(BF16)

    @pl.when(j == 1)
    def _():
        k_ref[...] = acc.astype(BF16)

    @pl.when(j == 2)
    def _():
        v_ref[...] = acc.astype(BF16)

    @pl.when(j == 3)
    def _():
        xr_ref[...] = acc

    @pl.when(j == 4)
    def _():
        gy_ref[...] = jax.nn.gelu(acc).astype(BF16)

    @pl.when(j == 5)
    def _():
        fr_ref[...] = acc.astype(BF16)

    @pl.when(j >= 6)
    def _():
        gate_ref[...] = jax.nn.sigmoid(acc + bg_ref[...]).astype(BF16)


def _mix_in(x2, g, sh, sc, wcat, bgate, B, T):
    N, D = x2.shape
    tm, tn = TM_MIX, TN_MIX
    tpb = T // tm
    n_proj = 6
    n_gate = bgate.shape[1] // tn
    W = tn
    tok = lambda i, j: (i, 0)
    scan = lambda i, j: (i % tpb, i // tpb)
    per_b = lambda i, j: (i // tpb, 0, 0)
    out_shape = [jax.ShapeDtypeStruct((N, W), BF16)] * 3 + [
        jax.ShapeDtypeStruct((N, W), F32),
        jax.ShapeDtypeStruct((T, B * W), BF16),
        jax.ShapeDtypeStruct((N, W), BF16),
        jax.ShapeDtypeStruct((N, n_gate * tn), BF16)]
    out_specs = [pl.BlockSpec((tm, W), tok)] * 3 + [
        pl.BlockSpec((tm, W), tok),
        pl.BlockSpec((tm, W), scan),
        pl.BlockSpec((tm, W), tok),
        pl.BlockSpec((tm, tn), lambda i, j: (i, jnp.maximum(j - n_proj, 0)))]
    return pl.pallas_call(
        _mixin_kernel,
        grid=(N // tm, n_proj + n_gate),
        in_specs=[pl.BlockSpec((tm, D), tok),
                  pl.BlockSpec((1, D), lambda i, j: (0, 0)),
                  pl.BlockSpec((1, 1, D), per_b),
                  pl.BlockSpec((1, 1, D), per_b),
                  pl.BlockSpec((D, tn), lambda i, j: (0, j)),
                  pl.BlockSpec((1, tn), lambda i, j: (0, jnp.maximum(j - n_proj, 0)))],
        out_specs=out_specs,
        out_shape=out_shape,
        scratch_shapes=[pltpu.VMEM((tm, D), BF16)],
        compiler_params=_cparams(("parallel", "arbitrary"), 48),
        name="mix_in",
    )(x2, g, sh, sc, wcat, bgate)


def _na_kernel(q_ref, kp_ref, kc_ref, kn_ref, vp_ref, vc_ref, vn_ref, bias_ref, o_ref, kb, vb):
    j = pl.program_id(1)
    nj = pl.num_programs(1)
    R = NA_ROWS_PER_STEP
    half = R // 2
    W = GRID_W
    hw = half * W
    tq = R * W
    win = NA_WIN_ROWS * W
    HT = NA_HEADS_PER_TILE
    lt = HT * NA_HEAD_DIM
    kb[0:hw] = kp_ref[...]
    kb[hw:hw + tq] = kc_ref[...]
    kb[hw + tq:2 * hw + tq] = kn_ref[...]
    vb[0:hw] = vp_ref[...]
    vb[hw:hw + tq] = vc_ref[...]
    vb[hw + tq:2 * hw + tq] = vn_ref[...]
    head_of_lane = lax.broadcasted_iota(I32, (W, lt), 1) // NA_HEAD_DIM
    n_tiles = q_ref.shape[1] // lt
    scale = jnp.asarray(NA_HEAD_DIM ** -0.5, BF16)
    for dr in range(R):
        lo = jnp.where(j == 0, max(dr, half), jnp.where(j == nj - 1, min(dr, half), dr))
        var = lo - dr + (half - 1)
        start = pl.multiple_of(lo * W, W)
        for ht in range(n_tiles):
            cs = slice(ht * lt, (ht + 1) * lt)
            qrow = q_ref[dr * W:(dr + 1) * W, cs] * scale
            qbd = jnp.concatenate(
                [jnp.where(head_of_lane == h, qrow, jnp.zeros_like(qrow)) for h in range(HT)], axis=0)
            kwin = kb[pl.ds(start, win), cs]
            s = lax.dot_general(qbd, kwin, (((1,), (1,)), ((), ())), preferred_element_type=F32)
            s = s + bias_ref[var, ht]
            m = jnp.max(s, axis=-1, keepdims=True)
            p = jnp.exp(s - m)
            l = jnp.sum(p, axis=-1, keepdims=True)
            vwin = vb[pl.ds(start, win), cs]
            o = jnp.dot(p.astype(BF16), vwin, preferred_element_type=F32) / l
            orow = jnp.where(head_of_lane == 0, o[0:W], 0.0)
            for h in range(1, HT):
                orow = orow + jnp.where(head_of_lane == h, o[h * W:(h + 1) * W], 0.0)
            o_ref[dr * W:(dr + 1) * W, cs] = orow.astype(BF16)


def _na_bias_table(rpb):
    H = rpb.shape[0]
    W, KR, KC = GRID_W, NA_WIN_ROWS, NA_WIN_COLS
    w = np.arange(W)
    cstart = np.clip(w - KC // 2, 0, W - KC)
    wk = np.arange(W)
    col_ok = (wk[None, :] >= cstart[:, None]) & (wk[None, :] < cstart[:, None] + KC)
    cb = np.clip(wk[None, :] - w[:, None] + (KC - 1), 0, 2 * KC - 2)
    a_idx = np.arange(KR)[:, None] + np.arange(KR)[None, :]
    tab = rpb[:, a_idx[:, :, None, None], cb[None, None, :, :]]
    tab = jnp.where(col_ok[None, None, None], tab.astype(F32), MASK_VALUE)
    tab = tab.transpose(1, 0, 3, 2, 4)
    HT = NA_HEADS_PER_TILE
    return tab.reshape(KR, H // HT, HT * W, KR * W)


def _na(q, k, v, bias, B, T):
    N, C = q.shape
    R = NA_ROWS_PER_STEP
    tq = R * GRID_W
    hw = tq // 2
    nj = T // tq
    nh = T // hw
    qmap = lambda b, j: (b * nj + j, 0)
    pmap = lambda b, j: (b * nh + jnp.maximum(2 * j - 1, 0), 0)
    nmap = lambda b, j: (b * nh + jnp.minimum(2 * j + 2, nh - 1), 0)
    return pl.pallas_call(
        _na_kernel,
        grid=(B, nj),
        in_specs=[pl.BlockSpec((tq, C), qmap),
                  pl.BlockSpec((hw, C), pmap), pl.BlockSpec((tq, C), qmap), pl.BlockSpec((hw, C), nmap),
                  pl.BlockSpec((hw, C), pmap), pl.BlockSpec((tq, C), qmap), pl.BlockSpec((hw, C), nmap),
                  pl.BlockSpec(bias.shape, lambda b, j: (0, 0, 0, 0))],
        out_specs=pl.BlockSpec((tq, C), qmap),
        out_shape=jax.ShapeDtypeStruct((N, C), BF16),
        scratch_shapes=[pltpu.VMEM((2 * tq, C), BF16), pltpu.VMEM((2 * tq, C), BF16)],
        compiler_params=_cparams(("parallel", "parallel"), 48),
        name="na_attention",
    )(q, k, k, k, v, v, v, bias)


def _lru_gate_kernel(x_ref, xp_ref, xn_ref, cw_ref, cb_ref, wg_ref, bg_ref, lam_ref,
                     af_ref, bf_ref, ab_ref, bb_ref, xpad, *, tpb):
    i = pl.program_id(0)
    tm, C = x_ref.shape
    H = SUBLANES
    first = (i % tpb) == 0
    last = (i % tpb) == tpb - 1
    xpad[0:H] = jnp.where(first, 0.0, xp_ref[...])
    xpad[H:H + tm] = x_ref[...]
    xpad[H + tm:2 * H + tm] = jnp.where(last, 0.0, xn_ref[...])
    left = CONV_W // 2
    xc = xpad[H - left:H - left + tm] * cw_ref[0:1]
    for t in range(1, CONV_W):
        xc = xc + xpad[H - left + t:H - left + t + tm] * cw_ref[t:t + 1]
    xc = xc + cb_ref[...]
    xcb = xc.astype(BF16)
    for d, (a_ref, b_ref) in enumerate(((af_ref, bf_ref), (ab_ref, bb_ref))):
        g = jnp.dot(xcb, wg_ref[d], preferred_element_type=F32) + bg_ref[d]
        r = jax.nn.sigmoid(g[:, :C])
        ig = jax.nn.sigmoid(g[:, C:])
        log_a = -LRU_C * r * jax.nn.softplus(-lam_ref[d])
        a = jnp.exp(log_a)
        a_ref[...] = a
        b_ref[...] = jnp.sqrt(1.0 - a * a) * (ig * xc)


def _lru_gates(xr, conv_w, conv_b, wg, bg, lam, B, T):
    N, C = xr.shape
    tm = TM_LRU
    tpb = T // tm
    nb8 = N // SUBLANES
    r8 = tm // SUBLANES
    scan = lambda i: (i % tpb, i // tpb)
    const2 = lambda i: (0, 0)
    const3 = lambda i: (0, 0, 0)
    outs = pl.pallas_call(
        functools.partial(_lru_gate_kernel, tpb=tpb),
        grid=(N // tm,),
        in_specs=[pl.BlockSpec((tm, C), lambda i: (i, 0)),
                  pl.BlockSpec((SUBLANES, C), lambda i: (jnp.maximum(i * r8 - 1, 0), 0)),
                  pl.BlockSpec((SUBLANES, C), lambda i: (jnp.minimum((i + 1) * r8, nb8 - 1), 0)),
                  pl.BlockSpec((CONV_W, C), const2),
                  pl.BlockSpec((1, C), const2),
                  pl.BlockSpec((2, C, 2 * C), const3),
                  pl.BlockSpec((2, 1, 2 * C), const3),
                  pl.BlockSpec((2, 1, C), const3)],
        out_specs=[pl.BlockSpec((tm, C), scan)] * 4,
        out_shape=[jax.ShapeDtypeStruct((T, B * C), F32)] * 4,
        scratch_shapes=[pltpu.VMEM((tm + 2 * SUBLANES, C), F32)],
        compiler_params=_cparams(("parallel",), 48),
        name="lru_gates",
    )(xr, xr, xr, conv_w, conv_b, wg, bg, lam)
    return outs


def _scan_bwd_kernel(a_ref, b_ref, h_ref, carry):
    @pl.when(pl.program_id(0) == 0)
    def _():
        carry[...] = jnp.zeros_like(carry)

    tc = a_ref.shape[0]

    def body(s, h):
        t = tc - 1 - s
        h = a_ref[t] * h + b_ref[t]
        h_ref[t] = h
        return h

    carry[...] = lax.fori_loop(0, tc, body, carry[...], unroll=8)


def _scan_fwd_kernel(a_ref, b_ref, hb_ref, gy_ref, y_ref, carry):
    @pl.when(pl.program_id(0) == 0)
    def _():
        carry[...] = jnp.zeros_like(carry)

    tc = a_ref.shape[0]

    def body(t, h):
        h = a_ref[t] * h + b_ref[t]
        y_ref[t] = ((h + hb_ref[t]) * gy_ref[t].astype(F32)).astype(BF16)
        return h

    carry[...] = lax.fori_loop(0, tc, body, carry[...], unroll=8)


def _lru_scans(af, bf, ab, bb, gy):
    T, BC = af.shape
    S = BC // LANES
    tc = T_SCAN
    nc = T // tc
    v3 = lambda z: z.reshape(T, S, LANES)
    blk = (tc, S, LANES)
    fwd = lambda c: (c, 0, 0)
    rev = lambda c: (nc - 1 - c, 0, 0)
    hb = pl.pallas_call(
        _scan_bwd_kernel,
        grid=(nc,),
        in_specs=[pl.BlockSpec(blk, rev), pl.BlockSpec(blk, rev)],
        out_specs=pl.BlockSpec(blk, rev),
        out_shape=jax.ShapeDtypeStruct((T, S, LANES), F32),
        scratch_shapes=[pltpu.VMEM((S, LANES), F32)],
        compiler_params=_cparams(("arbitrary",), 48),
        name="lru_scan_bwd",
    )(v3(ab), v3(bb))
    y = pl.pallas_call(
        _scan_fwd_kernel,
        grid=(nc,),
        in_specs=[pl.BlockSpec(blk, fwd)] * 4,
        out_specs=pl.BlockSpec(blk, fwd),
        out_shape=jax.ShapeDtypeStruct((T, S, LANES), BF16),
        scratch_shapes=[pltpu.VMEM((S, LANES), F32)],
        compiler_params=_cparams(("arbitrary",), 48),
        name="lru_scan_fwd",
    )(v3(af), v3(bf), hb, v3(gy))
    return y.reshape(T, BC)


def _fnet_tables(T):
    T2 = FNET_GROUP_W
    T1 = T // T2
    k1 = np.arange(T1)[:, None]
    t1 = np.arange(T1)[None, :]
    stage1 = np.zeros((T2, 2 * T1, 2 * T1), np.float32)
    for t2 in range(T2):
        ph = 2.0 * np.pi * ((k1 * (T2 * t1 + t2)) % T) / T
        c, s = np.cos(ph), np.sin(ph)
        stage1[t2] = np.block([[c, s], [-s, c]])
    n = np.arange(T2)
    ph2 = 2.0 * np.pi * ((n[:, None] * n[None, :]) % T2) / T2
    c2, s2 = np.cos(ph2).astype(np.float32), np.sin(ph2).astype(np.float32)
    return stage1, np.concatenate([c2, s2], axis=1)


def _fnet_kernel(x_ref, m1_ref, cs_ref, o_ref, zr, zi, yo, *, T1):
    T2 = FNET_GROUP_W
    T = T1 * T2
    cs = cs_ref[...]
    for k1 in range(T1):
        blk = slice(k1 * T2, (k1 + 1) * T2)
        z = jnp.dot(x_ref[blk, :], cs, preferred_element_type=F32)
        zr[blk, :] = z[:, :T2]
        zi[blk, :] = -z[:, T2:]
    for t2 in range(T2):
        rows = pl.ds(t2, T1, stride=T2)
        zin = jnp.concatenate([zr[rows, :], zi[rows, :]], axis=0).astype(BF16)
        a = jnp.dot(m1_ref[t2], zin, preferred_element_type=F32)
        zr[rows, :] = a[:T1]
        zi[rows, :] = a[T1:]
    inv = 1.0 / np.sqrt(float(T) * T2)
    for k1 in range(T1):
        blk = slice(k1 * T2, (k1 + 1) * T2)
        ain = jnp.concatenate([zr[blk, :], zi[blk, :]], axis=0).astype(BF16)
        y = jnp.dot(cs, ain, preferred_element_type=F32)
        yo[pl.ds(k1, T2, stride=T1), :] = y * inv
    o_ref[...] = yo[...].astype(BF16)


def _fnet(fr, m1, cs, B, T):
    N, C = fr.shape
    G = C // FNET_GROUP_W
    T1 = T // FNET_GROUP_W
    blk = pl.BlockSpec((T, FNET_GROUP_W), lambda b, g: (b, g))
    return pl.pallas_call(
        functools.partial(_fnet_kernel, T1=T1),
        grid=(B, G),
        in_specs=[blk,
                  pl.BlockSpec(m1.shape, lambda b, g: (0, 0, 0)),
                  pl.BlockSpec(cs.shape, lambda b, g: (0, 0))],
        out_specs=blk,
        out_shape=jax.ShapeDtypeStruct((N, C), BF16),
        scratch_shapes=[pltpu.VMEM((T, FNET_GROUP_W), F32)] * 3,
        compiler_params=_cparams(("parallel", "parallel"), 48),
        name="fnet",
    )(fr, m1, cs)


def _merge_kernel(gate_ref, yna_ref, ylru_ref, yf_ref, x_ref, wna_ref, wlru_ref, wf_ref, wout_ref,
                  gt_ref, g_ref, sh_ref, sc_ref, wr_ref, rb_ref, tri_ref,
                  x1_ref, u2_ref, ri_ref, rw_ref, cnt_ref, cnt_sc):
    i = pl.program_id(0)

    @pl.when(i == 0)
    def _():
        cnt_sc[...] = jnp.zeros_like(cnt_sc)

    tm, D = x_ref.shape
    gates = gate_ref[...]
    merged = gates[:, 0:D].astype(F32) * jnp.dot(yna_ref[...], wna_ref[...], preferred_element_type=F32)
    merged = merged + gates[:, D:2 * D].astype(F32) * jnp.dot(ylru_ref[...], wlru_ref[...], preferred_element_type=F32)
    merged = merged + gates[:, 2 * D:3 * D].astype(F32) * jnp.dot(yf_ref[...], wf_ref[...], preferred_element_type=F32)
    out = jnp.dot(merged.astype(BF16), wout_ref[...], preferred_element_type=F32)
    x1 = x_ref[...] + gt_ref[0] * out
    x1_ref[...] = x1
    u2 = _rmsnorm_mod(x1, g_ref[...], sh_ref[0], sc_ref[0])
    u2_ref[...] = u2

    G, S = N_EXPERT_GROUPS, EXPERTS_PER_GROUP
    logits = _dot_hi(wr_ref[...], u2, ((1,), (1,)))
    score = jax.nn.sigmoid(logits)
    sel = score + rb_ref[...]
    sel_s = [sel[s * G:(s + 1) * G] for s in range(S)]
    sc_s = [score[s * G:(s + 1) * G] for s in range(S)]
    gscore = None
    for a in range(S):
        for b in range(a + 1, S):
            pair = sel_s[a] + sel_s[b]
            gscore = pair if gscore is None else jnp.maximum(gscore, pair)
    giota = lax.broadcasted_iota(I32, (G, tm), 0)
    gmax = jnp.max(gscore, axis=0, keepdims=True)
    gidx = jnp.min(jnp.where(gscore == gmax, giota, G), axis=0, keepdims=True)
    gm = giota == gidx
    cand = [jnp.sum(jnp.where(gm, sel_s[s], 0.0), axis=0, keepdims=True) for s in range(S)]
    raw = [jnp.sum(jnp.where(gm, sc_s[s], 0.0), axis=0, keepdims=True) for s in range(S)]

    def first_argmax(vals):
        m = vals[0]
        for s in range(1, S):
            m = jnp.maximum(m, vals[s])
        idx = jnp.full(m.shape, S - 1, I32)
        for s in range(S - 2, -1, -1):
            idx = jnp.where(vals[s] == m, s, idx)
        return idx

    l0 = first_argmax(cand)
    l1 = first_argmax([jnp.where(l0 == s, -jnp.inf, cand[s]) for s in range(S)])

    def pick(vals, idx):
        out = vals[S - 1]
        for s in range(S - 2, -1, -1):
            out = jnp.where(idx == s, vals[s], out)
        return out

    w0 = pick(raw, l0)
    w1 = pick(raw, l1)
    wsum = w0 + w1
    w0 = w0 / wsum
    w1 = w1 / wsum
    e0 = gidx * S + l0
    e1 = gidx * S + l1

    E = G * S
    eiota = lax.broadcasted_iota(I32, (E, tm), 0)
    oh0 = eiota == (l0 * G + gidx)
    oh1 = eiota == (l1 * G + gidx)
    oh = jnp.where(oh0 | oh1, 1.0, 0.0)
    incl = jnp.dot(oh.astype(BF16), tri_ref[...], preferred_element_type=F32)
    before = incl - oh + cnt_sc[:, 0:1]
    p0 = jnp.sum(jnp.where(oh0, before, 0.0), axis=0, keepdims=True).astype(I32)
    p1 = jnp.sum(jnp.where(oh1, before, 0.0), axis=0, keepdims=True).astype(I32)
    cnt_sc[...] = cnt_sc[...] + jnp.sum(oh, axis=1, keepdims=True)
    cnt_ref[...] = cnt_sc[...].astype(I32)

    row = lax.broadcasted_iota(I32, (SUBLANES, tm), 0)
    ri_ref[...] = jnp.where(row == 0, e0, jnp.where(row == 1, e1, jnp.where(row == 2, p0, jnp.where(row == 3, p1, 0))))
    rw_ref[...] = jnp.where(row == 0, w0, jnp.where(row == 1, w1, 0.0))


def _merge(gates, y_na, y_lru, y_f, x2, wna, wlru, wf, wout, gt, g, sh, sc, wr, rb, tri, B, T):
    N, D = x2.shape
    C = y_na.shape[1]
    tm = TM_MERGE
    tpb = T // tm
    E = wr.shape[0]
    tok = lambda i: (i, 0)
    scan = lambda i: (i % tpb, i // tpb)
    per_b = lambda i: (i // tpb, 0, 0)
    const2 = lambda i: (0, 0)
    return pl.pallas_call(
        _merge_kernel,
        grid=(N // tm,),
        in_specs=[pl.BlockSpec((tm, 3 * D), tok),
                  pl.BlockSpec((tm, C), tok),
                  pl.BlockSpec((tm, C), scan),
                  pl.BlockSpec((tm, C), tok),
                  pl.BlockSpec((tm, D), tok),
                  pl.BlockSpec((C, D), const2), pl.BlockSpec((C, D), const2), pl.BlockSpec((C, D), const2),
                  pl.BlockSpec((D, D), const2),
                  pl.BlockSpec((1, 1, D), per_b),
                  pl.BlockSpec((1, D), const2),
                  pl.BlockSpec((1, 1, D), per_b),
                  pl.BlockSpec((1, 1, D), per_b),
                  pl.BlockSpec((E, D), const2),
                  pl.BlockSpec((E, 1), const2),
                  pl.BlockSpec((tm, tm), const2)],
        out_specs=[pl.BlockSpec((tm, D), tok),
                   pl.BlockSpec((tm, D), tok),
                   pl.BlockSpec((SUBLANES, tm), lambda i: (0, i)),
                   pl.BlockSpec((SUBLANES, tm), lambda i: (0, i)),
                   pl.BlockSpec((E, LANES), const2)],
        out_shape=[jax.ShapeDtypeStruct((N, D), F32),
                   jax.ShapeDtypeStruct((N, D), F32),
                   jax.ShapeDtypeStruct((SUBLANES, N), I32),
                   jax.ShapeDtypeStruct((SUBLANES, N), F32),
                   jax.ShapeDtypeStruct((E, LANES), I32)],
        scratch_shapes=[pltpu.VMEM((E, LANES), F32)],
        compiler_params=_cparams(("arbitrary",), 48),
        name="merge_route",
    )(gates, y_na, y_lru, y_f, x2, wna, wlru, wf, wout, gt, g, sh, sc, wr, rb, tri)


def _row_copy(src, s, dst, d, sem):
    return pltpu.make_async_copy(src.at[pl.ds(s, 1)], dst.at[pl.ds(d, 1)], sem)


def _dispatch_kernel(ps_ref, ri_ref, u_ref, xs_in, xs_out, sem):
    del xs_in
    tm = u_ref.shape[0]

    def body(t, c):
        for k in range(TOP_K):
            d = ps_ref[ri_ref[k, t]] + ri_ref[TOP_K + k, t]
            _row_copy(u_ref, t, xs_out, d, sem).start()
        return c

    lax.fori_loop(0, tm, body, 0, unroll=8)

    for k in range(TOP_K):
        pltpu.make_async_copy(u_ref, xs_out.at[pl.ds(0, tm)], sem).wait()


def _dispatch(pstart, ri, u2, P):
    N, D = u2.shape
    tm = TM_MOE
    xs0 = jnp.zeros((P, D), F32)
    grid_spec = pltpu.PrefetchScalarGridSpec(
        num_scalar_prefetch=1,
        grid=(N // tm,),
        in_specs=[pl.BlockSpec((SUBLANES, tm), lambda i, ps: (0, i), memory_space=pltpu.SMEM),
                  pl.BlockSpec((tm, D), lambda i, ps: (i, 0)),
                  pl.BlockSpec(memory_space=pl.ANY)],
        out_specs=pl.BlockSpec(memory_space=pl.ANY),
        scratch_shapes=[pltpu.SemaphoreType.DMA(())],
    )
    return pl.pallas_call(
        _dispatch_kernel,
        grid_spec=grid_spec,
        out_shape=jax.ShapeDtypeStruct((P, D), F32),
        input_output_aliases={3: 0},
        compiler_params=_cparams(("arbitrary",), 32),
        name="moe_dispatch",
    )(pstart, ri, u2, xs0)


def _expert_kernel(be_ref, nu_ref, xs_ref, wg_ref, wu_ref, wd_ref, ys_ref, wg_sc, wu_sc, wd_sc):
    i = pl.program_id(0)
    prev = be_ref[jnp.maximum(i - 1, 0)]
    fresh = (i == 0) | (be_ref[i] != prev)

    @pl.when(fresh)
    def _():
        wg_sc[...] = wg_ref[0].astype(BF16)
        wu_sc[...] = wu_ref[0].astype(BF16)
        wd_sc[...] = wd_ref[0].astype(BF16)

    @pl.when(i < nu_ref[0])
    def _():
        xb = xs_ref[...].astype(BF16)
        hg = jnp.dot(xb, wg_sc[...], preferred_element_type=F32)
        hu = jnp.dot(xb, wu_sc[...], preferred_element_type=F32)
        h = (hg * jax.nn.sigmoid(hg)) * hu
        ys_ref[...] = jnp.dot(h.astype(BF16), wd_sc[...], preferred_element_type=F32)

    @pl.when(i >= nu_ref[0])
    def _():
        ys_ref[...] = jnp.zeros_like(ys_ref)


def _experts(block_e, n_used, xs, w_gate, w_up, w_down):
    P, D = xs.shape
    E, _, DE = w_gate.shape
    nb = P // MOE_ROWS
    wmap = lambda i, be, nu: (be[i], 0, 0)
    grid_spec = pltpu.PrefetchScalarGridSpec(
        num_scalar_prefetch=2,
        grid=(nb,),
        in_specs=[pl.BlockSpec((MOE_ROWS, D), lambda i, be, nu: (i, 0)),
                  pl.BlockSpec((1, D, DE), wmap),
                  pl.BlockSpec((1, D, DE), wmap),
                  pl.BlockSpec((1, DE, D), wmap)],
        out_specs=pl.BlockSpec((MOE_ROWS, D), lambda i, be, nu: (i, 0)),
        scratch_shapes=[pltpu.VMEM((D, DE), BF16), pltpu.VMEM((D, DE), BF16), pltpu.VMEM((DE, D), BF16)],
    )
    return pl.pallas_call(
        _expert_kernel,
        grid_spec=grid_spec,
        out_shape=jax.ShapeDtypeStruct((P, D), F32),
        compiler_params=_cparams(("arbitrary",), 48),
        name="moe_experts",
    )(block_e, n_used, xs, w_gate, w_up, w_down)


def _combine_kernel(ps_ref, ri_ref, rw_ref, x_ref, gt_ref, fg_ref, ys_ref, o_ref, g0, g1, sem, *, final):
    tm = x_ref.shape[0]
    bufs = (g0, g1)

    def body(t, c):
        for k in range(TOP_K):
            d = ps_ref[ri_ref[k, t]] + ri_ref[TOP_K + k, t]
            _row_copy(ys_ref, d, bufs[k], t, sem).start()
        return c

    lax.fori_loop(0, tm, body, 0, unroll=8)

    for k in range(TOP_K):
        pltpu.make_async_copy(ys_ref.at[pl.ds(0, tm)], bufs[k], sem).wait()
    w = rw_ref[...]
    y = g0[...] * w[:, 0:1] + g1[...] * w[:, 1:2]
    x2 = x_ref[...] + gt_ref[0] * y
    if final:
        x2 = (x2 * lax.rsqrt(jnp.mean(x2 * x2, axis=-1, keepdims=True) + EPS)) * fg_ref[...]
    o_ref[...] = x2


def _combine(pstart, ri, rw_t, x1, gt, final_g, ys, T, final):
    N, D = x1.shape
    tm = TM_MOE
    tpb = T // tm
    grid_spec = pltpu.PrefetchScalarGridSpec(
        num_scalar_prefetch=1,
        grid=(N // tm,),
        in_specs=[pl.BlockSpec((SUBLANES, tm), lambda i, ps: (0, i), memory_space=pltpu.SMEM),
                  pl.BlockSpec((tm, SUBLANES), lambda i, ps: (i, 0)),
                  pl.BlockSpec((tm, D), lambda i, ps: (i, 0)),
                  pl.BlockSpec((1, 1, D), lambda i, ps: (i // tpb, 0, 0)),
                  pl.BlockSpec((1, D), lambda i, ps: (0, 0)),
                  pl.BlockSpec(memory_space=pl.ANY)],
        out_specs=pl.BlockSpec((tm, D), lambda i, ps: (i, 0)),
        scratch_shapes=[pltpu.VMEM((tm, D), F32), pltpu.VMEM((tm, D), F32), pltpu.SemaphoreType.DMA(())],
    )
    return pl.pallas_call(
        functools.partial(_combine_kernel, final=final),
        grid_spec=grid_spec,
        out_shape=jax.ShapeDtypeStruct((N, D), F32),
        compiler_params=_cparams(("arbitrary",), 32),
        name="moe_combine",
    )(pstart, ri, rw_t, x1, gt, final_g, ys)


def _moe_plan(cnt):
    G, S = N_EXPERT_GROUPS, EXPERTS_PER_GROUP
    counts = cnt[:, 0].reshape(S, G).T.reshape(G * S)
    padded = (counts + MOE_ROWS - 1) // MOE_ROWS * MOE_ROWS
    pend = jnp.cumsum(padded)
    return counts, (pend - padded).astype(I32), pend


def _block_diag(w):
    nb, bw, _ = w.shape
    eye = jnp.eye(nb, dtype=w.dtype)
    return jnp.einsum('nij,nm->nimj', w, eye).reshape(nb * bw, nb * bw)


def kernel(x, c, ada_w, ada_b, norm_mix_g, norm_ffn_g, w_in, w_branch_gate, b_branch_gate,
           na_rpb, lru_conv_w, lru_conv_b, lru_w_r, lru_b_r, lru_w_i, lru_b_i, lru_lambda,
           w_proj_na, w_proj_lru, w_proj_fnet, w_out, w_router, router_bias,
           w_exp_gate, w_exp_up, w_exp_down, final_g):
    B, T, D = x.shape
    L = ada_w.shape[0]
    N = B * T
    E = N_EXPERTS
    G, S = N_EXPERT_GROUPS, EXPERTS_PER_GROUP
    A = N * TOP_K
    nb = A // MOE_ROWS + E
    P = nb * MOE_ROWS

    mod = _ada_mod(c, ada_w, ada_b)
    m1_np, cs_np = _fnet_tables(T)
    m1 = jnp.asarray(m1_np).astype(BF16)
    cs = jnp.asarray(cs_np).astype(BF16)
    tri = (lax.broadcasted_iota(I32, (TM_MERGE, TM_MERGE), 0)
           <= lax.broadcasted_iota(I32, (TM_MERGE, TM_MERGE), 1)).astype(BF16)
    wr = w_router.T.reshape(G, S, D).transpose(1, 0, 2).reshape(E, D)
    rb = router_bias.reshape(G, S).T.reshape(E, 1)
    fg = final_g.reshape(1, D)

    x2 = x.reshape(N, D)
    for l in range(L):
        sh_mix, sc_mix, gt_mix, sh_ffn, sc_ffn, gt_ffn = [m.reshape(B, 1, D) for m in jnp.split(mod[l], 6, axis=-1)]
        wcat = jnp.concatenate([w_in[l], w_branch_gate[l]], axis=1).astype(BF16)
        q, k, v, xr, gy, fr, gates = _mix_in(x2, norm_mix_g[l].reshape(1, D), sh_mix, sc_mix, wcat,
                                             b_branch_gate[l].reshape(1, -1), B, T)
        y_na = _na(q, k, v, _na_bias_table(na_rpb[l]), B, T)

        wg = jnp.stack([jnp.concatenate([_block_diag(lru_w_r[l, d]), _block_diag(lru_w_i[l, d])], axis=1)
                        for d in range(2)]).astype(BF16)
        bgl = jnp.stack([jnp.concatenate([lru_b_r[l, d], lru_b_i[l, d]]) for d in range(2)])[:, None, :]
        af, bf, ab, bb = _lru_gates(xr, lru_conv_w[l], lru_conv_b[l].reshape(1, -1), wg, bgl,
                                    lru_lambda[l][:, None, :], B, T)
        y_lru = _lru_scans(af, bf, ab, bb, gy)
        y_f = _fnet(fr, m1, cs, B, T)

        x1, u2, ri, rw, cnt = _merge(gates, y_na, y_lru, y_f, x2,
                                     w_proj_na[l].astype(BF16), w_proj_lru[l].astype(BF16),
                                     w_proj_fnet[l].astype(BF16), w_out[l].astype(BF16),
                                     gt_mix, norm_ffn_g[l].reshape(1, D), sh_ffn, sc_ffn, wr, rb, tri, B, T)
        _, pstart, pend = _moe_plan(cnt)
        block_e = jnp.minimum(jnp.searchsorted(pend, jnp.arange(nb, dtype=I32) * MOE_ROWS, side='right'),
                              E - 1).astype(I32)
        n_used = (pend[-1:] // MOE_ROWS).astype(I32)
        xs = _dispatch(pstart, ri, u2, P)
        ys = _experts(block_e, n_used, xs, w_exp_gate[l], w_exp_up[l], w_exp_down[l])
        x2 = _combine(pstart, ri, rw.T, x1, gt_ffn, fg, ys, T, final=(l == L - 1))
    return x2.reshape(B, T, D)
```

```python
import functools

import numpy as np
import jax
import jax.numpy as jnp
from jax import lax
from jax.experimental import pallas as pl
from jax.experimental.pallas import tpu as pltpu

F32 = jnp.float32
BF16 = jnp.bfloat16
I32 = jnp.int32

GRID_W = 64
NA_HEAD_DIM = 64
NA_WIN_ROWS = 8
NA_WIN_COLS = 16
NA_HEADS_PER_TILE = 4
LRU_BLOCKS = 8
LRU_C = 8.0
CONV_W = 4
FNET_GROUP_W = 128
N_EXPERTS = 32
N_EXPERT_GROUPS = 8
EXPERTS_PER_GROUP = 4
TOP_K = 2
EPS = 1e-6
MASK_VALUE = -1e30

V7X_VMEM_BYTES = 64 * 1024 * 1024
LANES = 128
SUBLANES = 8

TM_MIX = 1024
TN_MIX = 1024
MIX_ROW_CHUNK = 256
NA_ROWS_PER_STEP = 8
TM_LRU = 512
T_SCAN = 512
TM_MERGE = 512
TM_MOE = 512
MOE_ROWS = 256


def _cparams(semantics, vmem_mb):
    assert vmem_mb * 1024 * 1024 < V7X_VMEM_BYTES
    return pltpu.CompilerParams(dimension_semantics=semantics, vmem_limit_bytes=vmem_mb * 1024 * 1024)


def _split2(a):
    hi = a.astype(BF16)
    lo = (a - hi.astype(F32)).astype(BF16)
    return hi, lo


def _dot_hi(a, b, dims):
    a_hi, a_lo = _split2(a)
    b_hi, b_lo = _split2(b)
    dn = (dims, ((), ()))
    out = lax.dot_general(a_hi, b_hi, dn, preferred_element_type=F32)
    out = out + lax.dot_general(a_hi, b_lo, dn, preferred_element_type=F32)
    out = out + lax.dot_general(a_lo, b_hi, dn, preferred_element_type=F32)
    return out


def _rmsnorm_mod(x, g, sh, sc):
    y = x * lax.rsqrt(jnp.mean(x * x, axis=-1, keepdims=True) + EPS)
    return (y * g) * (1.0 + sc) + sh


def _ada_kernel(c_ref, w_ref, b_ref, o_ref):
    c = c_ref[...]
    ca = c * jax.nn.sigmoid(c)
    o_ref[0] = _dot_hi(ca, w_ref[0], ((1,), (0,))) + b_ref[0]


def _ada_mod(c, ada_w, ada_b):
    L, D, D6 = ada_w.shape
    B = c.shape[0]
    bp = -(-B // SUBLANES) * SUBLANES
    cp = jnp.pad(c, ((0, bp - B), (0, 0)))
    tn = D6 // 4
    out = pl.pallas_call(
        _ada_kernel,
        grid=(L, D6 // tn),
        in_specs=[pl.BlockSpec((bp, D), lambda l, j: (0, 0)),
                  pl.BlockSpec((1, D, tn), lambda l, j: (l, 0, j)),
                  pl.BlockSpec((1, 1, tn), lambda l, j: (l, 0, j))],
        out_specs=pl.BlockSpec((1, bp, tn), lambda l, j: (l, 0, j)),
        out_shape=jax.ShapeDtypeStruct((L, bp, D6), F32),
        compiler_params=_cparams(("parallel", "parallel"), 40),
        name="ada_mod",
    )(cp, ada_w, ada_b.reshape(L, 1, D6))
    return out[:, :B]


def _mixin_kernel(x_ref, g_ref, sh_ref, sc_ref, w_ref, bg_ref,
                  q_ref, k_ref, v_ref, xr_ref, gy_ref, fr_ref, gate_ref, u_sc):
    j = pl.program_id(1)

    @pl.when(j == 0)
    def _():
        u_sc[...] = _rmsnorm_mod(x_ref[...], g_ref[...], sh_ref[0], sc_ref[0]).astype(BF16)

    tm = u_sc.shape[0]
    W = q_ref.shape[1]
    lo, hi = slice(0, W), slice(W, 2 * W)

    def per_chunk(store):
        for c in range(tm // MIX_ROW_CHUNK):
            rows = slice(c * MIX_ROW_CHUNK, (c + 1) * MIX_ROW_CHUNK)
            store(rows, jnp.dot(u_sc[rows, :], w_ref[...], preferred_element_type=F32))

    @pl.when(j == 0)
    def _():
        def store(rows, acc):
            q_ref[rows, :] = acc[:, lo].astype(BF16)
            k_ref[rows, :] = acc[:, hi].astype(BF16)
        per_chunk(store)

    @pl.when(j == 1)
    def _():
        def store(rows, acc):
            v_ref[rows, :] = acc[:, lo].astype(BF16)
            xr_ref[rows, :] = acc[:, hi]
        per_chunk(store)

    @pl.when(j == 2)
    def _():
        def store(rows, acc):
            gy_ref[rows, :] = jax.nn.gelu(acc[:, lo]).astype(BF16)
            fr_ref[rows, :] = acc[:, hi].astype(BF16)
        per_chunk(store)

    @pl.when(j >= 3)
    def _():
        def store(rows, acc):
            gate_ref[rows, :] = jax.nn.sigmoid(acc + bg_ref[...]).astype(BF16)
        per_chunk(store)


def _mix_in(x2, g, sh, sc, wcat, bgate, B, T):
    N, D = x2.shape
    tm, tn = TM_MIX, TN_MIX
    tpb = T // tm
    W = tn // 2
    n_proj = 3
    n_gate = bgate.shape[1] // tn
    tok = lambda i, j: (i, 0)
    scan = lambda i, j: (i % tpb, i // tpb)
    per_b = lambda i, j: (i // tpb, 0, 0)
    out_shape = [jax.ShapeDtypeStruct((N, W), BF16)] * 3 + [
        jax.ShapeDtypeStruct((N, W), F32),
        jax.ShapeDtypeStruct((T, B * W), BF16),
        jax.ShapeDtypeStruct((N, W), BF16),
        jax.ShapeDtypeStruct((N, n_gate * tn), BF16)]
    out_specs = [pl.BlockSpec((tm, W), tok)] * 3 + [
        pl.BlockSpec((tm, W), tok),
        pl.BlockSpec((tm, W), scan),
        pl.BlockSpec((tm, W), tok),
        pl.BlockSpec((tm, tn), lambda i, j: (i, jnp.maximum(j - n_proj, 0)))]
    return pl.pallas_call(
        _mixin_kernel,
        grid=(N // tm, n_proj + n_gate),
        in_specs=[pl.BlockSpec((tm, D), tok),
                  pl.BlockSpec((1, D), lambda i, j: (0, 0)),
                  pl.BlockSpec((1, 1, D), per_b),
                  pl.BlockSpec((1, 1, D), per_b),
                  pl.BlockSpec((D, tn), lambda i, j: (0, j)),
                  pl.BlockSpec((1, tn), lambda i, j: (0, jnp.maximum(j - n_proj, 0)))],
        out_specs=out_specs,
        out_shape=out_shape,
        scratch_shapes=[pltpu.VMEM((tm, D), BF16)],
        compiler_params=_cparams(("parallel", "arbitrary"), 48),
        name="mix_in",
    )(x2, g, sh, sc, wcat, bgate)


def _na_kernel(q_ref, kp_ref, kc_ref, kn_ref, vp_ref, vc_ref, vn_ref, bias_ref, o_ref, kb, vb):
    j = pl.program_id(1)
    nj = pl.num_programs(1)
    R = NA_ROWS_PER_STEP
    half = R // 2
    W = GRID_W
    hw = half * W
    tq = R * W
    win = NA_WIN_ROWS * W
    HT = NA_HEADS_PER_TILE
    lt = HT * NA_HEAD_DIM
    kb[0:hw] = kp_ref[...]
    kb[hw:hw + tq] = kc_ref[...]
    kb[hw + tq:2 * hw + tq] = kn_ref[...]
    vb[0:hw] = vp_ref[...]
    vb[hw:hw + tq] = vc_ref[...]
    vb[hw + tq:2 * hw + tq] = vn_ref[...]
    head_of_lane = lax.broadcasted_iota(I32, (W, lt), 1) // NA_HEAD_DIM
    n_tiles = q_ref.shape[1] // lt
    scale = jnp.asarray(NA_HEAD_DIM ** -0.5, BF16)
    for dr in range(R):
        lo = jnp.where(j == 0, max(dr, half), jnp.where(j == nj - 1, min(dr, half), dr))
        var = lo - dr + (half - 1)
        start = pl.multiple_of(lo * W, W)
        for ht in range(n_tiles):
            cs = slice(ht * lt, (ht + 1) * lt)
            qrow = q_ref[dr * W:(dr + 1) * W, cs] * scale
            qbd = jnp.concatenate(
                [jnp.where(head_of_lane == h, qrow, jnp.zeros_like(qrow)) for h in range(HT)], axis=0)
            kwin = kb[pl.ds(start, win), cs]
            s = lax.dot_general(qbd, kwin, (((1,), (1,)), ((), ())), preferred_element_type=F32)
            s = s + bias_ref[var, ht]
            m = jnp.max(s, axis=-1, keepdims=True)
            p = jnp.exp(s - m)
            l = jnp.sum(p, axis=-1, keepdims=True)
            vwin = vb[pl.ds(start, win), cs]
            o = jnp.dot(p.astype(BF16), vwin, preferred_element_type=F32) / l
            orow = jnp.where(head_of_lane == 0, o[0:W], 0.0)
            for h in range(1, HT):
                orow = orow + jnp.where(head_of_lane == h, o[h * W:(h + 1) * W], 0.0)
            o_ref[dr * W:(dr + 1) * W, cs] = orow.astype(BF16)


def _na_bias_table(rpb):
    H = rpb.shape[0]
    W, KR, KC = GRID_W, NA_WIN_ROWS, NA_WIN_COLS
    w = np.arange(W)
    cstart = np.clip(w - KC // 2, 0, W - KC)
    wk = np.arange(W)
    col_ok = (wk[None, :] >= cstart[:, None]) & (wk[None, :] < cstart[:, None] + KC)
    cb = wk[None, :] - w[:, None] + (KC - 1)
    onehot = ((cb[None] == np.arange(2 * KC - 1)[:, None, None]) & col_ok[None]).astype(np.float32)
    toep = jnp.einsum('hac,cwx->hawx', rpb.astype(F32), jnp.asarray(onehot), precision=lax.Precision.HIGHEST)
    toep = jnp.where(jnp.asarray(col_ok)[None, None], toep, MASK_VALUE)
    tab = jnp.stack([toep[:, v:v + KR] for v in range(KR)])
    tab = tab.transpose(0, 1, 3, 2, 4)
    HT = NA_HEADS_PER_TILE
    return tab.reshape(KR, H // HT, HT * W, KR * W)


def _na(q, k, v, bias, B, T):
    N, C = q.shape
    R = NA_ROWS_PER_STEP
    tq = R * GRID_W
    hw = tq // 2
    nj = T // tq
    nh = T // hw
    qmap = lambda b, j: (b * nj + j, 0)
    pmap = lambda b, j: (b * nh + jnp.maximum(2 * j - 1, 0), 0)
    nmap = lambda b, j: (b * nh + jnp.minimum(2 * j + 2, nh - 1), 0)
    return pl.pallas_call(
        _na_kernel,
        grid=(B, nj),
        in_specs=[pl.BlockSpec((tq, C), qmap),
                  pl.BlockSpec((hw, C), pmap), pl.BlockSpec((tq, C), qmap), pl.BlockSpec((hw, C), nmap),
                  pl.BlockSpec((hw, C), pmap), pl.BlockSpec((tq, C), qmap), pl.BlockSpec((hw, C), nmap),
                  pl.BlockSpec(bias.shape, lambda b, j: (0, 0, 0, 0))],
        out_specs=pl.BlockSpec((tq, C), qmap),
        out_shape=jax.ShapeDtypeStruct((N, C), BF16),
        scratch_shapes=[pltpu.VMEM((2 * tq, C), BF16), pltpu.VMEM((2 * tq, C), BF16)],
        compiler_params=_cparams(("parallel", "parallel"), 48),
        name="na_attention",
    )(q, k, k, k, v, v, v, bias)


def _lru_gate_kernel(x_ref, xp_ref, xn_ref, cw_ref, cb_ref, wg_ref, bg_ref, lam_ref,
                     af_ref, bf_ref, ab_ref, bb_ref, xpad, *, tpb):
    i = pl.program_id(0)
    tm, C = x_ref.shape
    H = SUBLANES
    first = (i % tpb) == 0
    last = (i % tpb) == tpb - 1
    xpad[0:H] = jnp.where(first, 0.0, xp_ref[...])
    xpad[H:H + tm] = x_ref[...]
    xpad[H + tm:2 * H + tm] = jnp.where(last, 0.0, xn_ref[...])
    left = CONV_W // 2
    xc = xpad[H - left:H - left + tm] * cw_ref[0:1]
    for t in range(1, CONV_W):
        xc = xc + xpad[H - left + t:H - left + t + tm] * cw_ref[t:t + 1]
    xc = xc + cb_ref[...]
    xcb = xc.astype(BF16)
    for d, (a_ref, b_ref) in enumerate(((af_ref, bf_ref), (ab_ref, bb_ref))):
        g = jnp.dot(xcb, wg_ref[d], preferred_element_type=F32) + bg_ref[d]
        r = jax.nn.sigmoid(g[:, :C])
        ig = jax.nn.sigmoid(g[:, C:])
        log_a = -LRU_C * r * jax.nn.softplus(-lam_ref[d])
        a = jnp.exp(log_a)
        a_ref[...] = a
        b_ref[...] = jnp.sqrt(1.0 - a * a) * (ig * xc)


def _lru_gates(xr, conv_w, conv_b, wg, bg, lam, B, T):
    N, C = xr.shape
    tm = TM_LRU
    tpb = T // tm
    nb8 = N // SUBLANES
    r8 = tm // SUBLANES
    scan = lambda i: (i % tpb, i // tpb)
    const2 = lambda i: (0, 0)
    const3 = lambda i: (0, 0, 0)
    outs = pl.pallas_call(
        functools.partial(_lru_gate_kernel, tpb=tpb),
        grid=(N // tm,),
        in_specs=[pl.BlockSpec((tm, C), lambda i: (i, 0)),
                  pl.BlockSpec((SUBLANES, C), lambda i: (jnp.maximum(i * r8 - 1, 0), 0)),
                  pl.BlockSpec((SUBLANES, C), lambda i: (jnp.minimum((i + 1) * r8, nb8 - 1), 0)),
                  pl.BlockSpec((CONV_W, C), const2),
                  pl.BlockSpec((1, C), const2),
                  pl.BlockSpec((2, C, 2 * C), const3),
                  pl.BlockSpec((2, 1, 2 * C), const3),
                  pl.BlockSpec((2, 1, C), const3)],
        out_specs=[pl.BlockSpec((tm, C), scan)] * 4,
        out_shape=[jax.ShapeDtypeStruct((T, B * C), F32)] * 4,
        scratch_shapes=[pltpu.VMEM((tm + 2 * SUBLANES, C), F32)],
        compiler_params=_cparams(("parallel",), 48),
        name="lru_gates",
    )(xr, xr, xr, conv_w, conv_b, wg, bg, lam)
    return outs


def _scan_bwd_kernel(a_ref, b_ref, h_ref, carry):
    @pl.when(pl.program_id(0) == 0)
    def _():
        carry[...] = jnp.zeros_like(carry)

    tc = a_ref.shape[0]

    def body(s, h):
        t = tc - 1 - s
        h = a_ref[t] * h + b_ref[t]
        h_ref[t] = h
        return h

    carry[...] = lax.fori_loop(0, tc, body, carry[...], unroll=8)


def _scan_fwd_kernel(a_ref, b_ref, hb_ref, gy_ref, y_ref, carry):
    @pl.when(pl.program_id(0) == 0)
    def _():
        carry[...] = jnp.zeros_like(carry)

    tc = a_ref.shape[0]

    def body(t, h):
        h = a_ref[t] * h + b_ref[t]
        y_ref[t] = ((h + hb_ref[t]) * gy_ref[t].astype(F32)).astype(BF16)
        return h

    carry[...] = lax.fori_loop(0, tc, body, carry[...], unroll=8)


def _lru_scans(af, bf, ab, bb, gy):
    T, BC = af.shape
    S = BC // LANES
    tc = T_SCAN
    nc = T // tc
    v3 = lambda z: z.reshape(T, S, LANES)
    blk = (tc, S, LANES)
    fwd = lambda c: (c, 0, 0)
    rev = lambda c: (nc - 1 - c, 0, 0)
    hb = pl.pallas_call(
        _scan_bwd_kernel,
        grid=(nc,),
        in_specs=[pl.BlockSpec(blk, rev), pl.BlockSpec(blk, rev)],
        out_specs=pl.BlockSpec(blk, rev),
        out_shape=jax.ShapeDtypeStruct((T, S, LANES), F32),
        scratch_shapes=[pltpu.VMEM((S, LANES), F32)],
        compiler_params=_cparams(("arbitrary",), 48),
        name="lru_scan_bwd",
    )(v3(ab), v3(bb))
    y = pl.pallas_call(
        _scan_fwd_kernel,
        grid=(nc,),
        in_specs=[pl.BlockSpec(blk, fwd)] * 4,
        out_specs=pl.BlockSpec(blk, fwd),
        out_shape=jax.ShapeDtypeStruct((T, S, LANES), BF16),
        scratch_shapes=[pltpu.VMEM((S, LANES), F32)],
        compiler_params=_cparams(("arbitrary",), 48),
        name="lru_scan_fwd",
    )(v3(af), v3(bf), hb, v3(gy))
    return y.reshape(T, BC)


def _fnet_tables(T):
    T2 = FNET_GROUP_W
    T1 = T // T2
    k1 = np.arange(T1)[:, None]
    t1 = np.arange(T1)[None, :]
    stage1 = np.zeros((T2, 2 * T1, 2 * T1), np.float32)
    for t2 in range(T2):
        ph = 2.0 * np.pi * ((k1 * (T2 * t1 + t2)) % T) / T
        c, s = np.cos(ph), np.sin(ph)
        stage1[t2] = np.block([[c, s], [-s, c]])
    n = np.arange(T2)
    ph2 = 2.0 * np.pi * ((n[:, None] * n[None, :]) % T2) / T2
    c2, s2 = np.cos(ph2).astype(np.float32), np.sin(ph2).astype(np.float32)
    return stage1, np.concatenate([c2, s2], axis=1)


def _fnet_kernel(x_ref, m1_ref, cs_ref, o_ref, zr, zi, yo, *, T1):
    T2 = FNET_GROUP_W
    T = T1 * T2
    cs = cs_ref[...]
    for k1 in range(T1):
        blk = slice(k1 * T2, (k1 + 1) * T2)
        z = jnp.dot(x_ref[blk, :], cs, preferred_element_type=F32)
        zr[blk, :] = z[:, :T2]
        zi[blk, :] = -z[:, T2:]
    for t2 in range(T2):
        rows = pl.ds(t2, T1, stride=T2)
        zin = jnp.concatenate([zr[rows, :], zi[rows, :]], axis=0).astype(BF16)
        a = jnp.dot(m1_ref[t2], zin, preferred_element_type=F32)
        zr[rows, :] = a[:T1]
        zi[rows, :] = a[T1:]
    inv = 1.0 / np.sqrt(float(T) * T2)
    for k1 in range(T1):
        blk = slice(k1 * T2, (k1 + 1) * T2)
        ain = jnp.concatenate([zr[blk, :], zi[blk, :]], axis=0).astype(BF16)
        y = jnp.dot(cs, ain, preferred_element_type=F32)
        yo[pl.ds(k1, T2, stride=T1), :] = y * inv
    o_ref[...] = yo[...].astype(BF16)


def _fnet(fr, m1, cs, B, T):
    N, C = fr.shape
    G = C // FNET_GROUP_W
    T1 = T // FNET_GROUP_W
    blk = pl.BlockSpec((T, FNET_GROUP_W), lambda b, g: (b, g))
    return pl.pallas_call(
        functools.partial(_fnet_kernel, T1=T1),
        grid=(B, G),
        in_specs=[blk,
                  pl.BlockSpec(m1.shape, lambda b, g: (0, 0, 0)),
                  pl.BlockSpec(cs.shape, lambda b, g: (0, 0))],
        out_specs=blk,
        out_shape=jax.ShapeDtypeStruct((N, C), BF16),
        scratch_shapes=[pltpu.VMEM((T, FNET_GROUP_W), F32)] * 3,
        compiler_params=_cparams(("parallel", "parallel"), 48),
        name="fnet",
    )(fr, m1, cs)


def _merge_kernel(gate_ref, yna_ref, ylru_ref, yf_ref, x_ref, wna_ref, wlru_ref, wf_ref, wout_ref,
                  gt_ref, g_ref, sh_ref, sc_ref, wr_ref, rb_ref, tri_ref,
                  x1_ref, u2_ref, ri_ref, rw_ref, cnt_ref, cnt_sc):
    i = pl.program_id(0)

    @pl.when(i == 0)
    def _():
        cnt_sc[...] = jnp.zeros_like(cnt_sc)

    tm, D = x_ref.shape
    gates = gate_ref[...]
    merged = gates[:, 0:D].astype(F32) * jnp.dot(yna_ref[...], wna_ref[...], preferred_element_type=F32)
    merged = merged + gates[:, D:2 * D].astype(F32) * jnp.dot(ylru_ref[...], wlru_ref[...], preferred_element_type=F32)
    merged = merged + gates[:, 2 * D:3 * D].astype(F32) * jnp.dot(yf_ref[...], wf_ref[...], preferred_element_type=F32)
    out = jnp.dot(merged.astype(BF16), wout_ref[...], preferred_element_type=F32)
    x1 = x_ref[...] + gt_ref[0] * out
    x1_ref[...] = x1
    u2 = _rmsnorm_mod(x1, g_ref[...], sh_ref[0], sc_ref[0])
    u2_ref[...] = u2

    G, S = N_EXPERT_GROUPS, EXPERTS_PER_GROUP
    logits = _dot_hi(wr_ref[...], u2, ((1,), (1,)))
    score = jax.nn.sigmoid(logits)
    sel = score + rb_ref[...]
    sel_s = [sel[s * G:(s + 1) * G] for s in range(S)]
    sc_s = [score[s * G:(s + 1) * G] for s in range(S)]
    gscore = None
    for a in range(S):
        for b in range(a + 1, S):
            pair = sel_s[a] + sel_s[b]
            gscore = pair if gscore is None else jnp.maximum(gscore, pair)
    giota = lax.broadcasted_iota(I32, (G, tm), 0)
    gmax = jnp.max(gscore, axis=0, keepdims=True)
    gidx = jnp.min(jnp.where(gscore == gmax, giota, G), axis=0, keepdims=True)
    gm = giota == gidx
    cand = [jnp.sum(jnp.where(gm, sel_s[s], 0.0), axis=0, keepdims=True) for s in range(S)]
    raw = [jnp.sum(jnp.where(gm, sc_s[s], 0.0), axis=0, keepdims=True) for s in range(S)]

    def first_argmax(vals):
        m = vals[0]
        for s in range(1, S):
            m = jnp.maximum(m, vals[s])
        idx = jnp.full(m.shape, S - 1, I32)
        for s in range(S - 2, -1, -1):
            idx = jnp.where(vals[s] == m, s, idx)
        return idx

    l0 = first_argmax(cand)
    l1 = first_argmax([jnp.where(l0 == s, -jnp.inf, cand[s]) for s in range(S)])

    def pick(vals, idx):
        out = vals[S - 1]
        for s in range(S - 2, -1, -1):
            out = jnp.where(idx == s, vals[s], out)
        return out

    w0 = pick(raw, l0)
    w1 = pick(raw, l1)
    wsum = w0 + w1
    w0 = w0 / wsum
    w1 = w1 / wsum
    e0 = gidx * S + l0
    e1 = gidx * S + l1

    E = G * S
    eiota = lax.broadcasted_iota(I32, (E, tm), 0)
    oh0 = eiota == (l0 * G + gidx)
    oh1 = eiota == (l1 * G + gidx)
    oh = jnp.where(oh0 | oh1, 1.0, 0.0)
    incl = jnp.dot(oh.astype(BF16), tri_ref[...], preferred_element_type=F32)
    before = incl - oh + cnt_sc[:, 0:1]
    p0 = jnp.sum(jnp.where(oh0, before, 0.0), axis=0, keepdims=True).astype(I32)
    p1 = jnp.sum(jnp.where(oh1, before, 0.0), axis=0, keepdims=True).astype(I32)
    cnt_sc[...] = cnt_sc[...] + jnp.sum(oh, axis=1, keepdims=True)
    cnt_ref[...] = cnt_sc[...].astype(I32)

    row = lax.broadcasted_iota(I32, (SUBLANES, tm), 0)
    ri_ref[...] = jnp.where(row == 0, e0, jnp.where(row == 1, e1, jnp.where(row == 2, p0, jnp.where(row == 3, p1, 0))))
    rw_ref[...] = jnp.where(row == 0, w0, jnp.where(row == 1, w1, 0.0))


def _merge(gates, y_na, y_lru, y_f, x2, wna, wlru, wf, wout, gt, g, sh, sc, wr, rb, tri, B, T):
    N, D = x2.shape
    C = y_na.shape[1]
    tm = TM_MERGE
    tpb = T // tm
    E = wr.shape[0]
    tok = lambda i: (i, 0)
    scan = lambda i: (i % tpb, i // tpb)
    per_b = lambda i: (i // tpb, 0, 0)
    const2 = lambda i: (0, 0)
    return pl.pallas_call(
        _merge_kernel,
        grid=(N // tm,),
        in_specs=[pl.BlockSpec((tm, 3 * D), tok),
                  pl.BlockSpec((tm, C), tok),
                  pl.BlockSpec((tm, C), scan),
                  pl.BlockSpec((tm, C), tok),
                  pl.BlockSpec((tm, D), tok),
                  pl.BlockSpec((C, D), const2), pl.BlockSpec((C, D), const2), pl.BlockSpec((C, D), const2),
                  pl.BlockSpec((D, D), const2),
                  pl.BlockSpec((1, 1, D), per_b),
                  pl.BlockSpec((1, D), const2),
                  pl.BlockSpec((1, 1, D), per_b),
                  pl.BlockSpec((1, 1, D), per_b),
                  pl.BlockSpec((E, D), const2),
                  pl.BlockSpec((E, 1), const2),
                  pl.BlockSpec((tm, tm), const2)],
        out_specs=[pl.BlockSpec((tm, D), tok),
                   pl.BlockSpec((tm, D), tok),
                   pl.BlockSpec((SUBLANES, tm), lambda i: (0, i)),
                   pl.BlockSpec((SUBLANES, tm), lambda i: (0, i)),
                   pl.BlockSpec((E, LANES), const2)],
        out_shape=[jax.ShapeDtypeStruct((N, D), F32),
                   jax.ShapeDtypeStruct((N, D), F32),
                   jax.ShapeDtypeStruct((SUBLANES, N), I32),
                   jax.ShapeDtypeStruct((SUBLANES, N), F32),
                   jax.ShapeDtypeStruct((E, LANES), I32)],
        scratch_shapes=[pltpu.VMEM((E, LANES), F32)],
        compiler_params=_cparams(("arbitrary",), 48),
        name="merge_route",
    )(gates, y_na, y_lru, y_f, x2, wna, wlru, wf, wout, gt, g, sh, sc, wr, rb, tri)


def _row_copy(src, s, dst, d, sem):
    return pltpu.make_async_copy(src.at[pl.ds(s, 1)], dst.at[pl.ds(d, 1)], sem)


def _dispatch_kernel(ps_ref, pad_ref, ri_ref, u_ref, xs_out, zbuf, sem):
    tm = u_ref.shape[0]

    @pl.when(pl.program_id(0) == 0)
    def _():
        zbuf[...] = jnp.zeros_like(zbuf)
        def zero_copy(e):
            at = pl.multiple_of(pad_ref[e], SUBLANES)
            return pltpu.make_async_copy(zbuf, xs_out.at[pl.ds(at, MOE_ROWS)], sem)

        for e in range(N_EXPERTS):
            zero_copy(e).start()
        for e in range(N_EXPERTS):
            zero_copy(e).wait()

    def body(t, c):
        for k in range(TOP_K):
            d = ps_ref[ri_ref[k, t]] + ri_ref[TOP_K + k, t]
            _row_copy(u_ref, t, xs_out, d, sem).start()
        return c

    lax.fori_loop(0, tm, body, 0, unroll=8)

    for k in range(TOP_K):
        pltpu.make_async_copy(u_ref, xs_out.at[pl.ds(0, tm)], sem).wait()


def _dispatch(pstart, pad_at, ri, u2, P):
    N, D = u2.shape
    tm = TM_MOE
    grid_spec = pltpu.PrefetchScalarGridSpec(
        num_scalar_prefetch=2,
        grid=(N // tm,),
        in_specs=[pl.BlockSpec((SUBLANES, tm), lambda i, ps, pa: (0, i), memory_space=pltpu.SMEM),
                  pl.BlockSpec((tm, D), lambda i, ps, pa: (i, 0))],
        out_specs=pl.BlockSpec(memory_space=pl.ANY),
        scratch_shapes=[pltpu.VMEM((MOE_ROWS, D), F32), pltpu.SemaphoreType.DMA(())],
    )
    return pl.pallas_call(
        _dispatch_kernel,
        grid_spec=grid_spec,
        out_shape=jax.ShapeDtypeStruct((P, D), F32),
        compiler_params=_cparams(("arbitrary",), 32),
        name="moe_dispatch",
    )(pstart, pad_at, ri, u2)


def _expert_kernel(be_ref, nu_ref, xs_ref, wg_ref, wu_ref, wd_ref, ys_ref, wg_sc, wu_sc, wd_sc):
    i = pl.program_id(0)
    prev = be_ref[jnp.maximum(i - 1, 0)]
    fresh = (i == 0) | (be_ref[i] != prev)

    @pl.when(fresh)
    def _():
        wg_sc[...] = wg_ref[0, 0].astype(BF16)
        wu_sc[...] = wu_ref[0, 0].astype(BF16)
        wd_sc[...] = wd_ref[0, 0].astype(BF16)

    @pl.when(i < nu_ref[0])
    def _():
        xb = xs_ref[...].astype(BF16)
        hg = jnp.dot(xb, wg_sc[...], preferred_element_type=F32)
        hu = jnp.dot(xb, wu_sc[...], preferred_element_type=F32)
        h = (hg * jax.nn.sigmoid(hg)) * hu
        ys_ref[...] = jnp.dot(h.astype(BF16), wd_sc[...], preferred_element_type=F32)

    @pl.when(i >= nu_ref[0])
    def _():
        ys_ref[...] = jnp.zeros_like(ys_ref)


def _experts(block_e, n_used, xs, w_gate, w_up, w_down, layer):
    P, D = xs.shape
    _, E, _, DE = w_gate.shape
    nb = P // MOE_ROWS
    wmap = lambda i, be, nu: (layer, be[i], 0, 0)
    grid_spec = pltpu.PrefetchScalarGridSpec(
        num_scalar_prefetch=2,
        grid=(nb,),
        in_specs=[pl.BlockSpec((MOE_ROWS, D), lambda i, be, nu: (i, 0)),
                  pl.BlockSpec((1, 1, D, DE), wmap),
                  pl.BlockSpec((1, 1, D, DE), wmap),
                  pl.BlockSpec((1, 1, DE, D), wmap)],
        out_specs=pl.BlockSpec((MOE_ROWS, D), lambda i, be, nu: (i, 0)),
        scratch_shapes=[pltpu.VMEM((D, DE), BF16), pltpu.VMEM((D, DE), BF16), pltpu.VMEM((DE, D), BF16)],
    )
    return pl.pallas_call(
        _expert_kernel,
        grid_spec=grid_spec,
        out_shape=jax.ShapeDtypeStruct((P, D), F32),
        compiler_params=_cparams(("arbitrary",), 48),
        name="moe_experts",
    )(block_e, n_used, xs, w_gate, w_up, w_down)


def _combine_kernel(ps_ref, ri_ref, rw_ref, x_ref, gt_ref, fg_ref, ys_ref, o_ref, g0, g1, sem, *, final):
    tm = x_ref.shape[0]
    bufs = (g0, g1)

    def body(t, c):
        for k in range(TOP_K):
            d = ps_ref[ri_ref[k, t]] + ri_ref[TOP_K + k, t]
            _row_copy(ys_ref, d, bufs[k], t, sem).start()
        return c

    lax.fori_loop(0, tm, body, 0, unroll=8)

    for k in range(TOP_K):
        pltpu.make_async_copy(ys_ref.at[pl.ds(0, tm)], bufs[k], sem).wait()
    w = rw_ref[...]
    y = g0[...] * w[:, 0:1] + g1[...] * w[:, 1:2]
    x2 = x_ref[...] + gt_ref[0] * y
    if final:
        x2 = (x2 * lax.rsqrt(jnp.mean(x2 * x2, axis=-1, keepdims=True) + EPS)) * fg_ref[...]
    o_ref[...] = x2


def _combine(pstart, ri, rw_t, x1, gt, final_g, ys, T, final):
    N, D = x1.shape
    tm = TM_MOE
    tpb = T // tm
    grid_spec = pltpu.PrefetchScalarGridSpec(
        num_scalar_prefetch=1,
        grid=(N // tm,),
        in_specs=[pl.BlockSpec((SUBLANES, tm), lambda i, ps: (0, i), memory_space=pltpu.SMEM),
                  pl.BlockSpec((tm, SUBLANES), lambda i, ps: (i, 0)),
                  pl.BlockSpec((tm, D), lambda i, ps: (i, 0)),
                  pl.BlockSpec((1, 1, D), lambda i, ps: (i // tpb, 0, 0)),
                  pl.BlockSpec((1, D), lambda i, ps: (0, 0)),
                  pl.BlockSpec(memory_space=pl.ANY)],
        out_specs=pl.BlockSpec((tm, D), lambda i, ps: (i, 0)),
        scratch_shapes=[pltpu.VMEM((tm, D), F32), pltpu.VMEM((tm, D), F32), pltpu.SemaphoreType.DMA(())],
    )
    return pl.pallas_call(
        functools.partial(_combine_kernel, final=final),
        grid_spec=grid_spec,
        out_shape=jax.ShapeDtypeStruct((N, D), F32),
        compiler_params=_cparams(("arbitrary",), 32),
        name="moe_combine",
    )(pstart, ri, rw_t, x1, gt, final_g, ys)


def _moe_plan(cnt, nb):
    G, S = N_EXPERT_GROUPS, EXPERTS_PER_GROUP
    counts = cnt[:, 0].reshape(S, G).T.reshape(G * S)
    padded = (counts + MOE_ROWS - 1) // MOE_ROWS * MOE_ROWS
    pend = jnp.cumsum(padded).astype(I32)
    pstart = pend - padded
    pad_at = jnp.minimum((pstart + counts) // SUBLANES * SUBLANES, (nb - 1) * MOE_ROWS).astype(I32)
    block_start = jnp.arange(nb, dtype=I32) * MOE_ROWS
    block_e = jnp.minimum(jnp.sum((pend[None, :] <= block_start[:, None]).astype(I32), axis=1), G * S - 1)
    n_used = pend[-1:] // MOE_ROWS
    return pstart, pad_at, block_e.astype(I32), n_used.astype(I32)


def _block_diag(w):
    nb, bw, _ = w.shape
    eye = jnp.eye(nb, dtype=w.dtype)
    return jnp.einsum('nij,nm->nimj', w, eye).reshape(nb * bw, nb * bw)


def kernel(x, c, ada_w, ada_b, norm_mix_g, norm_ffn_g, w_in, w_branch_gate, b_branch_gate,
           na_rpb, lru_conv_w, lru_conv_b, lru_w_r, lru_b_r, lru_w_i, lru_b_i, lru_lambda,
           w_proj_na, w_proj_lru, w_proj_fnet, w_out, w_router, router_bias,
           w_exp_gate, w_exp_up, w_exp_down, final_g):
    B, T, D = x.shape
    L = ada_w.shape[0]
    N = B * T
    E = N_EXPERTS
    G, S = N_EXPERT_GROUPS, EXPERTS_PER_GROUP
    A = N * TOP_K
    nb = A // MOE_ROWS + E
    P = nb * MOE_ROWS

    mod = _ada_mod(c, ada_w, ada_b)
    m1_np, cs_np = _fnet_tables(T)
    m1 = jnp.asarray(m1_np).astype(BF16)
    cs = jnp.asarray(cs_np).astype(BF16)
    tri = (lax.broadcasted_iota(I32, (TM_MERGE, TM_MERGE), 0)
           <= lax.broadcasted_iota(I32, (TM_MERGE, TM_MERGE), 1)).astype(BF16)
    wr = w_router.T.reshape(G, S, D).transpose(1, 0, 2).reshape(E, D)
    rb = router_bias.reshape(G, S).T.reshape(E, 1)
    fg = final_g.reshape(1, D)

    x2 = x.reshape(N, D)
    for l in range(L):
        sh_mix, sc_mix, gt_mix, sh_ffn, sc_ffn, gt_ffn = [m.reshape(B, 1, D) for m in jnp.split(mod[l], 6, axis=-1)]
        wcat = jnp.concatenate([w_in[l], w_branch_gate[l]], axis=1).astype(BF16)
        q, k, v, xr, gy, fr, gates = _mix_in(x2, norm_mix_g[l].reshape(1, D), sh_mix, sc_mix, wcat,
                                             b_branch_gate[l].reshape(1, -1), B, T)
        y_na = _na(q, k, v, _na_bias_table(na_rpb[l]), B, T)

        wg = jnp.stack([jnp.concatenate([_block_diag(lru_w_r[l, d]), _block_diag(lru_w_i[l, d])], axis=1)
                        for d in range(2)]).astype(BF16)
        bgl = jnp.stack([jnp.concatenate([lru_b_r[l, d], lru_b_i[l, d]]) for d in range(2)])[:, None, :]
        af, bf, ab, bb = _lru_gates(xr, lru_conv_w[l], lru_conv_b[l].reshape(1, -1), wg, bgl,
                                    lru_lambda[l][:, None, :], B, T)
        y_lru = _lru_scans(af, bf, ab, bb, gy)
        y_f = _fnet(fr, m1, cs, B, T)

        x1, u2, ri, rw, cnt = _merge(gates, y_na, y_lru, y_f, x2,
                                     w_proj_na[l].astype(BF16), w_proj_lru[l].astype(BF16),
                                     w_proj_fnet[l].astype(BF16), w_out[l].astype(BF16),
                                     gt_mix, norm_ffn_g[l].reshape(1, D), sh_ffn, sc_ffn, wr, rb, tri, B, T)
        pstart, pad_at, block_e, n_used = _moe_plan(cnt, nb)
        xs = _dispatch(pstart, pad_at, ri, u2, P)
        ys = _experts(block_e, n_used, xs, w_exp_gate, w_exp_up, w_exp_down, l)
        x2 = _combine(pstart, ri, rw.T, x1, gt_ffn, fg, ys, T, final=(l == L - 1))
    return x2.reshape(B, T, D)
```

```python
import functools

import numpy as np
import jax
import jax.numpy as jnp
from jax import lax
from jax.experimental import pallas as pl
from jax.experimental.pallas import tpu as pltpu

F32 = jnp.float32
BF16 = jnp.bfloat16
I32 = jnp.int32

GRID_W = 64
NA_HEAD_DIM = 64
NA_WIN_ROWS = 8
NA_WIN_COLS = 16
NA_HEADS_PER_TILE = 4
LRU_BLOCKS = 8
LRU_C = 8.0
CONV_W = 4
FNET_GROUP_W = 128
N_EXPERTS = 32
N_EXPERT_GROUPS = 8
EXPERTS_PER_GROUP = 4
TOP_K = 2
EPS = 1e-6
MASK_VALUE = -1e30

V7X_VMEM_BYTES = 64 * 1024 * 1024
LANES = 128
SUBLANES = 8

TM_MIX = 1024
TN_MIX = 1024
MIX_ROW_CHUNK = 512
NA_ROWS_PER_STEP = 8
TM_LRU = 512
T_SCAN = 512
TM_MERGE = 512
TM_MOE = 512
MOE_ROWS = 512


def _cparams(semantics, vmem_mb):
    assert vmem_mb * 1024 * 1024 < V7X_VMEM_BYTES
    return pltpu.CompilerParams(dimension_semantics=semantics, vmem_limit_bytes=vmem_mb * 1024 * 1024)


def _split2(a):
    hi = a.astype(BF16)
    lo = (a - hi.astype(F32)).astype(BF16)
    return hi, lo


def _dot_hi(a, b, dims):
    a_hi, a_lo = _split2(a)
    b_hi, b_lo = _split2(b)
    dn = (dims, ((), ()))
    out = lax.dot_general(a_hi, b_hi, dn, preferred_element_type=F32)
    out = out + lax.dot_general(a_hi, b_lo, dn, preferred_element_type=F32)
    out = out + lax.dot_general(a_lo, b_hi, dn, preferred_element_type=F32)
    return out


def _sigmoid(x):
    return 0.5 * jnp.tanh(0.5 * x) + 0.5


def _rmsnorm_mod(x, g, sh, sc):
    y = x * lax.rsqrt(jnp.mean(x * x, axis=-1, keepdims=True) + EPS)
    return (y * g) * (1.0 + sc) + sh


def _ada_kernel(c_ref, w_ref, b_ref, o_ref):
    c = c_ref[...]
    ca = c * jax.nn.sigmoid(c)
    o_ref[0] = _dot_hi(ca, w_ref[0], ((1,), (0,))) + b_ref[0]


def _ada_mod(c, ada_w, ada_b):
    L, D, D6 = ada_w.shape
    B = c.shape[0]
    bp = -(-B // SUBLANES) * SUBLANES
    cp = jnp.pad(c, ((0, bp - B), (0, 0)))
    tn = D6 // 4
    out = pl.pallas_call(
        _ada_kernel,
        grid=(L, D6 // tn),
        in_specs=[pl.BlockSpec((bp, D), lambda l, j: (0, 0)),
                  pl.BlockSpec((1, D, tn), lambda l, j: (l, 0, j)),
                  pl.BlockSpec((1, 1, tn), lambda l, j: (l, 0, j))],
        out_specs=pl.BlockSpec((1, bp, tn), lambda l, j: (l, 0, j)),
        out_shape=jax.ShapeDtypeStruct((L, bp, D6), F32),
        compiler_params=_cparams(("parallel", "parallel"), 40),
        name="ada_mod",
    )(cp, ada_w, ada_b.reshape(L, 1, D6))
    return out[:, :B]


def _mixin_kernel(x_ref, g_ref, sh_ref, sc_ref, w_ref, bg_ref,
                  q_ref, k_ref, v_ref, xr_ref, gy_ref, fr_ref, gate_ref, u_sc):
    j = pl.program_id(1)

    @pl.when(j == 0)
    def _():
        u_sc[...] = _rmsnorm_mod(x_ref[...], g_ref[...], sh_ref[0], sc_ref[0]).astype(BF16)

    tm = u_sc.shape[0]
    W = q_ref.shape[1]
    lo, hi = slice(0, W), slice(W, 2 * W)

    def per_chunk(store):
        for c in range(tm // MIX_ROW_CHUNK):
            rows = slice(c * MIX_ROW_CHUNK, (c + 1) * MIX_ROW_CHUNK)
            store(rows, jnp.dot(u_sc[rows, :], w_ref[...], preferred_element_type=F32))

    @pl.when(j == 0)
    def _():
        def store(rows, acc):
            q_ref[rows, :] = acc[:, lo].astype(BF16)
            k_ref[rows, :] = acc[:, hi].astype(BF16)
        per_chunk(store)

    @pl.when(j == 1)
    def _():
        def store(rows, acc):
            v_ref[rows, :] = acc[:, lo].astype(BF16)
            xr_ref[rows, :] = acc[:, hi]
        per_chunk(store)

    @pl.when(j == 2)
    def _():
        def store(rows, acc):
            gy_ref[rows, :] = jax.nn.gelu(acc[:, lo]).astype(BF16)
            fr_ref[rows, :] = acc[:, hi].astype(BF16)
        per_chunk(store)

    @pl.when(j >= 3)
    def _():
        def store(rows, acc):
            gate_ref[rows, :] = _sigmoid(acc + bg_ref[...]).astype(BF16)
        per_chunk(store)


def _mix_in(x2, g, sh, sc, wcat, bgate, B, T):
    N, D = x2.shape
    tm, tn = TM_MIX, TN_MIX
    tpb = T // tm
    W = tn // 2
    n_proj = 3
    n_gate = bgate.shape[1] // tn
    tok = lambda i, j: (i, 0)
    scan = lambda i, j: (i % tpb, i // tpb)
    per_b = lambda i, j: (i // tpb, 0, 0)
    out_shape = [jax.ShapeDtypeStruct((N, W), BF16)] * 3 + [
        jax.ShapeDtypeStruct((N, W), F32),
        jax.ShapeDtypeStruct((T, B * W), BF16),
        jax.ShapeDtypeStruct((N, W), BF16),
        jax.ShapeDtypeStruct((N, n_gate * tn), BF16)]
    out_specs = [pl.BlockSpec((tm, W), tok)] * 3 + [
        pl.BlockSpec((tm, W), tok),
        pl.BlockSpec((tm, W), scan),
        pl.BlockSpec((tm, W), tok),
        pl.BlockSpec((tm, tn), lambda i, j: (i, jnp.maximum(j - n_proj, 0)))]
    return pl.pallas_call(
        _mixin_kernel,
        grid=(N // tm, n_proj + n_gate),
        in_specs=[pl.BlockSpec((tm, D), tok),
                  pl.BlockSpec((1, D), lambda i, j: (0, 0)),
                  pl.BlockSpec((1, 1, D), per_b),
                  pl.BlockSpec((1, 1, D), per_b),
                  pl.BlockSpec((D, tn), lambda i, j: (0, j)),
                  pl.BlockSpec((1, tn), lambda i, j: (0, jnp.maximum(j - n_proj, 0)))],
        out_specs=out_specs,
        out_shape=out_shape,
        scratch_shapes=[pltpu.VMEM((tm, D), BF16)],
        compiler_params=_cparams(("parallel", "arbitrary"), 48),
        name="mix_in",
    )(x2, g, sh, sc, wcat, bgate)


def _na_kernel(q_ref, kp_ref, kc_ref, kn_ref, vp_ref, vc_ref, vn_ref, bias_ref, o_ref, kb, vb):
    j = pl.program_id(1)
    nj = pl.num_programs(1)
    R = NA_ROWS_PER_STEP
    half = R // 2
    W = GRID_W
    hw = half * W
    tq = R * W
    win = NA_WIN_ROWS * W
    HT = NA_HEADS_PER_TILE
    lt = HT * NA_HEAD_DIM
    kb[0:hw] = kp_ref[...]
    kb[hw:hw + tq] = kc_ref[...]
    kb[hw + tq:2 * hw + tq] = kn_ref[...]
    vb[0:hw] = vp_ref[...]
    vb[hw:hw + tq] = vc_ref[...]
    vb[hw + tq:2 * hw + tq] = vn_ref[...]
    head_of_lane = lax.broadcasted_iota(I32, (W, lt), 1) // NA_HEAD_DIM
    n_tiles = q_ref.shape[1] // lt
    scale = jnp.asarray(NA_HEAD_DIM ** -0.5, BF16)
    for dr in range(R):
        lo = jnp.where(j == 0, max(dr, half), jnp.where(j == nj - 1, min(dr, half), dr))
        var = lo - dr + (half - 1)
        start = pl.multiple_of(lo * W, W)
        for ht in range(n_tiles):
            cs = slice(ht * lt, (ht + 1) * lt)
            qrow = q_ref[dr * W:(dr + 1) * W, cs] * scale
            qbd = jnp.concatenate(
                [jnp.where(head_of_lane == h, qrow, jnp.zeros_like(qrow)) for h in range(HT)], axis=0)
            kwin = kb[pl.ds(start, win), cs]
            s = lax.dot_general(qbd, kwin, (((1,), (1,)), ((), ())), preferred_element_type=F32)
            s = s + bias_ref[var, ht]
            m = jnp.max(s, axis=-1, keepdims=True)
            p = jnp.exp(s - m)
            l = jnp.sum(p, axis=-1, keepdims=True)
            vwin = vb[pl.ds(start, win), cs]
            o = jnp.dot(p.astype(BF16), vwin, preferred_element_type=F32) / l
            orow = jnp.where(head_of_lane == 0, o[0:W], 0.0)
            for h in range(1, HT):
                orow = orow + jnp.where(head_of_lane == h, o[h * W:(h + 1) * W], 0.0)
            o_ref[dr * W:(dr + 1) * W, cs] = orow.astype(BF16)


def _na_bias_table(rpb):
    H = rpb.shape[0]
    W, KR, KC = GRID_W, NA_WIN_ROWS, NA_WIN_COLS
    w = np.arange(W)
    cstart = np.clip(w - KC // 2, 0, W - KC)
    wk = np.arange(W)
    col_ok = (wk[None, :] >= cstart[:, None]) & (wk[None, :] < cstart[:, None] + KC)
    cb = wk[None, :] - w[:, None] + (KC - 1)
    onehot = ((cb[None] == np.arange(2 * KC - 1)[:, None, None]) & col_ok[None]).astype(np.float32)
    toep = jnp.einsum('hac,cwx->hawx', rpb.astype(F32), jnp.asarray(onehot), precision=lax.Precision.HIGHEST)
    toep = jnp.where(jnp.asarray(col_ok)[None, None], toep, MASK_VALUE)
    tab = jnp.stack([toep[:, v:v + KR] for v in range(KR)])
    tab = tab.transpose(0, 1, 3, 2, 4)
    HT = NA_HEADS_PER_TILE
    return tab.reshape(KR, H // HT, HT * W, KR * W)


def _na(q, k, v, bias, B, T):
    N, C = q.shape
    R = NA_ROWS_PER_STEP
    tq = R * GRID_W
    hw = tq // 2
    nj = T // tq
    nh = T // hw
    qmap = lambda b, j: (b * nj + j, 0)
    pmap = lambda b, j: (b * nh + jnp.maximum(2 * j - 1, 0), 0)
    nmap = lambda b, j: (b * nh + jnp.minimum(2 * j + 2, nh - 1), 0)
    return pl.pallas_call(
        _na_kernel,
        grid=(B, nj),
        in_specs=[pl.BlockSpec((tq, C), qmap),
                  pl.BlockSpec((hw, C), pmap), pl.BlockSpec((tq, C), qmap), pl.BlockSpec((hw, C), nmap),
                  pl.BlockSpec((hw, C), pmap), pl.BlockSpec((tq, C), qmap), pl.BlockSpec((hw, C), nmap),
                  pl.BlockSpec(bias.shape, lambda b, j: (0, 0, 0, 0))],
        out_specs=pl.BlockSpec((tq, C), qmap),
        out_shape=jax.ShapeDtypeStruct((N, C), BF16),
        scratch_shapes=[pltpu.VMEM((2 * tq, C), BF16), pltpu.VMEM((2 * tq, C), BF16)],
        compiler_params=_cparams(("parallel", "parallel"), 48),
        name="na_attention",
    )(q, k, k, k, v, v, v, bias)


def _lru_gate_kernel(x_ref, xp_ref, xn_ref, cw_ref, cb_ref, wg_ref, bg_ref, lam_ref,
                     af_ref, bf_ref, ab_ref, bb_ref, xpad, *, tpb):
    i = pl.program_id(0)
    tm, C = x_ref.shape
    H = SUBLANES
    first = (i % tpb) == 0
    last = (i % tpb) == tpb - 1
    xpad[0:H] = jnp.where(first, 0.0, xp_ref[...])
    xpad[H:H + tm] = x_ref[...]
    xpad[H + tm:2 * H + tm] = jnp.where(last, 0.0, xn_ref[...])
    left = CONV_W // 2
    xc = xpad[H - left:H - left + tm] * cw_ref[0:1]
    for t in range(1, CONV_W):
        xc = xc + xpad[H - left + t:H - left + t + tm] * cw_ref[t:t + 1]
    xc = xc + cb_ref[...]
    xcb = xc.astype(BF16)
    for d, (a_ref, b_ref) in enumerate(((af_ref, bf_ref), (ab_ref, bb_ref))):
        g = jnp.dot(xcb, wg_ref[d], preferred_element_type=F32) + bg_ref[d]
        r = _sigmoid(g[:, :C])
        ig = _sigmoid(g[:, C:])
        log_a = -LRU_C * r * jax.nn.softplus(-lam_ref[d])
        a = jnp.exp(log_a)
        a_ref[...] = a
        b_ref[...] = jnp.sqrt(1.0 - a * a) * (ig * xc)


def _lru_gates(xr, conv_w, conv_b, wg, bg, lam, B, T):
    N, C = xr.shape
    tm = TM_LRU
    tpb = T // tm
    nb8 = N // SUBLANES
    r8 = tm // SUBLANES
    scan = lambda i: (i % tpb, i // tpb)
    const2 = lambda i: (0, 0)
    const3 = lambda i: (0, 0, 0)
    outs = pl.pallas_call(
        functools.partial(_lru_gate_kernel, tpb=tpb),
        grid=(N // tm,),
        in_specs=[pl.BlockSpec((tm, C), lambda i: (i, 0)),
                  pl.BlockSpec((SUBLANES, C), lambda i: (jnp.maximum(i * r8 - 1, 0), 0)),
                  pl.BlockSpec((SUBLANES, C), lambda i: (jnp.minimum((i + 1) * r8, nb8 - 1), 0)),
                  pl.BlockSpec((CONV_W, C), const2),
                  pl.BlockSpec((1, C), const2),
                  pl.BlockSpec((2, C, 2 * C), const3),
                  pl.BlockSpec((2, 1, 2 * C), const3),
                  pl.BlockSpec((2, 1, C), const3)],
        out_specs=[pl.BlockSpec((tm, C), scan)] * 4,
        out_shape=[jax.ShapeDtypeStruct((T, B * C), F32)] * 4,
        scratch_shapes=[pltpu.VMEM((tm + 2 * SUBLANES, C), F32)],
        compiler_params=_cparams(("parallel",), 48),
        name="lru_gates",
    )(xr, xr, xr, conv_w, conv_b, wg, bg, lam)
    return outs


def _scan_bwd_kernel(a_ref, b_ref, h_ref, carry):
    @pl.when(pl.program_id(0) == 0)
    def _():
        carry[...] = jnp.zeros_like(carry)

    tc = a_ref.shape[0]

    def body(s, h):
        t = tc - 1 - s
        h = a_ref[t] * h + b_ref[t]
        h_ref[t] = h
        return h

    carry[...] = lax.fori_loop(0, tc, body, carry[...], unroll=8)


def _scan_fwd_kernel(a_ref, b_ref, hb_ref, gy_ref, y_ref, carry):
    @pl.when(pl.program_id(0) == 0)
    def _():
        carry[...] = jnp.zeros_like(carry)

    tc = a_ref.shape[0]

    def body(t, h):
        h = a_ref[t] * h + b_ref[t]
        y_ref[t] = ((h + hb_ref[t]) * gy_ref[t].astype(F32)).astype(BF16)
        return h

    carry[...] = lax.fori_loop(0, tc, body, carry[...], unroll=8)


def _lru_scans(af, bf, ab, bb, gy):
    T, BC = af.shape
    S = BC // LANES
    tc = T_SCAN
    nc = T // tc
    v3 = lambda z: z.reshape(T, S, LANES)
    blk = (tc, S, LANES)
    fwd = lambda c: (c, 0, 0)
    rev = lambda c: (nc - 1 - c, 0, 0)
    hb = pl.pallas_call(
        _scan_bwd_kernel,
        grid=(nc,),
        in_specs=[pl.BlockSpec(blk, rev), pl.BlockSpec(blk, rev)],
        out_specs=pl.BlockSpec(blk, rev),
        out_shape=jax.ShapeDtypeStruct((T, S, LANES), F32),
        scratch_shapes=[pltpu.VMEM((S, LANES), F32)],
        compiler_params=_cparams(("arbitrary",), 48),
        name="lru_scan_bwd",
    )(v3(ab), v3(bb))
    y = pl.pallas_call(
        _scan_fwd_kernel,
        grid=(nc,),
        in_specs=[pl.BlockSpec(blk, fwd)] * 4,
        out_specs=pl.BlockSpec(blk, fwd),
        out_shape=jax.ShapeDtypeStruct((T, S, LANES), BF16),
        scratch_shapes=[pltpu.VMEM((S, LANES), F32)],
        compiler_params=_cparams(("arbitrary",), 48),
        name="lru_scan_fwd",
    )(v3(af), v3(bf), hb, v3(gy))
    return y.reshape(T, BC)


def _fnet_tables(T):
    T2 = FNET_GROUP_W
    T1 = T // T2
    k1 = np.arange(T1)[:, None]
    t1 = np.arange(T1)[None, :]
    stage1 = np.zeros((T2, 2 * T1, 2 * T1), np.float32)
    for t2 in range(T2):
        ph = 2.0 * np.pi * ((k1 * (T2 * t1 + t2)) % T) / T
        c, s = np.cos(ph), np.sin(ph)
        stage1[t2] = np.block([[c, s], [-s, c]])
    n = np.arange(T2)
    ph2 = 2.0 * np.pi * ((n[:, None] * n[None, :]) % T2) / T2
    c2, s2 = np.cos(ph2).astype(np.float32), np.sin(ph2).astype(np.float32)
    return stage1, np.concatenate([c2, s2], axis=1)


def _fnet_kernel(x_ref, m1_ref, cs_ref, o_ref, zr, zi, yo, *, T1):
    T2 = FNET_GROUP_W
    T = T1 * T2
    cs = cs_ref[...]
    for k1 in range(T1):
        blk = slice(k1 * T2, (k1 + 1) * T2)
        z = jnp.dot(x_ref[blk, :], cs, preferred_element_type=F32)
        zr[blk, :] = z[:, :T2]
        zi[blk, :] = -z[:, T2:]
    for t2 in range(T2):
        rows = pl.ds(t2, T1, stride=T2)
        zin = jnp.concatenate([zr[rows, :], zi[rows, :]], axis=0).astype(BF16)
        a = jnp.dot(m1_ref[t2], zin, preferred_element_type=F32)
        zr[rows, :] = a[:T1]
        zi[rows, :] = a[T1:]
    inv = 1.0 / np.sqrt(float(T) * T2)
    for k1 in range(T1):
        blk = slice(k1 * T2, (k1 + 1) * T2)
        ain = jnp.concatenate([zr[blk, :], zi[blk, :]], axis=0).astype(BF16)
        y = jnp.dot(cs, ain, preferred_element_type=F32)
        yo[pl.ds(k1, T2, stride=T1), :] = y * inv
    o_ref[...] = yo[...].astype(BF16)


def _fnet(fr, m1, cs, B, T):
    N, C = fr.shape
    G = C // FNET_GROUP_W
    T1 = T // FNET_GROUP_W
    blk = pl.BlockSpec((T, FNET_GROUP_W), lambda b, g: (b, g))
    return pl.pallas_call(
        functools.partial(_fnet_kernel, T1=T1),
        grid=(B, G),
        in_specs=[blk,
                  pl.BlockSpec(m1.shape, lambda b, g: (0, 0, 0)),
                  pl.BlockSpec(cs.shape, lambda b, g: (0, 0))],
        out_specs=blk,
        out_shape=jax.ShapeDtypeStruct((N, C), BF16),
        scratch_shapes=[pltpu.VMEM((T, FNET_GROUP_W), F32)] * 3,
        compiler_params=_cparams(("parallel", "parallel"), 48),
        name="fnet",
    )(fr, m1, cs)


def _merge_kernel(gate_ref, yna_ref, ylru_ref, yf_ref, x_ref, wna_ref, wlru_ref, wf_ref, wout_ref,
                  gt_ref, g_ref, sh_ref, sc_ref, wr_ref, rb_ref, tri_ref,
                  x1_ref, u2_ref, ri_ref, rw_ref, cnt_ref, cnt_sc):
    i = pl.program_id(0)

    @pl.when(i == 0)
    def _():
        cnt_sc[...] = jnp.zeros_like(cnt_sc)

    tm, D = x_ref.shape
    gates = gate_ref[...]
    merged = gates[:, 0:D].astype(F32) * jnp.dot(yna_ref[...], wna_ref[...], preferred_element_type=F32)
    merged = merged + gates[:, D:2 * D].astype(F32) * jnp.dot(ylru_ref[...], wlru_ref[...], preferred_element_type=F32)
    merged = merged + gates[:, 2 * D:3 * D].astype(F32) * jnp.dot(yf_ref[...], wf_ref[...], preferred_element_type=F32)
    out = jnp.dot(merged.astype(BF16), wout_ref[...], preferred_element_type=F32)
    x1 = x_ref[...] + gt_ref[0] * out
    x1_ref[...] = x1
    u2 = _rmsnorm_mod(x1, g_ref[...], sh_ref[0], sc_ref[0])
    u2_ref[...] = u2

    G, S = N_EXPERT_GROUPS, EXPERTS_PER_GROUP
    logits = _dot_hi(wr_ref[...], u2, ((1,), (1,)))
    score = jax.nn.sigmoid(logits)
    sel = score + rb_ref[...]
    sel_s = [sel[s * G:(s + 1) * G] for s in range(S)]
    sc_s = [score[s * G:(s + 1) * G] for s in range(S)]
    gscore = None
    for a in range(S):
        for b in range(a + 1, S):
            pair = sel_s[a] + sel_s[b]
            gscore = pair if gscore is None else jnp.maximum(gscore, pair)
    giota = lax.broadcasted_iota(I32, (G, tm), 0)
    gmax = jnp.max(gscore, axis=0, keepdims=True)
    gidx = jnp.min(jnp.where(gscore == gmax, giota, G), axis=0, keepdims=True)
    gm = giota == gidx
    cand = [jnp.sum(jnp.where(gm, sel_s[s], 0.0), axis=0, keepdims=True) for s in range(S)]
    raw = [jnp.sum(jnp.where(gm, sc_s[s], 0.0), axis=0, keepdims=True) for s in range(S)]

    def first_argmax(vals):
        m = vals[0]
        for s in range(1, S):
            m = jnp.maximum(m, vals[s])
        idx = jnp.full(m.shape, S - 1, I32)
        for s in range(S - 2, -1, -1):
            idx = jnp.where(vals[s] == m, s, idx)
        return idx

    l0 = first_argmax(cand)
    l1 = first_argmax([jnp.where(l0 == s, -jnp.inf, cand[s]) for s in range(S)])

    def pick(vals, idx):
        out = vals[S - 1]
        for s in range(S - 2, -1, -1):
            out = jnp.where(idx == s, vals[s], out)
        return out

    w0 = pick(raw, l0)
    w1 = pick(raw, l1)
    wsum = w0 + w1
    w0 = w0 / wsum
    w1 = w1 / wsum
    e0 = gidx * S + l0
    e1 = gidx * S + l1

    E = G * S
    eiota = lax.broadcasted_iota(I32, (E, tm), 0)
    oh0 = eiota == (l0 * G + gidx)
    oh1 = eiota == (l1 * G + gidx)
    oh = jnp.where(oh0 | oh1, 1.0, 0.0)
    incl = jnp.dot(oh.astype(BF16), tri_ref[...], preferred_element_type=F32)
    before = incl - oh + cnt_sc[:, 0:1]
    p0 = jnp.sum(jnp.where(oh0, before, 0.0), axis=0, keepdims=True).astype(I32)
    p1 = jnp.sum(jnp.where(oh1, before, 0.0), axis=0, keepdims=True).astype(I32)
    cnt_sc[...] = cnt_sc[...] + jnp.sum(oh, axis=1, keepdims=True)
    cnt_ref[...] = cnt_sc[...].astype(I32)

    row = lax.broadcasted_iota(I32, (SUBLANES, tm), 0)
    ri_ref[...] = jnp.where(row == 0, e0, jnp.where(row == 1, e1, jnp.where(row == 2, p0, jnp.where(row == 3, p1, 0))))
    rw_ref[...] = jnp.where(row == 0, w0, jnp.where(row == 1, w1, 0.0))


def _merge(gates, y_na, y_lru, y_f, x2, wna, wlru, wf, wout, gt, g, sh, sc, wr, rb, tri, B, T):
    N, D = x2.shape
    C = y_na.shape[1]
    tm = TM_MERGE
    tpb = T // tm
    E = wr.shape[0]
    tok = lambda i: (i, 0)
    scan = lambda i: (i % tpb, i // tpb)
    per_b = lambda i: (i // tpb, 0, 0)
    const2 = lambda i: (0, 0)
    return pl.pallas_call(
        _merge_kernel,
        grid=(N // tm,),
        in_specs=[pl.BlockSpec((tm, 3 * D), tok),
                  pl.BlockSpec((tm, C), tok),
                  pl.BlockSpec((tm, C), scan),
                  pl.BlockSpec((tm, C), tok),
                  pl.BlockSpec((tm, D), tok),
                  pl.BlockSpec((C, D), const2), pl.BlockSpec((C, D), const2), pl.BlockSpec((C, D), const2),
                  pl.BlockSpec((D, D), const2),
                  pl.BlockSpec((1, 1, D), per_b),
                  pl.BlockSpec((1, D), const2),
                  pl.BlockSpec((1, 1, D), per_b),
                  pl.BlockSpec((1, 1, D), per_b),
                  pl.BlockSpec((E, D), const2),
                  pl.BlockSpec((E, 1), const2),
                  pl.BlockSpec((tm, tm), const2)],
        out_specs=[pl.BlockSpec((tm, D), tok),
                   pl.BlockSpec((tm, D), tok),
                   pl.BlockSpec((SUBLANES, tm), lambda i: (0, i)),
                   pl.BlockSpec((SUBLANES, tm), lambda i: (0, i)),
                   pl.BlockSpec((E, LANES), const2)],
        out_shape=[jax.ShapeDtypeStruct((N, D), F32),
                   jax.ShapeDtypeStruct((N, D), F32),
                   jax.ShapeDtypeStruct((SUBLANES, N), I32),
                   jax.ShapeDtypeStruct((SUBLANES, N), F32),
                   jax.ShapeDtypeStruct((E, LANES), I32)],
        scratch_shapes=[pltpu.VMEM((E, LANES), F32)],
        compiler_params=_cparams(("arbitrary",), 48),
        name="merge_route",
    )(gates, y_na, y_lru, y_f, x2, wna, wlru, wf, wout, gt, g, sh, sc, wr, rb, tri)


def _row_copy(src, s, dst, d, sem):
    return pltpu.make_async_copy(src.at[pl.ds(s, 1)], dst.at[pl.ds(d, 1)], sem)


def _dispatch_kernel(ps_ref, pad_ref, nu_ref, ri_ref, u_ref, xs_out, zbuf, sem):
    tm = u_ref.shape[0]
    nb = xs_out.shape[0] // MOE_ROWS

    @pl.when(pl.program_id(0) == 0)
    def _():
        zbuf[...] = jnp.zeros_like(zbuf)

        def zero_copy(blk):
            return pltpu.make_async_copy(zbuf, xs_out.at[pl.ds(pl.multiple_of(blk * MOE_ROWS, MOE_ROWS), MOE_ROWS)], sem)

        def guarded(e, fn):
            pl.when(pad_ref[e] >= 0)(lambda: fn(zero_copy(pad_ref[e])))
            blk = nb - 1 - e
            pl.when(blk >= nu_ref[0])(lambda: fn(zero_copy(blk)))

        for e in range(N_EXPERTS):
            guarded(e, lambda cp: cp.start())
        for e in range(N_EXPERTS):
            guarded(e, lambda cp: cp.wait())

    def body(t, c):
        for k in range(TOP_K):
            d = ps_ref[ri_ref[k, t]] + ri_ref[TOP_K + k, t]
            _row_copy(u_ref, t, xs_out, d, sem).start()
        return c

    lax.fori_loop(0, tm, body, 0, unroll=8)

    for k in range(TOP_K):
        pltpu.make_async_copy(u_ref, xs_out.at[pl.ds(0, tm)], sem).wait()


def _dispatch(pstart, pad_at, n_used, ri, u2, P):
    N, D = u2.shape
    tm = TM_MOE
    grid_spec = pltpu.PrefetchScalarGridSpec(
        num_scalar_prefetch=3,
        grid=(N // tm,),
        in_specs=[pl.BlockSpec((SUBLANES, tm), lambda i, ps, pa, nu: (0, i), memory_space=pltpu.SMEM),
                  pl.BlockSpec((tm, D), lambda i, ps, pa, nu: (i, 0))],
        out_specs=pl.BlockSpec(memory_space=pl.ANY),
        scratch_shapes=[pltpu.VMEM((MOE_ROWS, D), F32), pltpu.SemaphoreType.DMA(())],
    )
    return pl.pallas_call(
        _dispatch_kernel,
        grid_spec=grid_spec,
        out_shape=jax.ShapeDtypeStruct((P, D), F32),
        compiler_params=_cparams(("arbitrary",), 32),
        name="moe_dispatch",
    )(pstart, pad_at, n_used, ri, u2)


def _expert_kernel(be_ref, nu_ref, xs_ref, wg_ref, wu_ref, wd_ref, ys_ref, wg_sc, wu_sc, wd_sc):
    i = pl.program_id(0)
    prev = be_ref[jnp.maximum(i - 1, 0)]
    fresh = (i == 0) | (be_ref[i] != prev)

    @pl.when(fresh)
    def _():
        wg_sc[...] = wg_ref[0, 0].astype(BF16)
        wu_sc[...] = wu_ref[0, 0].astype(BF16)
        wd_sc[...] = wd_ref[0, 0].astype(BF16)

    @pl.when(i < nu_ref[0])
    def _():
        xb = xs_ref[...].astype(BF16)
        hg = jnp.dot(xb, wg_sc[...], preferred_element_type=F32)
        hu = jnp.dot(xb, wu_sc[...], preferred_element_type=F32)
        h = (hg * _sigmoid(hg)) * hu
        ys_ref[...] = jnp.dot(h.astype(BF16), wd_sc[...], preferred_element_type=F32)

    @pl.when(i >= nu_ref[0])
    def _():
        ys_ref[...] = jnp.zeros_like(ys_ref)


def _experts(block_e, n_used, xs, w_gate, w_up, w_down, layer):
    P, D = xs.shape
    _, E, _, DE = w_gate.shape
    nb = P // MOE_ROWS
    wmap = lambda i, be, nu: (layer, be[i], 0, 0)
    grid_spec = pltpu.PrefetchScalarGridSpec(
        num_scalar_prefetch=2,
        grid=(nb,),
        in_specs=[pl.BlockSpec((MOE_ROWS, D), lambda i, be, nu: (i, 0)),
                  pl.BlockSpec((1, 1, D, DE), wmap),
                  pl.BlockSpec((1, 1, D, DE), wmap),
                  pl.BlockSpec((1, 1, DE, D), wmap)],
        out_specs=pl.BlockSpec((MOE_ROWS, D), lambda i, be, nu: (i, 0)),
        scratch_shapes=[pltpu.VMEM((D, DE), BF16), pltpu.VMEM((D, DE), BF16), pltpu.VMEM((DE, D), BF16)],
    )
    return pl.pallas_call(
        _expert_kernel,
        grid_spec=grid_spec,
        out_shape=jax.ShapeDtypeStruct((P, D), F32),
        compiler_params=_cparams(("arbitrary",), 48),
        name="moe_experts",
    )(block_e, n_used, xs, w_gate, w_up, w_down)


def _combine_kernel(ps_ref, ri_ref, rw_ref, x_ref, gt_ref, fg_ref, ys_ref, o_ref, g0, g1, sem, *, final):
    tm = x_ref.shape[0]
    bufs = (g0, g1)

    def body(t, c):
        for k in range(TOP_K):
            d = ps_ref[ri_ref[k, t]] + ri_ref[TOP_K + k, t]
            _row_copy(ys_ref, d, bufs[k], t, sem).start()
        return c

    lax.fori_loop(0, tm, body, 0, unroll=8)

    for k in range(TOP_K):
        pltpu.make_async_copy(ys_ref.at[pl.ds(0, tm)], bufs[k], sem).wait()
    w = rw_ref[...]
    y = g0[...] * w[:, 0:1] + g1[...] * w[:, 1:2]
    x2 = x_ref[...] + gt_ref[0] * y
    if final:
        x2 = (x2 * lax.rsqrt(jnp.mean(x2 * x2, axis=-1, keepdims=True) + EPS)) * fg_ref[...]
    o_ref[...] = x2


def _combine(pstart, ri, rw_t, x1, gt, final_g, ys, T, final):
    N, D = x1.shape
    tm = TM_MOE
    tpb = T // tm
    grid_spec = pltpu.PrefetchScalarGridSpec(
        num_scalar_prefetch=1,
        grid=(N // tm,),
        in_specs=[pl.BlockSpec((SUBLANES, tm), lambda i, ps: (0, i), memory_space=pltpu.SMEM),
                  pl.BlockSpec((tm, SUBLANES), lambda i, ps: (i, 0)),
                  pl.BlockSpec((tm, D), lambda i, ps: (i, 0)),
                  pl.BlockSpec((1, 1, D), lambda i, ps: (i // tpb, 0, 0)),
                  pl.BlockSpec((1, D), lambda i, ps: (0, 0)),
                  pl.BlockSpec(memory_space=pl.ANY)],
        out_specs=pl.BlockSpec((tm, D), lambda i, ps: (i, 0)),
        scratch_shapes=[pltpu.VMEM((tm, D), F32), pltpu.VMEM((tm, D), F32), pltpu.SemaphoreType.DMA(())],
    )
    return pl.pallas_call(
        functools.partial(_combine_kernel, final=final),
        grid_spec=grid_spec,
        out_shape=jax.ShapeDtypeStruct((N, D), F32),
        compiler_params=_cparams(("arbitrary",), 32),
        name="moe_combine",
    )(pstart, ri, rw_t, x1, gt, final_g, ys)


def _moe_plan(cnt, nb):
    G, S = N_EXPERT_GROUPS, EXPERTS_PER_GROUP
    counts = cnt[:, 0].reshape(S, G).T.reshape(G * S)
    padded = (counts + MOE_ROWS - 1) // MOE_ROWS * MOE_ROWS
    pend = jnp.cumsum(padded).astype(I32)
    pstart = pend - padded
    pad_blk = jnp.where(padded > 0, pend // MOE_ROWS - 1, -1).astype(I32)
    block_start = jnp.arange(nb, dtype=I32) * MOE_ROWS
    block_e = jnp.minimum(jnp.sum((pend[None, :] <= block_start[:, None]).astype(I32), axis=1), G * S - 1)
    n_used = pend[-1:] // MOE_ROWS
    return pstart, pad_blk, block_e.astype(I32), n_used.astype(I32)


def _block_diag(w):
    nb, bw, _ = w.shape
    eye = jnp.eye(nb, dtype=w.dtype)
    return jnp.einsum('nij,nm->nimj', w, eye).reshape(nb * bw, nb * bw)


def kernel(x, c, ada_w, ada_b, norm_mix_g, norm_ffn_g, w_in, w_branch_gate, b_branch_gate,
           na_rpb, lru_conv_w, lru_conv_b, lru_w_r, lru_b_r, lru_w_i, lru_b_i, lru_lambda,
           w_proj_na, w_proj_lru, w_proj_fnet, w_out, w_router, router_bias,
           w_exp_gate, w_exp_up, w_exp_down, final_g):
    B, T, D = x.shape
    L = ada_w.shape[0]
    N = B * T
    E = N_EXPERTS
    G, S = N_EXPERT_GROUPS, EXPERTS_PER_GROUP
    A = N * TOP_K
    nb = A // MOE_ROWS + E
    P = nb * MOE_ROWS

    mod = _ada_mod(c, ada_w, ada_b)
    m1_np, cs_np = _fnet_tables(T)
    m1 = jnp.asarray(m1_np).astype(BF16)
    cs = jnp.asarray(cs_np).astype(BF16)
    tri = (lax.broadcasted_iota(I32, (TM_MERGE, TM_MERGE), 0)
           <= lax.broadcasted_iota(I32, (TM_MERGE, TM_MERGE), 1)).astype(BF16)
    wr = w_router.T.reshape(G, S, D).transpose(1, 0, 2).reshape(E, D)
    rb = router_bias.reshape(G, S).T.reshape(E, 1)
    fg = final_g.reshape(1, D)

    x2 = x.reshape(N, D)
    for l in range(L):
        sh_mix, sc_mix, gt_mix, sh_ffn, sc_ffn, gt_ffn = [m.reshape(B, 1, D) for m in jnp.split(mod[l], 6, axis=-1)]
        wcat = jnp.concatenate([w_in[l], w_branch_gate[l]], axis=1).astype(BF16)
        q, k, v, xr, gy, fr, gates = _mix_in(x2, norm_mix_g[l].reshape(1, D), sh_mix, sc_mix, wcat,
                                             b_branch_gate[l].reshape(1, -1), B, T)
        y_na = _na(q, k, v, _na_bias_table(na_rpb[l]), B, T)

        wg = jnp.stack([jnp.concatenate([_block_diag(lru_w_r[l, d]), _block_diag(lru_w_i[l, d])], axis=1)
                        for d in range(2)]).astype(BF16)
        bgl = jnp.stack([jnp.concatenate([lru_b_r[l, d], lru_b_i[l, d]]) for d in range(2)])[:, None, :]
        af, bf, ab, bb = _lru_gates(xr, lru_conv_w[l], lru_conv_b[l].reshape(1, -1), wg, bgl,
                                    lru_lambda[l][:, None, :], B, T)
        y_lru = _lru_scans(af, bf, ab, bb, gy)
        y_f = _fnet(fr, m1, cs, B, T)

        x1, u2, ri, rw, cnt = _merge(gates, y_na, y_lru, y_f, x2,
                                     w_proj_na[l].astype(BF16), w_proj_lru[l].astype(BF16),
                                     w_proj_fnet[l].astype(BF16), w_out[l].astype(BF16),
                                     gt_mix, norm_ffn_g[l].reshape(1, D), sh_ffn, sc_ffn, wr, rb, tri, B, T)
        pstart, pad_at, block_e, n_used = _moe_plan(cnt, nb)
        xs = _dispatch(pstart, pad_at, n_used, ri, u2, P)
        ys = _experts(block_e, n_used, xs, w_exp_gate, w_exp_up, w_exp_down, l)
        x2 = _combine(pstart, ri, rw.T, x1, gt_ffn, fg, ys, T, final=(l == L - 1))
    return x2.reshape(B, T, D)
```

```python
import functools

import numpy as np
import jax
import jax.numpy as jnp
from jax import lax
from jax.experimental import pallas as pl
from jax.experimental.pallas import tpu as pltpu

F32 = jnp.float32
BF16 = jnp.bfloat16
I32 = jnp.int32

GRID_W = 64
NA_HEAD_DIM = 64
NA_WIN_ROWS = 8
NA_WIN_COLS = 16
NA_HEADS_PER_TILE = 4
LRU_BLOCKS = 8
LRU_C = 8.0
CONV_W = 4
FNET_GROUP_W = 128
N_EXPERTS = 32
N_EXPERT_GROUPS = 8
EXPERTS_PER_GROUP = 4
TOP_K = 2
EPS = 1e-6
MASK_VALUE = -1e30

V7X_VMEM_BYTES = 64 * 1024 * 1024
LANES = 128
SUBLANES = 8

TM_MIX = 1024
TN_MIX = 1024
MIX_ROW_CHUNK = 512
NA_ROWS_PER_STEP = 8
TM_LRU = 512
T_SCAN = 512
TM_MERGE = 512
TM_MOE = 512
MOE_ROWS = 512
RUN_CHUNK = 64
MOE_STAGE_ROWS = -(-(TM_MOE * TOP_K + N_EXPERTS * (SUBLANES - 1)) // 256) * 256


def _cparams(semantics, vmem_mb):
    assert vmem_mb * 1024 * 1024 < V7X_VMEM_BYTES
    return pltpu.CompilerParams(dimension_semantics=semantics, vmem_limit_bytes=vmem_mb * 1024 * 1024)


def _split2(a):
    hi = a.astype(BF16)
    lo = (a - hi.astype(F32)).astype(BF16)
    return hi, lo


def _dot_hi(a, b, dims):
    a_hi, a_lo = _split2(a)
    b_hi, b_lo = _split2(b)
    dn = (dims, ((), ()))
    out = lax.dot_general(a_hi, b_hi, dn, preferred_element_type=F32)
    out = out + lax.dot_general(a_hi, b_lo, dn, preferred_element_type=F32)
    out = out + lax.dot_general(a_lo, b_hi, dn, preferred_element_type=F32)
    return out


def _sigmoid(x):
    return 0.5 * jnp.tanh(0.5 * x) + 0.5


def _rmsnorm_mod(x, g, sh, sc):
    y = x * lax.rsqrt(jnp.mean(x * x, axis=-1, keepdims=True) + EPS)
    return (y * g) * (1.0 + sc) + sh


def _ada_kernel(c_ref, w_ref, b_ref, o_ref):
    c = c_ref[...]
    ca = c * jax.nn.sigmoid(c)
    o_ref[0] = _dot_hi(ca, w_ref[0], ((1,), (0,))) + b_ref[0]


def _ada_mod(c, ada_w, ada_b):
    L, D, D6 = ada_w.shape
    B = c.shape[0]
    bp = -(-B // SUBLANES) * SUBLANES
    cp = jnp.pad(c, ((0, bp - B), (0, 0)))
    tn = D6 // 4
    out = pl.pallas_call(
        _ada_kernel,
        grid=(L, D6 // tn),
        in_specs=[pl.BlockSpec((bp, D), lambda l, j: (0, 0)),
                  pl.BlockSpec((1, D, tn), lambda l, j: (l, 0, j)),
                  pl.BlockSpec((1, 1, tn), lambda l, j: (l, 0, j))],
        out_specs=pl.BlockSpec((1, bp, tn), lambda l, j: (l, 0, j)),
        out_shape=jax.ShapeDtypeStruct((L, bp, D6), F32),
        compiler_params=_cparams(("parallel", "parallel"), 40),
        name="ada_mod",
    )(cp, ada_w, ada_b.reshape(L, 1, D6))
    return out[:, :B]


def _mixin_kernel(x_ref, g_ref, sh_ref, sc_ref, w_ref, bg_ref,
                  q_ref, k_ref, v_ref, xr_ref, gy_ref, fr_ref, gate_ref, u_sc):
    j = pl.program_id(1)

    @pl.when(j == 0)
    def _():
        u_sc[...] = _rmsnorm_mod(x_ref[...], g_ref[...], sh_ref[0], sc_ref[0]).astype(BF16)

    tm = u_sc.shape[0]
    W = q_ref.shape[1]
    lo, hi = slice(0, W), slice(W, 2 * W)

    def per_chunk(store):
        for c in range(tm // MIX_ROW_CHUNK):
            rows = slice(c * MIX_ROW_CHUNK, (c + 1) * MIX_ROW_CHUNK)
            store(rows, jnp.dot(u_sc[rows, :], w_ref[...], preferred_element_type=F32))

    @pl.when(j == 0)
    def _():
        def store(rows, acc):
            q_ref[rows, :] = acc[:, lo].astype(BF16)
            k_ref[rows, :] = acc[:, hi].astype(BF16)
        per_chunk(store)

    @pl.when(j == 1)
    def _():
        def store(rows, acc):
            v_ref[rows, :] = acc[:, lo].astype(BF16)
            xr_ref[rows, :] = acc[:, hi]
        per_chunk(store)

    @pl.when(j == 2)
    def _():
        def store(rows, acc):
            gy_ref[rows, :] = jax.nn.gelu(acc[:, lo]).astype(BF16)
            fr_ref[rows, :] = acc[:, hi].astype(BF16)
        per_chunk(store)

    @pl.when(j >= 3)
    def _():
        def store(rows, acc):
            gate_ref[rows, :] = _sigmoid(acc + bg_ref[...]).astype(BF16)
        per_chunk(store)


def _mix_in(x2, g, sh, sc, wcat, bgate, B, T):
    N, D = x2.shape
    tm, tn = TM_MIX, TN_MIX
    tpb = T // tm
    W = tn // 2
    n_proj = 3
    n_gate = bgate.shape[1] // tn
    tok = lambda i, j: (i, 0)
    scan = lambda i, j: (i % tpb, i // tpb)
    per_b = lambda i, j: (i // tpb, 0, 0)
    out_shape = [jax.ShapeDtypeStruct((N, W), BF16)] * 3 + [
        jax.ShapeDtypeStruct((N, W), F32),
        jax.ShapeDtypeStruct((T, B * W), BF16),
        jax.ShapeDtypeStruct((N, W), BF16),
        jax.ShapeDtypeStruct((N, n_gate * tn), BF16)]
    out_specs = [pl.BlockSpec((tm, W), tok)] * 3 + [
        pl.BlockSpec((tm, W), tok),
        pl.BlockSpec((tm, W), scan),
        pl.BlockSpec((tm, W), tok),
        pl.BlockSpec((tm, tn), lambda i, j: (i, jnp.maximum(j - n_proj, 0)))]
    return pl.pallas_call(
        _mixin_kernel,
        grid=(N // tm, n_proj + n_gate),
        in_specs=[pl.BlockSpec((tm, D), tok),
                  pl.BlockSpec((1, D), lambda i, j: (0, 0)),
                  pl.BlockSpec((1, 1, D), per_b),
                  pl.BlockSpec((1, 1, D), per_b),
                  pl.BlockSpec((D, tn), lambda i, j: (0, j)),
                  pl.BlockSpec((1, tn), lambda i, j: (0, jnp.maximum(j - n_proj, 0)))],
        out_specs=out_specs,
        out_shape=out_shape,
        scratch_shapes=[pltpu.VMEM((tm, D), BF16)],
        compiler_params=_cparams(("parallel", "arbitrary"), 48),
        name="mix_in",
    )(x2, g, sh, sc, wcat, bgate)


def _na_kernel(q_ref, kp_ref, kc_ref, kn_ref, vp_ref, vc_ref, vn_ref, bias_ref, o_ref, kb, vb):
    j = pl.program_id(1)
    nj = pl.num_programs(1)
    R = NA_ROWS_PER_STEP
    half = R // 2
    W = GRID_W
    hw = half * W
    tq = R * W
    win = NA_WIN_ROWS * W
    HT = NA_HEADS_PER_TILE
    lt = HT * NA_HEAD_DIM
    kb[0:hw] = kp_ref[...]
    kb[hw:hw + tq] = kc_ref[...]
    kb[hw + tq:2 * hw + tq] = kn_ref[...]
    vb[0:hw] = vp_ref[...]
    vb[hw:hw + tq] = vc_ref[...]
    vb[hw + tq:2 * hw + tq] = vn_ref[...]
    head_of_lane = lax.broadcasted_iota(I32, (W, lt), 1) // NA_HEAD_DIM
    n_tiles = q_ref.shape[1] // lt
    scale = jnp.asarray(NA_HEAD_DIM ** -0.5, BF16)
    for dr in range(R):
        lo = jnp.where(j == 0, max(dr, half), jnp.where(j == nj - 1, min(dr, half), dr))
        var = lo - dr + (half - 1)
        start = pl.multiple_of(lo * W, W)
        for ht in range(n_tiles):
            cs = slice(ht * lt, (ht + 1) * lt)
            qrow = q_ref[dr * W:(dr + 1) * W, cs] * scale
            qbd = jnp.concatenate(
                [jnp.where(head_of_lane == h, qrow, jnp.zeros_like(qrow)) for h in range(HT)], axis=0)
            kwin = kb[pl.ds(start, win), cs]
            s = lax.dot_general(qbd, kwin, (((1,), (1,)), ((), ())), preferred_element_type=F32)
            s = s + bias_ref[var, ht]
            m = jnp.max(s, axis=-1, keepdims=True)
            p = jnp.exp(s - m)
            l = jnp.sum(p, axis=-1, keepdims=True)
            vwin = vb[pl.ds(start, win), cs]
            o = jnp.dot(p.astype(BF16), vwin, preferred_element_type=F32) / l
            orow = jnp.where(head_of_lane == 0, o[0:W], 0.0)
            for h in range(1, HT):
                orow = orow + jnp.where(head_of_lane == h, o[h * W:(h + 1) * W], 0.0)
            o_ref[dr * W:(dr + 1) * W, cs] = orow.astype(BF16)


def _na_bias_table(rpb):
    H = rpb.shape[0]
    W, KR, KC = GRID_W, NA_WIN_ROWS, NA_WIN_COLS
    w = np.arange(W)
    cstart = np.clip(w - KC // 2, 0, W - KC)
    wk = np.arange(W)
    col_ok = (wk[None, :] >= cstart[:, None]) & (wk[None, :] < cstart[:, None] + KC)
    cb = wk[None, :] - w[:, None] + (KC - 1)
    onehot = ((cb[None] == np.arange(2 * KC - 1)[:, None, None]) & col_ok[None]).astype(np.float32)
    toep = jnp.einsum('hac,cwx->hawx', rpb.astype(F32), jnp.asarray(onehot), precision=lax.Precision.HIGHEST)
    toep = jnp.where(jnp.asarray(col_ok)[None, None], toep, MASK_VALUE)
    tab = jnp.stack([toep[:, v:v + KR] for v in range(KR)])
    tab = tab.transpose(0, 1, 3, 2, 4)
    HT = NA_HEADS_PER_TILE
    return tab.reshape(KR, H // HT, HT * W, KR * W)


def _na(q, k, v, bias, B, T):
    N, C = q.shape
    R = NA_ROWS_PER_STEP
    tq = R * GRID_W
    hw = tq // 2
    nj = T // tq
    nh = T // hw
    qmap = lambda b, j: (b * nj + j, 0)
    pmap = lambda b, j: (b * nh + jnp.maximum(2 * j - 1, 0), 0)
    nmap = lambda b, j: (b * nh + jnp.minimum(2 * j + 2, nh - 1), 0)
    return pl.pallas_call(
        _na_kernel,
        grid=(B, nj),
        in_specs=[pl.BlockSpec((tq, C), qmap),
                  pl.BlockSpec((hw, C), pmap), pl.BlockSpec((tq, C), qmap), pl.BlockSpec((hw, C), nmap),
                  pl.BlockSpec((hw, C), pmap), pl.BlockSpec((tq, C), qmap), pl.BlockSpec((hw, C), nmap),
                  pl.BlockSpec(bias.shape, lambda b, j: (0, 0, 0, 0))],
        out_specs=pl.BlockSpec((tq, C), qmap),
        out_shape=jax.ShapeDtypeStruct((N, C), BF16),
        scratch_shapes=[pltpu.VMEM((2 * tq, C), BF16), pltpu.VMEM((2 * tq, C), BF16)],
        compiler_params=_cparams(("parallel", "parallel"), 48),
        name="na_attention",
    )(q, k, k, k, v, v, v, bias)


def _lru_gate_kernel(x_ref, xp_ref, xn_ref, cw_ref, cb_ref, wg_ref, bg_ref, lam_ref,
                     af_ref, bf_ref, ab_ref, bb_ref, xpad, *, tpb):
    i = pl.program_id(0)
    tm, C = x_ref.shape
    H = SUBLANES
    first = (i % tpb) == 0
    last = (i % tpb) == tpb - 1
    xpad[0:H] = jnp.where(first, 0.0, xp_ref[...])
    xpad[H:H + tm] = x_ref[...]
    xpad[H + tm:2 * H + tm] = jnp.where(last, 0.0, xn_ref[...])
    left = CONV_W // 2
    xc = xpad[H - left:H - left + tm] * cw_ref[0:1]
    for t in range(1, CONV_W):
        xc = xc + xpad[H - left + t:H - left + t + tm] * cw_ref[t:t + 1]
    xc = xc + cb_ref[...]
    xcb = xc.astype(BF16)
    for d, (a_ref, b_ref) in enumerate(((af_ref, bf_ref), (ab_ref, bb_ref))):
        g = jnp.dot(xcb, wg_ref[d], preferred_element_type=F32) + bg_ref[d]
        r = _sigmoid(g[:, :C])
        ig = _sigmoid(g[:, C:])
        log_a = -LRU_C * r * jax.nn.softplus(-lam_ref[d])
        a = jnp.exp(log_a)
        a_ref[...] = a
        b_ref[...] = jnp.sqrt(1.0 - a * a) * (ig * xc)


def _lru_gates(xr, conv_w, conv_b, wg, bg, lam, B, T):
    N, C = xr.shape
    tm = TM_LRU
    tpb = T // tm
    nb8 = N // SUBLANES
    r8 = tm // SUBLANES
    scan = lambda i: (i % tpb, i // tpb)
    const2 = lambda i: (0, 0)
    const3 = lambda i: (0, 0, 0)
    outs = pl.pallas_call(
        functools.partial(_lru_gate_kernel, tpb=tpb),
        grid=(N // tm,),
        in_specs=[pl.BlockSpec((tm, C), lambda i: (i, 0)),
                  pl.BlockSpec((SUBLANES, C), lambda i: (jnp.maximum(i * r8 - 1, 0), 0)),
                  pl.BlockSpec((SUBLANES, C), lambda i: (jnp.minimum((i + 1) * r8, nb8 - 1), 0)),
                  pl.BlockSpec((CONV_W, C), const2),
                  pl.BlockSpec((1, C), const2),
                  pl.BlockSpec((2, C, 2 * C), const3),
                  pl.BlockSpec((2, 1, 2 * C), const3),
                  pl.BlockSpec((2, 1, C), const3)],
        out_specs=[pl.BlockSpec((tm, C), scan)] * 4,
        out_shape=[jax.ShapeDtypeStruct((T, B * C), F32)] * 4,
        scratch_shapes=[pltpu.VMEM((tm + 2 * SUBLANES, C), F32)],
        compiler_params=_cparams(("parallel",), 48),
        name="lru_gates",
    )(xr, xr, xr, conv_w, conv_b, wg, bg, lam)
    return outs


def _scan_bwd_kernel(a_ref, b_ref, h_ref, carry):
    @pl.when(pl.program_id(0) == 0)
    def _():
        carry[...] = jnp.zeros_like(carry)

    tc = a_ref.shape[0]

    def body(s, h):
        t = tc - 1 - s
        h = a_ref[t] * h + b_ref[t]
        h_ref[t] = h
        return h

    carry[...] = lax.fori_loop(0, tc, body, carry[...], unroll=8)


def _scan_fwd_kernel(a_ref, b_ref, hb_ref, gy_ref, y_ref, carry):
    @pl.when(pl.program_id(0) == 0)
    def _():
        carry[...] = jnp.zeros_like(carry)

    tc = a_ref.shape[0]

    def body(t, h):
        h = a_ref[t] * h + b_ref[t]
        y_ref[t] = ((h + hb_ref[t]) * gy_ref[t].astype(F32)).astype(BF16)
        return h

    carry[...] = lax.fori_loop(0, tc, body, carry[...], unroll=8)


def _lru_scans(af, bf, ab, bb, gy):
    T, BC = af.shape
    S = BC // LANES
    tc = T_SCAN
    nc = T // tc
    v3 = lambda z: z.reshape(T, S, LANES)
    blk = (tc, S, LANES)
    fwd = lambda c: (c, 0, 0)
    rev = lambda c: (nc - 1 - c, 0, 0)
    hb = pl.pallas_call(
        _scan_bwd_kernel,
        grid=(nc,),
        in_specs=[pl.BlockSpec(blk, rev), pl.BlockSpec(blk, rev)],
        out_specs=pl.BlockSpec(blk, rev),
        out_shape=jax.ShapeDtypeStruct((T, S, LANES), F32),
        scratch_shapes=[pltpu.VMEM((S, LANES), F32)],
        compiler_params=_cparams(("arbitrary",), 48),
        name="lru_scan_bwd",
    )(v3(ab), v3(bb))
    y = pl.pallas_call(
        _scan_fwd_kernel,
        grid=(nc,),
        in_specs=[pl.BlockSpec(blk, fwd)] * 4,
        out_specs=pl.BlockSpec(blk, fwd),
        out_shape=jax.ShapeDtypeStruct((T, S, LANES), BF16),
        scratch_shapes=[pltpu.VMEM((S, LANES), F32)],
        compiler_params=_cparams(("arbitrary",), 48),
        name="lru_scan_fwd",
    )(v3(af), v3(bf), hb, v3(gy))
    return y.reshape(T, BC)


def _fnet_tables(T):
    T2 = FNET_GROUP_W
    T1 = T // T2
    k1 = np.arange(T1)[:, None]
    t1 = np.arange(T1)[None, :]
    stage1 = np.zeros((T2, 2 * T1, 2 * T1), np.float32)
    for t2 in range(T2):
        ph = 2.0 * np.pi * ((k1 * (T2 * t1 + t2)) % T) / T
        c, s = np.cos(ph), np.sin(ph)
        stage1[t2] = np.block([[c, s], [-s, c]])
    n = np.arange(T2)
    ph2 = 2.0 * np.pi * ((n[:, None] * n[None, :]) % T2) / T2
    c2, s2 = np.cos(ph2).astype(np.float32), np.sin(ph2).astype(np.float32)
    return stage1, np.concatenate([c2, s2], axis=1)


def _fnet_kernel(x_ref, m1_ref, cs_ref, o_ref, zr, zi, yo, *, T1):
    T2 = FNET_GROUP_W
    T = T1 * T2
    cs = cs_ref[...]
    for k1 in range(T1):
        blk = slice(k1 * T2, (k1 + 1) * T2)
        z = jnp.dot(x_ref[blk, :], cs, preferred_element_type=F32)
        zr[blk, :] = z[:, :T2]
        zi[blk, :] = -z[:, T2:]
    for t2 in range(T2):
        rows = pl.ds(t2, T1, stride=T2)
        zin = jnp.concatenate([zr[rows, :], zi[rows, :]], axis=0).astype(BF16)
        a = jnp.dot(m1_ref[t2], zin, preferred_element_type=F32)
        zr[rows, :] = a[:T1]
        zi[rows, :] = a[T1:]
    inv = 1.0 / np.sqrt(float(T) * T2)
    for k1 in range(T1):
        blk = slice(k1 * T2, (k1 + 1) * T2)
        ain = jnp.concatenate([zr[blk, :], zi[blk, :]], axis=0).astype(BF16)
        y = jnp.dot(cs, ain, preferred_element_type=F32)
        yo[pl.ds(k1, T2, stride=T1), :] = y * inv
    o_ref[...] = yo[...].astype(BF16)


def _fnet(fr, m1, cs, B, T):
    N, C = fr.shape
    G = C // FNET_GROUP_W
    T1 = T // FNET_GROUP_W
    blk = pl.BlockSpec((T, FNET_GROUP_W), lambda b, g: (b, g))
    return pl.pallas_call(
        functools.partial(_fnet_kernel, T1=T1),
        grid=(B, G),
        in_specs=[blk,
                  pl.BlockSpec(m1.shape, lambda b, g: (0, 0, 0)),
                  pl.BlockSpec(cs.shape, lambda b, g: (0, 0))],
        out_specs=blk,
        out_shape=jax.ShapeDtypeStruct((N, C), BF16),
        scratch_shapes=[pltpu.VMEM((T, FNET_GROUP_W), F32)] * 3,
        compiler_params=_cparams(("parallel", "parallel"), 48),
        name="fnet",
    )(fr, m1, cs)


def _merge_kernel(gate_ref, yna_ref, ylru_ref, yf_ref, x_ref, wna_ref, wlru_ref, wf_ref, wout_ref,
                  gt_ref, g_ref, sh_ref, sc_ref, wr_ref, rb_ref, tri_ref,
                  x1_ref, u2_ref, ls_ref, rw_ref, tc8_ref, toff_ref, tg_ref, cnt_ref, cnt_sc):
    i = pl.program_id(0)

    @pl.when(i == 0)
    def _():
        cnt_sc[...] = jnp.zeros_like(cnt_sc)

    tm, D = x_ref.shape
    gates = gate_ref[...]
    merged = gates[:, 0:D].astype(F32) * jnp.dot(yna_ref[...], wna_ref[...], preferred_element_type=F32)
    merged = merged + gates[:, D:2 * D].astype(F32) * jnp.dot(ylru_ref[...], wlru_ref[...], preferred_element_type=F32)
    merged = merged + gates[:, 2 * D:3 * D].astype(F32) * jnp.dot(yf_ref[...], wf_ref[...], preferred_element_type=F32)
    out = jnp.dot(merged.astype(BF16), wout_ref[...], preferred_element_type=F32)
    x1 = x_ref[...] + gt_ref[0] * out
    x1_ref[...] = x1
    u2 = _rmsnorm_mod(x1, g_ref[...], sh_ref[0], sc_ref[0])
    u2_ref[...] = u2.astype(BF16)

    G, S = N_EXPERT_GROUPS, EXPERTS_PER_GROUP
    logits = _dot_hi(wr_ref[...], u2, ((1,), (1,)))
    score = jax.nn.sigmoid(logits)
    sel = score + rb_ref[...]
    sel_s = [sel[s * G:(s + 1) * G] for s in range(S)]
    sc_s = [score[s * G:(s + 1) * G] for s in range(S)]
    gscore = None
    for a in range(S):
        for b in range(a + 1, S):
            pair = sel_s[a] + sel_s[b]
            gscore = pair if gscore is None else jnp.maximum(gscore, pair)
    giota = lax.broadcasted_iota(I32, (G, tm), 0)
    gmax = jnp.max(gscore, axis=0, keepdims=True)
    gidx = jnp.min(jnp.where(gscore == gmax, giota, G), axis=0, keepdims=True)
    gm = giota == gidx
    cand = [jnp.sum(jnp.where(gm, sel_s[s], 0.0), axis=0, keepdims=True) for s in range(S)]
    raw = [jnp.sum(jnp.where(gm, sc_s[s], 0.0), axis=0, keepdims=True) for s in range(S)]

    def first_argmax(vals):
        m = vals[0]
        for s in range(1, S):
            m = jnp.maximum(m, vals[s])
        idx = jnp.full(m.shape, S - 1, I32)
        for s in range(S - 2, -1, -1):
            idx = jnp.where(vals[s] == m, s, idx)
        return idx

    l0 = first_argmax(cand)
    l1 = first_argmax([jnp.where(l0 == s, -jnp.inf, cand[s]) for s in range(S)])

    def pick(vals, idx):
        out = vals[S - 1]
        for s in range(S - 2, -1, -1):
            out = jnp.where(idx == s, vals[s], out)
        return out

    w0 = pick(raw, l0)
    w1 = pick(raw, l1)
    wsum = w0 + w1
    w0 = w0 / wsum
    w1 = w1 / wsum

    E = G * S
    eiota = lax.broadcasted_iota(I32, (E, tm), 0)
    oh0 = eiota == (l0 * G + gidx)
    oh1 = eiota == (l1 * G + gidx)
    oh = jnp.where(oh0 | oh1, 1.0, 0.0)
    incl = jnp.dot(oh.astype(BF16), tri_ref[...], preferred_element_type=F32)
    cnt = jnp.sum(oh, axis=1, keepdims=True)
    c8 = jnp.floor((cnt + (SUBLANES - 1)) * (1.0 / SUBLANES)) * SUBLANES
    c8b = jnp.broadcast_to(c8, (E, LANES))
    below = (lax.broadcasted_iota(I32, (E, E), 1) < lax.broadcasted_iota(I32, (E, E), 0)).astype(BF16)
    off = jnp.dot(below, c8b.astype(BF16), preferred_element_type=F32)
    local = incl - oh + off[:, 0:1]
    s0 = jnp.sum(jnp.where(oh0, local, 0.0), axis=0, keepdims=True).astype(I32)
    s1 = jnp.sum(jnp.where(oh1, local, 0.0), axis=0, keepdims=True).astype(I32)
    tc8_ref[0] = c8b.astype(I32)
    toff_ref[0] = off.astype(I32)
    tg_ref[0] = cnt_sc[...].astype(I32)
    cnt_sc[...] = cnt_sc[...] + c8b
    cnt_ref[...] = cnt_sc[...].astype(I32)

    row = lax.broadcasted_iota(I32, (SUBLANES, tm), 0)
    ls_ref[...] = jnp.where(row == 0, s0, jnp.where(row == 1, s1, 0))
    rw_ref[...] = jnp.where(row == 0, w0, jnp.where(row == 1, w1, 0.0))


def _merge(gates, y_na, y_lru, y_f, x2, wna, wlru, wf, wout, gt, g, sh, sc, wr, rb, tri, B, T):
    N, D = x2.shape
    C = y_na.shape[1]
    tm = TM_MERGE
    tpb = T // tm
    E = wr.shape[0]
    tok = lambda i: (i, 0)
    scan = lambda i: (i % tpb, i // tpb)
    per_b = lambda i: (i // tpb, 0, 0)
    const2 = lambda i: (0, 0)
    return pl.pallas_call(
        _merge_kernel,
        grid=(N // tm,),
        in_specs=[pl.BlockSpec((tm, 3 * D), tok),
                  pl.BlockSpec((tm, C), tok),
                  pl.BlockSpec((tm, C), scan),
                  pl.BlockSpec((tm, C), tok),
                  pl.BlockSpec((tm, D), tok),
                  pl.BlockSpec((C, D), const2), pl.BlockSpec((C, D), const2), pl.BlockSpec((C, D), const2),
                  pl.BlockSpec((D, D), const2),
                  pl.BlockSpec((1, 1, D), per_b),
                  pl.BlockSpec((1, D), const2),
                  pl.BlockSpec((1, 1, D), per_b),
                  pl.BlockSpec((1, 1, D), per_b),
                  pl.BlockSpec((E, D), const2),
                  pl.BlockSpec((E, 1), const2),
                  pl.BlockSpec((tm, tm), const2)],
        out_specs=[pl.BlockSpec((tm, D), tok),
                   pl.BlockSpec((tm, D), tok),
                   pl.BlockSpec((SUBLANES, tm), lambda i: (0, i)),
                   pl.BlockSpec((SUBLANES, tm), lambda i: (0, i)),
                   pl.BlockSpec((1, E, LANES), lambda i: (i, 0, 0)),
                   pl.BlockSpec((1, E, LANES), lambda i: (i, 0, 0)),
                   pl.BlockSpec((1, E, LANES), lambda i: (i, 0, 0)),
                   pl.BlockSpec((E, LANES), const2)],
        out_shape=[jax.ShapeDtypeStruct((N, D), F32),
                   jax.ShapeDtypeStruct((N, D), BF16),
                   jax.ShapeDtypeStruct((SUBLANES, N), I32),
                   jax.ShapeDtypeStruct((SUBLANES, N), F32),
                   jax.ShapeDtypeStruct((N // tm, E, LANES), I32),
                   jax.ShapeDtypeStruct((N // tm, E, LANES), I32),
                   jax.ShapeDtypeStruct((N // tm, E, LANES), I32),
                   jax.ShapeDtypeStruct((E, LANES), I32)],
        scratch_shapes=[pltpu.VMEM((E, LANES), F32)],
        compiler_params=_cparams(("arbitrary",), 48),
        name="merge_route",
    )(gates, y_na, y_lru, y_f, x2, wna, wlru, wf, wout, gt, g, sh, sc, wr, rb, tri)


def _for_run_pieces(n_rows, fn):
    def chunk(j, c):
        fn(j * RUN_CHUNK, RUN_CHUNK)
        return c

    lax.fori_loop(0, n_rows // RUN_CHUNK, chunk, 0)
    size = RUN_CHUNK // 2
    while size >= SUBLANES:
        pl.when((n_rows & size) != 0)(functools.partial(fn, (n_rows // (2 * size)) * (2 * size), size))
        size //= 2


def _run_copies(tc8_ref, toff_ref, tg_ref, ps_ref, tile, hbm, stage, sem, to_hbm, wait):
    for e in range(N_EXPERTS):
        idx = tile * N_EXPERTS + e
        loc0 = toff_ref[idx]
        hbm0 = ps_ref[e] + tg_ref[idx]

        def piece(off, size, loc0=loc0, hbm0=hbm0):
            loc = stage.at[pl.ds(pl.multiple_of(loc0 + off, SUBLANES), size)]
            glob = hbm.at[pl.ds(pl.multiple_of(hbm0 + off, SUBLANES), size)]
            cp = pltpu.make_async_copy(loc, glob, sem) if to_hbm else pltpu.make_async_copy(glob, loc, sem)
            if wait:
                cp.wait()
            else:
                cp.start()

        _for_run_pieces(tc8_ref[idx], piece)


def _dispatch_kernel(tc8_ref, toff_ref, tg_ref, ps_ref, pad_ref, nu_ref, ls_ref, u_ref, xs_out, stage, zbuf, sem,
                     *, n_tail):
    i = pl.program_id(0)
    tm = u_ref.shape[0]
    nb = xs_out.shape[0] // MOE_ROWS

    @pl.when(i == 0)
    def _():
        zbuf[...] = jnp.zeros_like(zbuf)

        def zero_copy(blk):
            return pltpu.make_async_copy(zbuf, xs_out.at[pl.ds(pl.multiple_of(blk * MOE_ROWS, MOE_ROWS), MOE_ROWS)], sem)

        def guarded(fn):
            for e in range(N_EXPERTS):
                pl.when(pad_ref[e] >= 0)(lambda e=e: fn(zero_copy(pad_ref[e])))
            for k in range(n_tail):
                pl.when(nb - 1 - k >= nu_ref[0])(lambda k=k: fn(zero_copy(nb - 1 - k)))

        guarded(lambda cp: cp.start())
        guarded(lambda cp: cp.wait())

    slots = ls_ref[...]
    riota = lax.broadcasted_iota(I32, (stage.shape[0], tm), 0)
    hit = (riota == slots[0:1]) | (riota == slots[1:2])
    perm = jnp.where(hit, 1.0, 0.0).astype(BF16)
    stage[...] = jnp.dot(perm, u_ref[...], preferred_element_type=F32)
    for wait in (False, True):
        _run_copies(tc8_ref, toff_ref, tg_ref, ps_ref, i, xs_out, stage, sem, to_hbm=True, wait=wait)


def _dispatch(tables, pstart, pad_blk, n_used, ls, u2, P, n_tail):
    N, D = u2.shape
    tm = TM_MOE
    n_pref = len(tables) + 3
    grid_spec = pltpu.PrefetchScalarGridSpec(
        num_scalar_prefetch=n_pref,
        grid=(N // tm,),
        in_specs=[pl.BlockSpec((SUBLANES, tm), lambda i, *_: (0, i)),
                  pl.BlockSpec((tm, D), lambda i, *_: (i, 0))],
        out_specs=pl.BlockSpec(memory_space=pl.ANY),
        scratch_shapes=[pltpu.VMEM((MOE_STAGE_ROWS, D), F32), pltpu.VMEM((MOE_ROWS, D), F32),
                        pltpu.SemaphoreType.DMA(())],
    )
    return pl.pallas_call(
        functools.partial(_dispatch_kernel, n_tail=n_tail),
        grid_spec=grid_spec,
        out_shape=jax.ShapeDtypeStruct((P, D), F32),
        compiler_params=_cparams(("arbitrary",), 40),
        name="moe_dispatch",
    )(*tables, pstart, pad_blk, n_used, ls, u2)


def _expert_kernel(be_ref, nu_ref, xs_ref, wg_ref, wu_ref, wd_ref, ys_ref, wg_sc, wu_sc, wd_sc):
    i = pl.program_id(0)
    prev = be_ref[jnp.maximum(i - 1, 0)]
    fresh = (i == 0) | (be_ref[i] != prev)

    @pl.when(fresh)
    def _():
        wg_sc[...] = wg_ref[0, 0].astype(BF16)
        wu_sc[...] = wu_ref[0, 0].astype(BF16)
        wd_sc[...] = wd_ref[0, 0].astype(BF16)

    @pl.when(i < nu_ref[0])
    def _():
        xb = xs_ref[...].astype(BF16)
        hg = jnp.dot(xb, wg_sc[...], preferred_element_type=F32)
        hu = jnp.dot(xb, wu_sc[...], preferred_element_type=F32)
        h = (hg * _sigmoid(hg)) * hu
        ys_ref[...] = jnp.dot(h.astype(BF16), wd_sc[...], preferred_element_type=F32)

    @pl.when(i >= nu_ref[0])
    def _():
        ys_ref[...] = jnp.zeros_like(ys_ref)


def _experts(block_e, n_used, xs, w_gate, w_up, w_down, layer):
    P, D = xs.shape
    _, E, _, DE = w_gate.shape
    nb = P // MOE_ROWS
    wmap = lambda i, be, nu: (layer, be[i], 0, 0)
    grid_spec = pltpu.PrefetchScalarGridSpec(
        num_scalar_prefetch=2,
        grid=(nb,),
        in_specs=[pl.BlockSpec((MOE_ROWS, D), lambda i, be, nu: (i, 0)),
                  pl.BlockSpec((1, 1, D, DE), wmap),
                  pl.BlockSpec((1, 1, D, DE), wmap),
                  pl.BlockSpec((1, 1, DE, D), wmap)],
        out_specs=pl.BlockSpec((MOE_ROWS, D), lambda i, be, nu: (i, 0)),
        scratch_shapes=[pltpu.VMEM((D, DE), BF16), pltpu.VMEM((D, DE), BF16), pltpu.VMEM((DE, D), BF16)],
    )
    return pl.pallas_call(
        _expert_kernel,
        grid_spec=grid_spec,
        out_shape=jax.ShapeDtypeStruct((P, D), F32),
        compiler_params=_cparams(("arbitrary",), 48),
        name="moe_experts",
    )(block_e, n_used, xs, w_gate, w_up, w_down)


def _combine_kernel(tc8_ref, toff_ref, tg_ref, ps_ref, ls_ref, rw_ref, x_ref, gt_ref, fg_ref, ys_ref, o_ref,
                    stage, sem, *, final):
    i = pl.program_id(0)
    tm = x_ref.shape[0]

    @pl.when(i == 0)
    def _():
        stage[...] = jnp.zeros_like(stage)

    for wait in (False, True):
        _run_copies(tc8_ref, toff_ref, tg_ref, ps_ref, i, ys_ref, stage, sem, to_hbm=False, wait=wait)
    rows = stage[...].astype(BF16)
    liota = lax.broadcasted_iota(I32, (tm, stage.shape[0]), 1)
    slots = ls_ref[...]
    w = rw_ref[...]
    y = None
    for k in range(TOP_K):
        pick = jnp.where(liota == slots[:, k:k + 1], 1.0, 0.0).astype(BF16)
        yk = jnp.dot(pick, rows, preferred_element_type=F32) * w[:, k:k + 1]
        y = yk if y is None else y + yk
    x2 = x_ref[...] + gt_ref[0] * y
    if final:
        x2 = (x2 * lax.rsqrt(jnp.mean(x2 * x2, axis=-1, keepdims=True) + EPS)) * fg_ref[...]
    o_ref[...] = x2


def _combine(tables, pstart, ls_t, rw_t, x1, gt, final_g, ys, T, final):
    N, D = x1.shape
    tm = TM_MOE
    tpb = T // tm
    grid_spec = pltpu.PrefetchScalarGridSpec(
        num_scalar_prefetch=len(tables) + 1,
        grid=(N // tm,),
        in_specs=[pl.BlockSpec((tm, SUBLANES), lambda i, *_: (i, 0)),
                  pl.BlockSpec((tm, SUBLANES), lambda i, *_: (i, 0)),
                  pl.BlockSpec((tm, D), lambda i, *_: (i, 0)),
                  pl.BlockSpec((1, 1, D), lambda i, *_: (i // tpb, 0, 0)),
                  pl.BlockSpec((1, D), lambda i, *_: (0, 0)),
                  pl.BlockSpec(memory_space=pl.ANY)],
        out_specs=pl.BlockSpec((tm, D), lambda i, *_: (i, 0)),
        scratch_shapes=[pltpu.VMEM((MOE_STAGE_ROWS, D), F32), pltpu.SemaphoreType.DMA(())],
    )
    return pl.pallas_call(
        functools.partial(_combine_kernel, final=final),
        grid_spec=grid_spec,
        out_shape=jax.ShapeDtypeStruct((N, D), F32),
        compiler_params=_cparams(("arbitrary",), 40),
        name="moe_combine",
    )(*tables, pstart, ls_t, rw_t, x1, gt, final_g, ys)


def _moe_plan(cnt, nb):
    G, S = N_EXPERT_GROUPS, EXPERTS_PER_GROUP
    counts = cnt[:, 0]
    padded = (counts + MOE_ROWS - 1) // MOE_ROWS * MOE_ROWS
    pend = jnp.cumsum(padded).astype(I32)
    pstart = pend - padded
    pad_blk = jnp.where(padded > 0, pend // MOE_ROWS - 1, -1).astype(I32)
    block_start = jnp.arange(nb, dtype=I32) * MOE_ROWS
    block_row = jnp.minimum(jnp.sum((pend[None, :] <= block_start[:, None]).astype(I32), axis=1), G * S - 1)
    block_e = (block_row % G) * S + block_row // G
    n_used = pend[-1:] // MOE_ROWS
    return pstart, pad_blk, block_e.astype(I32), n_used.astype(I32)


def _block_diag(w):
    nb, bw, _ = w.shape
    eye = jnp.eye(nb, dtype=w.dtype)
    return jnp.einsum('nij,nm->nimj', w, eye).reshape(nb * bw, nb * bw)


def kernel(x, c, ada_w, ada_b, norm_mix_g, norm_ffn_g, w_in, w_branch_gate, b_branch_gate,
           na_rpb, lru_conv_w, lru_conv_b, lru_w_r, lru_b_r, lru_w_i, lru_b_i, lru_lambda,
           w_proj_na, w_proj_lru, w_proj_fnet, w_out, w_router, router_bias,
           w_exp_gate, w_exp_up, w_exp_down, final_g):
    B, T, D = x.shape
    L = ada_w.shape[0]
    N = B * T
    E = N_EXPERTS
    G, S = N_EXPERT_GROUPS, EXPERTS_PER_GROUP
    A = N * TOP_K
    assert TM_MERGE == TM_MOE
    max_rows = A + (N // TM_MOE) * E * (SUBLANES - 1) + E * (MOE_ROWS - 1)
    nb = -(-max_rows // MOE_ROWS)
    P = nb * MOE_ROWS

    mod = _ada_mod(c, ada_w, ada_b)
    m1_np, cs_np = _fnet_tables(T)
    m1 = jnp.asarray(m1_np).astype(BF16)
    cs = jnp.asarray(cs_np).astype(BF16)
    tri = (lax.broadcasted_iota(I32, (TM_MERGE, TM_MERGE), 0)
           <= lax.broadcasted_iota(I32, (TM_MERGE, TM_MERGE), 1)).astype(BF16)
    wr = w_router.T.reshape(G, S, D).transpose(1, 0, 2).reshape(E, D)
    rb = router_bias.reshape(G, S).T.reshape(E, 1)
    fg = final_g.reshape(1, D)

    x2 = x.reshape(N, D)
    for l in range(L):
        sh_mix, sc_mix, gt_mix, sh_ffn, sc_ffn, gt_ffn = [m.reshape(B, 1, D) for m in jnp.split(mod[l], 6, axis=-1)]
        wcat = jnp.concatenate([w_in[l], w_branch_gate[l]], axis=1).astype(BF16)
        q, k, v, xr, gy, fr, gates = _mix_in(x2, norm_mix_g[l].reshape(1, D), sh_mix, sc_mix, wcat,
                                             b_branch_gate[l].reshape(1, -1), B, T)
        y_na = _na(q, k, v, _na_bias_table(na_rpb[l]), B, T)

        wg = jnp.stack([jnp.concatenate([_block_diag(lru_w_r[l, d]), _block_diag(lru_w_i[l, d])], axis=1)
                        for d in range(2)]).astype(BF16)
        bgl = jnp.stack([jnp.concatenate([lru_b_r[l, d], lru_b_i[l, d]]) for d in range(2)])[:, None, :]
        af, bf, ab, bb = _lru_gates(xr, lru_conv_w[l], lru_conv_b[l].reshape(1, -1), wg, bgl,
                                    lru_lambda[l][:, None, :], B, T)
        y_lru = _lru_scans(af, bf, ab, bb, gy)
        y_f = _fnet(fr, m1, cs, B, T)

        x1, u2, ls, rw, tc8, toff, tg, cnt = _merge(
            gates, y_na, y_lru, y_f, x2,
            w_proj_na[l].astype(BF16), w_proj_lru[l].astype(BF16),
            w_proj_fnet[l].astype(BF16), w_out[l].astype(BF16),
            gt_mix, norm_ffn_g[l].reshape(1, D), sh_ffn, sc_ffn, wr, rb, tri, B, T)
        tables = [t[:, :, 0].reshape(-1) for t in (tc8, toff, tg)]
        pstart, pad_blk, block_e, n_used = _moe_plan(cnt, nb)
        xs = _dispatch(tables, pstart, pad_blk, n_used, ls, u2, P, nb - A // MOE_ROWS)
        ys = _experts(block_e, n_used, xs, w_exp_gate, w_exp_up, w_exp_down, l)
        x2 = _combine(tables, pstart, ls.T, rw.T, x1, gt_ffn, fg, ys, T, final=(l == L - 1))
    return x2.reshape(B, T, D)
```

```python
import functools

import numpy as np
import jax
import jax.numpy as jnp
from jax import lax
from jax.experimental import pallas as pl
from jax.experimental.pallas import tpu as pltpu

F32 = jnp.float32
BF16 = jnp.bfloat16
I32 = jnp.int32

GRID_W = 64
NA_HEAD_DIM = 64
NA_WIN_ROWS = 8
NA_WIN_COLS = 16
NA_HEADS_PER_TILE = 4
LRU_BLOCKS = 8
LRU_C = 8.0
CONV_W = 4
FNET_GROUP_W = 128
N_EXPERTS = 32
N_EXPERT_GROUPS = 8
EXPERTS_PER_GROUP = 4
TOP_K = 2
EPS = 1e-6
MASK_VALUE = -1e30

V7X_VMEM_BYTES = 64 * 1024 * 1024
LANES = 128
SUBLANES = 8

TM_MIX = 1024
TN_MIX = 1024
MIX_ROW_CHUNK = 512
NA_ROWS_PER_STEP = 8
TM_LRU = 512
T_SCAN = 512
TM_MERGE = 512
TM_MOE = 512
MOE_ROWS = 512
RUN_CHUNK = 64
RUN_ALIGN = 16
MOE_STAGE_ROWS = -(-(TM_MOE * TOP_K + N_EXPERTS * (RUN_ALIGN - 1)) // 256) * 256


def _cparams(semantics, vmem_mb):
    assert vmem_mb * 1024 * 1024 < V7X_VMEM_BYTES
    return pltpu.CompilerParams(dimension_semantics=semantics, vmem_limit_bytes=vmem_mb * 1024 * 1024)


def _split2(a):
    hi = a.astype(BF16)
    lo = (a - hi.astype(F32)).astype(BF16)
    return hi, lo


def _dot_hi(a, b, dims):
    a_hi, a_lo = _split2(a)
    b_hi, b_lo = _split2(b)
    dn = (dims, ((), ()))
    out = lax.dot_general(a_hi, b_hi, dn, preferred_element_type=F32)
    out = out + lax.dot_general(a_hi, b_lo, dn, preferred_element_type=F32)
    out = out + lax.dot_general(a_lo, b_hi, dn, preferred_element_type=F32)
    return out


def _sigmoid(x):
    return 0.5 * jnp.tanh(0.5 * x) + 0.5


def _rmsnorm_mod(x, g, sh, sc):
    y = x * lax.rsqrt(jnp.mean(x * x, axis=-1, keepdims=True) + EPS)
    return (y * g) * (1.0 + sc) + sh


def _ada_kernel(c_ref, w_ref, b_ref, o_ref):
    c = c_ref[...]
    ca = c * jax.nn.sigmoid(c)
    o_ref[0] = _dot_hi(ca, w_ref[0], ((1,), (0,))) + b_ref[0]


def _ada_mod(c, ada_w, ada_b):
    L, D, D6 = ada_w.shape
    B = c.shape[0]
    bp = -(-B // SUBLANES) * SUBLANES
    cp = jnp.pad(c, ((0, bp - B), (0, 0)))
    tn = D6 // 4
    out = pl.pallas_call(
        _ada_kernel,
        grid=(L, D6 // tn),
        in_specs=[pl.BlockSpec((bp, D), lambda l, j: (0, 0)),
                  pl.BlockSpec((1, D, tn), lambda l, j: (l, 0, j)),
                  pl.BlockSpec((1, 1, tn), lambda l, j: (l, 0, j))],
        out_specs=pl.BlockSpec((1, bp, tn), lambda l, j: (l, 0, j)),
        out_shape=jax.ShapeDtypeStruct((L, bp, D6), F32),
        compiler_params=_cparams(("parallel", "parallel"), 40),
        name="ada_mod",
    )(cp, ada_w, ada_b.reshape(L, 1, D6))
    return out[:, :B]


def _mixin_kernel(x_ref, g_ref, sh_ref, sc_ref, w_ref, bg_ref,
                  q_ref, k_ref, v_ref, xr_ref, gy_ref, fr_ref, gate_ref, u_sc):
    j = pl.program_id(1)

    @pl.when(j == 0)
    def _():
        u_sc[...] = _rmsnorm_mod(x_ref[...], g_ref[...], sh_ref[0], sc_ref[0]).astype(BF16)

    tm = u_sc.shape[0]
    W = q_ref.shape[1]
    lo, hi = slice(0, W), slice(W, 2 * W)

    def per_chunk(store):
        for c in range(tm // MIX_ROW_CHUNK):
            rows = slice(c * MIX_ROW_CHUNK, (c + 1) * MIX_ROW_CHUNK)
            store(rows, jnp.dot(u_sc[rows, :], w_ref[...], preferred_element_type=F32))

    @pl.when(j == 0)
    def _():
        def store(rows, acc):
            q_ref[rows, :] = acc[:, lo].astype(BF16)
            k_ref[rows, :] = acc[:, hi].astype(BF16)
        per_chunk(store)

    @pl.when(j == 1)
    def _():
        def store(rows, acc):
            v_ref[rows, :] = acc[:, lo].astype(BF16)
            xr_ref[rows, :] = acc[:, hi]
        per_chunk(store)

    @pl.when(j == 2)
    def _():
        def store(rows, acc):
            gy_ref[rows, :] = jax.nn.gelu(acc[:, lo]).astype(BF16)
            fr_ref[rows, :] = acc[:, hi].astype(BF16)
        per_chunk(store)

    @pl.when(j >= 3)
    def _():
        def store(rows, acc):
            gate_ref[rows, :] = _sigmoid(acc + bg_ref[...]).astype(BF16)
        per_chunk(store)


def _mix_in(x2, g, sh, sc, wcat, bgate, B, T):
    N, D = x2.shape
    tm, tn = TM_MIX, TN_MIX
    tpb = T // tm
    W = tn // 2
    n_proj = 3
    n_gate = bgate.shape[1] // tn
    tok = lambda i, j: (i, 0)
    scan = lambda i, j: (i % tpb, i // tpb)
    per_b = lambda i, j: (i // tpb, 0, 0)
    out_shape = [jax.ShapeDtypeStruct((N, W), BF16)] * 3 + [
        jax.ShapeDtypeStruct((N, W), F32),
        jax.ShapeDtypeStruct((T, B * W), BF16),
        jax.ShapeDtypeStruct((N, W), BF16),
        jax.ShapeDtypeStruct((N, n_gate * tn), BF16)]
    out_specs = [pl.BlockSpec((tm, W), tok)] * 3 + [
        pl.BlockSpec((tm, W), tok),
        pl.BlockSpec((tm, W), scan),
        pl.BlockSpec((tm, W), tok),
        pl.BlockSpec((tm, tn), lambda i, j: (i, jnp.maximum(j - n_proj, 0)))]
    return pl.pallas_call(
        _mixin_kernel,
        grid=(N // tm, n_proj + n_gate),
        in_specs=[pl.BlockSpec((tm, D), tok),
                  pl.BlockSpec((1, D), lambda i, j: (0, 0)),
                  pl.BlockSpec((1, 1, D), per_b),
                  pl.BlockSpec((1, 1, D), per_b),
                  pl.BlockSpec((D, tn), lambda i, j: (0, j)),
                  pl.BlockSpec((1, tn), lambda i, j: (0, jnp.maximum(j - n_proj, 0)))],
        out_specs=out_specs,
        out_shape=out_shape,
        scratch_shapes=[pltpu.VMEM((tm, D), BF16)],
        compiler_params=_cparams(("parallel", "arbitrary"), 48),
        name="mix_in",
    )(x2, g, sh, sc, wcat, bgate)


def _na_kernel(q_ref, kp_ref, kc_ref, kn_ref, vp_ref, vc_ref, vn_ref, bias_ref, o_ref, kb, vb):
    j = pl.program_id(1)
    nj = pl.num_programs(1)
    R = NA_ROWS_PER_STEP
    half = R // 2
    W = GRID_W
    hw = half * W
    tq = R * W
    win = NA_WIN_ROWS * W
    HT = NA_HEADS_PER_TILE
    lt = HT * NA_HEAD_DIM
    kb[0:hw] = kp_ref[...]
    kb[hw:hw + tq] = kc_ref[...]
    kb[hw + tq:2 * hw + tq] = kn_ref[...]
    vb[0:hw] = vp_ref[...]
    vb[hw:hw + tq] = vc_ref[...]
    vb[hw + tq:2 * hw + tq] = vn_ref[...]
    head_of_lane = lax.broadcasted_iota(I32, (W, lt), 1) // NA_HEAD_DIM
    n_tiles = q_ref.shape[1] // lt
    scale = jnp.asarray(NA_HEAD_DIM ** -0.5, BF16)
    for dr in range(R):
        lo = jnp.where(j == 0, max(dr, half), jnp.where(j == nj - 1, min(dr, half), dr))
        var = lo - dr + (half - 1)
        start = pl.multiple_of(lo * W, W)
        for ht in range(n_tiles):
            cs = slice(ht * lt, (ht + 1) * lt)
            qrow = q_ref[dr * W:(dr + 1) * W, cs] * scale
            qbd = jnp.concatenate(
                [jnp.where(head_of_lane == h, qrow, jnp.zeros_like(qrow)) for h in range(HT)], axis=0)
            kwin = kb[pl.ds(start, win), cs]
            s = lax.dot_general(qbd, kwin, (((1,), (1,)), ((), ())), preferred_element_type=F32)
            s = s + bias_ref[var, ht]
            m = jnp.max(s, axis=-1, keepdims=True)
            p = jnp.exp(s - m)
            l = jnp.sum(p, axis=-1, keepdims=True)
            vwin = vb[pl.ds(start, win), cs]
            o = jnp.dot(p.astype(BF16), vwin, preferred_element_type=F32) / l
            orow = jnp.where(head_of_lane == 0, o[0:W], 0.0)
            for h in range(1, HT):
                orow = orow + jnp.where(head_of_lane == h, o[h * W:(h + 1) * W], 0.0)
            o_ref[dr * W:(dr + 1) * W, cs] = orow.astype(BF16)


def _na_bias_table(rpb):
    H = rpb.shape[0]
    W, KR, KC = GRID_W, NA_WIN_ROWS, NA_WIN_COLS
    w = np.arange(W)
    cstart = np.clip(w - KC // 2, 0, W - KC)
    wk = np.arange(W)
    col_ok = (wk[None, :] >= cstart[:, None]) & (wk[None, :] < cstart[:, None] + KC)
    cb = wk[None, :] - w[:, None] + (KC - 1)
    onehot = ((cb[None] == np.arange(2 * KC - 1)[:, None, None]) & col_ok[None]).astype(np.float32)
    toep = jnp.einsum('hac,cwx->hawx', rpb.astype(F32), jnp.asarray(onehot), precision=lax.Precision.HIGHEST)
    toep = jnp.where(jnp.asarray(col_ok)[None, None], toep, MASK_VALUE)
    tab = jnp.stack([toep[:, v:v + KR] for v in range(KR)])
    tab = tab.transpose(0, 1, 3, 2, 4)
    HT = NA_HEADS_PER_TILE
    return tab.reshape(KR, H // HT, HT * W, KR * W)


def _na(q, k, v, bias, B, T):
    N, C = q.shape
    R = NA_ROWS_PER_STEP
    tq = R * GRID_W
    hw = tq // 2
    nj = T // tq
    nh = T // hw
    qmap = lambda b, j: (b * nj + j, 0)
    pmap = lambda b, j: (b * nh + jnp.maximum(2 * j - 1, 0), 0)
    nmap = lambda b, j: (b * nh + jnp.minimum(2 * j + 2, nh - 1), 0)
    return pl.pallas_call(
        _na_kernel,
        grid=(B, nj),
        in_specs=[pl.BlockSpec((tq, C), qmap),
                  pl.BlockSpec((hw, C), pmap), pl.BlockSpec((tq, C), qmap), pl.BlockSpec((hw, C), nmap),
                  pl.BlockSpec((hw, C), pmap), pl.BlockSpec((tq, C), qmap), pl.BlockSpec((hw, C), nmap),
                  pl.BlockSpec(bias.shape, lambda b, j: (0, 0, 0, 0))],
        out_specs=pl.BlockSpec((tq, C), qmap),
        out_shape=jax.ShapeDtypeStruct((N, C), BF16),
        scratch_shapes=[pltpu.VMEM((2 * tq, C), BF16), pltpu.VMEM((2 * tq, C), BF16)],
        compiler_params=_cparams(("parallel", "parallel"), 48),
        name="na_attention",
    )(q, k, k, k, v, v, v, bias)


def _lru_gate_kernel(x_ref, xp_ref, xn_ref, cw_ref, cb_ref, wg_ref, bg_ref, lam_ref,
                     af_ref, bf_ref, ab_ref, bb_ref, xpad, *, tpb):
    i = pl.program_id(0)
    tm, C = x_ref.shape
    H = SUBLANES
    first = (i % tpb) == 0
    last = (i % tpb) == tpb - 1
    xpad[0:H] = jnp.where(first, 0.0, xp_ref[...])
    xpad[H:H + tm] = x_ref[...]
    xpad[H + tm:2 * H + tm] = jnp.where(last, 0.0, xn_ref[...])
    left = CONV_W // 2
    xc = xpad[H - left:H - left + tm] * cw_ref[0:1]
    for t in range(1, CONV_W):
        xc = xc + xpad[H - left + t:H - left + t + tm] * cw_ref[t:t + 1]
    xc = xc + cb_ref[...]
    xcb = xc.astype(BF16)
    for d, (a_ref, b_ref) in enumerate(((af_ref, bf_ref), (ab_ref, bb_ref))):
        g = jnp.dot(xcb, wg_ref[d], preferred_element_type=F32) + bg_ref[d]
        r = _sigmoid(g[:, :C])
        ig = _sigmoid(g[:, C:])
        log_a = -LRU_C * r * jax.nn.softplus(-lam_ref[d])
        a = jnp.exp(log_a)
        a_ref[...] = a
        b_ref[...] = jnp.sqrt(1.0 - a * a) * (ig * xc)


def _lru_gates(xr, conv_w, conv_b, wg, bg, lam, B, T):
    N, C = xr.shape
    tm = TM_LRU
    tpb = T // tm
    nb8 = N // SUBLANES
    r8 = tm // SUBLANES
    scan = lambda i: (i % tpb, i // tpb)
    const2 = lambda i: (0, 0)
    const3 = lambda i: (0, 0, 0)
    outs = pl.pallas_call(
        functools.partial(_lru_gate_kernel, tpb=tpb),
        grid=(N // tm,),
        in_specs=[pl.BlockSpec((tm, C), lambda i: (i, 0)),
                  pl.BlockSpec((SUBLANES, C), lambda i: (jnp.maximum(i * r8 - 1, 0), 0)),
                  pl.BlockSpec((SUBLANES, C), lambda i: (jnp.minimum((i + 1) * r8, nb8 - 1), 0)),
                  pl.BlockSpec((CONV_W, C), const2),
                  pl.BlockSpec((1, C), const2),
                  pl.BlockSpec((2, C, 2 * C), const3),
                  pl.BlockSpec((2, 1, 2 * C), const3),
                  pl.BlockSpec((2, 1, C), const3)],
        out_specs=[pl.BlockSpec((tm, C), scan)] * 4,
        out_shape=[jax.ShapeDtypeStruct((T, B * C), F32)] * 4,
        scratch_shapes=[pltpu.VMEM((tm + 2 * SUBLANES, C), F32)],
        compiler_params=_cparams(("parallel",), 48),
        name="lru_gates",
    )(xr, xr, xr, conv_w, conv_b, wg, bg, lam)
    return outs


def _scan_bwd_kernel(a_ref, b_ref, h_ref, carry):
    @pl.when(pl.program_id(0) == 0)
    def _():
        carry[...] = jnp.zeros_like(carry)

    tc = a_ref.shape[0]

    def body(s, h):
        t = tc - 1 - s
        h = a_ref[t] * h + b_ref[t]
        h_ref[t] = h
        return h

    carry[...] = lax.fori_loop(0, tc, body, carry[...], unroll=8)


def _scan_fwd_kernel(a_ref, b_ref, hb_ref, gy_ref, y_ref, carry):
    @pl.when(pl.program_id(0) == 0)
    def _():
        carry[...] = jnp.zeros_like(carry)

    tc = a_ref.shape[0]

    def body(t, h):
        h = a_ref[t] * h + b_ref[t]
        y_ref[t] = ((h + hb_ref[t]) * gy_ref[t].astype(F32)).astype(BF16)
        return h

    carry[...] = lax.fori_loop(0, tc, body, carry[...], unroll=8)


def _lru_scans(af, bf, ab, bb, gy):
    T, BC = af.shape
    S = BC // LANES
    tc = T_SCAN
    nc = T // tc
    v3 = lambda z: z.reshape(T, S, LANES)
    blk = (tc, S, LANES)
    fwd = lambda c: (c, 0, 0)
    rev = lambda c: (nc - 1 - c, 0, 0)
    hb = pl.pallas_call(
        _scan_bwd_kernel,
        grid=(nc,),
        in_specs=[pl.BlockSpec(blk, rev), pl.BlockSpec(blk, rev)],
        out_specs=pl.BlockSpec(blk, rev),
        out_shape=jax.ShapeDtypeStruct((T, S, LANES), F32),
        scratch_shapes=[pltpu.VMEM((S, LANES), F32)],
        compiler_params=_cparams(("arbitrary",), 48),
        name="lru_scan_bwd",
    )(v3(ab), v3(bb))
    y = pl.pallas_call(
        _scan_fwd_kernel,
        grid=(nc,),
        in_specs=[pl.BlockSpec(blk, fwd)] * 4,
        out_specs=pl.BlockSpec(blk, fwd),
        out_shape=jax.ShapeDtypeStruct((T, S, LANES), BF16),
        scratch_shapes=[pltpu.VMEM((S, LANES), F32)],
        compiler_params=_cparams(("arbitrary",), 48),
        name="lru_scan_fwd",
    )(v3(af), v3(bf), hb, v3(gy))
    return y.reshape(T, BC)


def _fnet_tables(T):
    T2 = FNET_GROUP_W
    T1 = T // T2
    k1 = np.arange(T1)[:, None]
    t1 = np.arange(T1)[None, :]
    stage1 = np.zeros((T2, 2 * T1, 2 * T1), np.float32)
    for t2 in range(T2):
        ph = 2.0 * np.pi * ((k1 * (T2 * t1 + t2)) % T) / T
        c, s = np.cos(ph), np.sin(ph)
        stage1[t2] = np.block([[c, s], [-s, c]])
    n = np.arange(T2)
    ph2 = 2.0 * np.pi * ((n[:, None] * n[None, :]) % T2) / T2
    c2, s2 = np.cos(ph2).astype(np.float32), np.sin(ph2).astype(np.float32)
    return stage1, np.concatenate([c2, s2], axis=1)


def _fnet_kernel(x_ref, m1_ref, cs_ref, o_ref, zr, zi, yo, *, T1):
    T2 = FNET_GROUP_W
    T = T1 * T2
    cs = cs_ref[...]
    for k1 in range(T1):
        blk = slice(k1 * T2, (k1 + 1) * T2)
        z = jnp.dot(x_ref[blk, :], cs, preferred_element_type=F32)
        zr[blk, :] = z[:, :T2]
        zi[blk, :] = -z[:, T2:]
    for t2 in range(T2):
        rows = pl.ds(t2, T1, stride=T2)
        zin = jnp.concatenate([zr[rows, :], zi[rows, :]], axis=0).astype(BF16)
        a = jnp.dot(m1_ref[t2], zin, preferred_element_type=F32)
        zr[rows, :] = a[:T1]
        zi[rows, :] = a[T1:]
    inv = 1.0 / np.sqrt(float(T) * T2)
    for k1 in range(T1):
        blk = slice(k1 * T2, (k1 + 1) * T2)
        ain = jnp.concatenate([zr[blk, :], zi[blk, :]], axis=0).astype(BF16)
        y = jnp.dot(cs, ain, preferred_element_type=F32)
        yo[pl.ds(k1, T2, stride=T1), :] = y * inv
    o_ref[...] = yo[...].astype(BF16)


def _fnet(fr, m1, cs, B, T):
    N, C = fr.shape
    G = C // FNET_GROUP_W
    T1 = T // FNET_GROUP_W
    blk = pl.BlockSpec((T, FNET_GROUP_W), lambda b, g: (b, g))
    return pl.pallas_call(
        functools.partial(_fnet_kernel, T1=T1),
        grid=(B, G),
        in_specs=[blk,
                  pl.BlockSpec(m1.shape, lambda b, g: (0, 0, 0)),
                  pl.BlockSpec(cs.shape, lambda b, g: (0, 0))],
        out_specs=blk,
        out_shape=jax.ShapeDtypeStruct((N, C), BF16),
        scratch_shapes=[pltpu.VMEM((T, FNET_GROUP_W), F32)] * 3,
        compiler_params=_cparams(("parallel", "parallel"), 48),
        name="fnet",
    )(fr, m1, cs)


def _merge_kernel(gate_ref, yna_ref, ylru_ref, yf_ref, x_ref, wna_ref, wlru_ref, wf_ref, wout_ref,
                  gt_ref, g_ref, sh_ref, sc_ref, wr_ref, rb_ref, tri_ref,
                  x1_ref, u2_ref, ls_ref, rw_ref, tc8_ref, toff_ref, tg_ref, cnt_ref, cnt_sc):
    i = pl.program_id(0)

    @pl.when(i == 0)
    def _():
        cnt_sc[...] = jnp.zeros_like(cnt_sc)

    tm, D = x_ref.shape
    gates = gate_ref[...]
    merged = gates[:, 0:D].astype(F32) * jnp.dot(yna_ref[...], wna_ref[...], preferred_element_type=F32)
    merged = merged + gates[:, D:2 * D].astype(F32) * jnp.dot(ylru_ref[...], wlru_ref[...], preferred_element_type=F32)
    merged = merged + gates[:, 2 * D:3 * D].astype(F32) * jnp.dot(yf_ref[...], wf_ref[...], preferred_element_type=F32)
    out = jnp.dot(merged.astype(BF16), wout_ref[...], preferred_element_type=F32)
    x1 = x_ref[...] + gt_ref[0] * out
    x1_ref[...] = x1
    u2 = _rmsnorm_mod(x1, g_ref[...], sh_ref[0], sc_ref[0])
    u2_ref[...] = u2.astype(BF16)

    G, S = N_EXPERT_GROUPS, EXPERTS_PER_GROUP
    logits = _dot_hi(wr_ref[...], u2, ((1,), (1,)))
    score = jax.nn.sigmoid(logits)
    sel = score + rb_ref[...]
    sel_s = [sel[s * G:(s + 1) * G] for s in range(S)]
    sc_s = [score[s * G:(s + 1) * G] for s in range(S)]
    gscore = None
    for a in range(S):
        for b in range(a + 1, S):
            pair = sel_s[a] + sel_s[b]
            gscore = pair if gscore is None else jnp.maximum(gscore, pair)
    giota = lax.broadcasted_iota(I32, (G, tm), 0)
    gmax = jnp.max(gscore, axis=0, keepdims=True)
    gidx = jnp.min(jnp.where(gscore == gmax, giota, G), axis=0, keepdims=True)
    gm = giota == gidx
    cand = [jnp.sum(jnp.where(gm, sel_s[s], 0.0), axis=0, keepdims=True) for s in range(S)]
    raw = [jnp.sum(jnp.where(gm, sc_s[s], 0.0), axis=0, keepdims=True) for s in range(S)]

    def first_argmax(vals):
        m = vals[0]
        for s in range(1, S):
            m = jnp.maximum(m, vals[s])
        idx = jnp.full(m.shape, S - 1, I32)
        for s in range(S - 2, -1, -1):
            idx = jnp.where(vals[s] == m, s, idx)
        return idx

    l0 = first_argmax(cand)
    l1 = first_argmax([jnp.where(l0 == s, -jnp.inf, cand[s]) for s in range(S)])

    def pick(vals, idx):
        out = vals[S - 1]
        for s in range(S - 2, -1, -1):
            out = jnp.where(idx == s, vals[s], out)
        return out

    w0 = pick(raw, l0)
    w1 = pick(raw, l1)
    wsum = w0 + w1
    w0 = w0 / wsum
    w1 = w1 / wsum

    E = G * S
    eiota = lax.broadcasted_iota(I32, (E, tm), 0)
    oh0 = eiota == (l0 * G + gidx)
    oh1 = eiota == (l1 * G + gidx)
    oh = jnp.where(oh0 | oh1, 1.0, 0.0)
    incl = jnp.dot(oh.astype(BF16), tri_ref[...], preferred_element_type=F32)
    cnt = jnp.sum(oh, axis=1, keepdims=True)
    c8 = jnp.floor((cnt + (RUN_ALIGN - 1)) * (1.0 / RUN_ALIGN)) * RUN_ALIGN
    c8b = jnp.broadcast_to(c8, (E, LANES))
    below = (lax.broadcasted_iota(I32, (E, E), 1) < lax.broadcasted_iota(I32, (E, E), 0)).astype(BF16)
    off = jnp.dot(below, c8b.astype(BF16), preferred_element_type=F32)
    local = incl - oh + off[:, 0:1]
    s0 = jnp.sum(jnp.where(oh0, local, 0.0), axis=0, keepdims=True).astype(I32)
    s1 = jnp.sum(jnp.where(oh1, local, 0.0), axis=0, keepdims=True).astype(I32)
    tc8_ref[0] = c8b.astype(I32)
    toff_ref[0] = off.astype(I32)
    tg_ref[0] = cnt_sc[...].astype(I32)
    cnt_sc[...] = cnt_sc[...] + c8b
    cnt_ref[...] = cnt_sc[...].astype(I32)

    row = lax.broadcasted_iota(I32, (SUBLANES, tm), 0)
    ls_ref[...] = jnp.where(row == 0, s0, jnp.where(row == 1, s1, 0))
    rw_ref[...] = jnp.where(row == 0, w0, jnp.where(row == 1, w1, 0.0))


def _merge(gates, y_na, y_lru, y_f, x2, wna, wlru, wf, wout, gt, g, sh, sc, wr, rb, tri, B, T):
    N, D = x2.shape
    C = y_na.shape[1]
    tm = TM_MERGE
    tpb = T // tm
    E = wr.shape[0]
    tok = lambda i: (i, 0)
    scan = lambda i: (i % tpb, i // tpb)
    per_b = lambda i: (i // tpb, 0, 0)
    const2 = lambda i: (0, 0)
    return pl.pallas_call(
        _merge_kernel,
        grid=(N // tm,),
        in_specs=[pl.BlockSpec((tm, 3 * D), tok),
                  pl.BlockSpec((tm, C), tok),
                  pl.BlockSpec((tm, C), scan),
                  pl.BlockSpec((tm, C), tok),
                  pl.BlockSpec((tm, D), tok),
                  pl.BlockSpec((C, D), const2), pl.BlockSpec((C, D), const2), pl.BlockSpec((C, D), const2),
                  pl.BlockSpec((D, D), const2),
                  pl.BlockSpec((1, 1, D), per_b),
                  pl.BlockSpec((1, D), const2),
                  pl.BlockSpec((1, 1, D), per_b),
                  pl.BlockSpec((1, 1, D), per_b),
                  pl.BlockSpec((E, D), const2),
                  pl.BlockSpec((E, 1), const2),
                  pl.BlockSpec((tm, tm), const2)],
        out_specs=[pl.BlockSpec((tm, D), tok),
                   pl.BlockSpec((tm, D), tok),
                   pl.BlockSpec((SUBLANES, tm), lambda i: (0, i)),
                   pl.BlockSpec((SUBLANES, tm), lambda i: (0, i)),
                   pl.BlockSpec((1, E, LANES), lambda i: (i, 0, 0)),
                   pl.BlockSpec((1, E, LANES), lambda i: (i, 0, 0)),
                   pl.BlockSpec((1, E, LANES), lambda i: (i, 0, 0)),
                   pl.BlockSpec((E, LANES), const2)],
        out_shape=[jax.ShapeDtypeStruct((N, D), F32),
                   jax.ShapeDtypeStruct((N, D), BF16),
                   jax.ShapeDtypeStruct((SUBLANES, N), I32),
                   jax.ShapeDtypeStruct((SUBLANES, N), F32),
                   jax.ShapeDtypeStruct((N // tm, E, LANES), I32),
                   jax.ShapeDtypeStruct((N // tm, E, LANES), I32),
                   jax.ShapeDtypeStruct((N // tm, E, LANES), I32),
                   jax.ShapeDtypeStruct((E, LANES), I32)],
        scratch_shapes=[pltpu.VMEM((E, LANES), F32)],
        compiler_params=_cparams(("arbitrary",), 48),
        name="merge_route",
    )(gates, y_na, y_lru, y_f, x2, wna, wlru, wf, wout, gt, g, sh, sc, wr, rb, tri)


def _for_run_pieces(n_rows, fn):
    def chunk(j, c):
        fn(j * RUN_CHUNK, RUN_CHUNK)
        return c

    lax.fori_loop(0, n_rows // RUN_CHUNK, chunk, 0)
    size = RUN_CHUNK // 2
    while size >= RUN_ALIGN:
        pl.when((n_rows & size) != 0)(functools.partial(fn, (n_rows // (2 * size)) * (2 * size), size))
        size //= 2


def _run_copies(tc8_ref, toff_ref, tg_ref, ps_ref, tile, hbm, stage, sem, to_hbm, wait):
    for e in range(N_EXPERTS):
        idx = tile * N_EXPERTS + e
        loc0 = toff_ref[idx]
        hbm0 = ps_ref[e] + tg_ref[idx]

        def piece(off, size, loc0=loc0, hbm0=hbm0):
            loc = stage.at[pl.ds(pl.multiple_of(loc0 + off, RUN_ALIGN), size)]
            glob = hbm.at[pl.ds(pl.multiple_of(hbm0 + off, RUN_ALIGN), size)]
            cp = pltpu.make_async_copy(loc, glob, sem) if to_hbm else pltpu.make_async_copy(glob, loc, sem)
            if wait:
                cp.wait()
            else:
                cp.start()

        _for_run_pieces(tc8_ref[idx], piece)


def _dispatch_kernel(tc8_ref, toff_ref, tg_ref, ps_ref, pad_ref, nu_ref, ls_ref, u_ref, xs_out, stage, zbuf, sem,
                     *, n_tail):
    i = pl.program_id(0)
    tm = u_ref.shape[0]
    nb = xs_out.shape[0] // MOE_ROWS

    @pl.when(i == 0)
    def _():
        zbuf[...] = jnp.zeros_like(zbuf)

        def zero_copy(blk):
            return pltpu.make_async_copy(zbuf, xs_out.at[pl.ds(pl.multiple_of(blk * MOE_ROWS, MOE_ROWS), MOE_ROWS)],
                                         sem.at[0])

        def guarded(fn):
            for e in range(N_EXPERTS):
                pl.when(pad_ref[e] >= 0)(lambda e=e: fn(zero_copy(pad_ref[e])))
            for k in range(n_tail):
                pl.when(nb - 1 - k >= nu_ref[0])(lambda k=k: fn(zero_copy(nb - 1 - k)))

        guarded(lambda cp: cp.start())
        guarded(lambda cp: cp.wait())

    slots = ls_ref[...]
    riota = lax.broadcasted_iota(I32, (stage.shape[1], tm), 0)
    hit = (riota == slots[0:1]) | (riota == slots[1:2])
    perm = jnp.where(hit, 1.0, 0.0).astype(BF16)
    cur = i % 2
    stage.at[cur][...] = jnp.dot(perm, u_ref[...], preferred_element_type=F32).astype(BF16)

    def runs(tile, buf, wait):
        _run_copies(tc8_ref, toff_ref, tg_ref, ps_ref, tile, xs_out, stage.at[buf], sem.at[buf], to_hbm=True, wait=wait)

    runs(i, cur, wait=False)
    pl.when(i > 0)(lambda: runs(i - 1, 1 - cur, wait=True))
    pl.when(i == pl.num_programs(0) - 1)(lambda: runs(i, cur, wait=True))


def _dispatch(tables, pstart, pad_blk, n_used, ls, u2, P, n_tail):
    N, D = u2.shape
    tm = TM_MOE
    n_pref = len(tables) + 3
    grid_spec = pltpu.PrefetchScalarGridSpec(
        num_scalar_prefetch=n_pref,
        grid=(N // tm,),
        in_specs=[pl.BlockSpec((SUBLANES, tm), lambda i, *_: (0, i)),
                  pl.BlockSpec((tm, D), lambda i, *_: (i, 0))],
        out_specs=pl.BlockSpec(memory_space=pl.ANY),
        scratch_shapes=[pltpu.VMEM((2, MOE_STAGE_ROWS, D), BF16), pltpu.VMEM((MOE_ROWS, D), BF16),
                        pltpu.SemaphoreType.DMA((2,))],
    )
    return pl.pallas_call(
        functools.partial(_dispatch_kernel, n_tail=n_tail),
        grid_spec=grid_spec,
        out_shape=jax.ShapeDtypeStruct((P, D), BF16),
        compiler_params=_cparams(("arbitrary",), 40),
        name="moe_dispatch",
    )(*tables, pstart, pad_blk, n_used, ls, u2)


def _expert_kernel(be_ref, nu_ref, xs_ref, wg_ref, wu_ref, wd_ref, ys_ref, wg_sc, wu_sc, wd_sc):
    i = pl.program_id(0)
    prev = be_ref[jnp.maximum(i - 1, 0)]
    fresh = (i == 0) | (be_ref[i] != prev)

    @pl.when(fresh)
    def _():
        wg_sc[...] = wg_ref[0, 0].astype(BF16)
        wu_sc[...] = wu_ref[0, 0].astype(BF16)
        wd_sc[...] = wd_ref[0, 0].astype(BF16)

    @pl.when(i < nu_ref[0])
    def _():
        xb = xs_ref[...]
        hg = jnp.dot(xb, wg_sc[...], preferred_element_type=F32)
        hu = jnp.dot(xb, wu_sc[...], preferred_element_type=F32)
        h = (hg * _sigmoid(hg)) * hu
        ys_ref[...] = jnp.dot(h.astype(BF16), wd_sc[...], preferred_element_type=F32).astype(BF16)

    @pl.when(i >= nu_ref[0])
    def _():
        ys_ref[...] = jnp.zeros_like(ys_ref)


def _experts(block_e, n_used, xs, w_gate, w_up, w_down, layer):
    P, D = xs.shape
    _, E, _, DE = w_gate.shape
    nb = P // MOE_ROWS
    wmap = lambda i, be, nu: (layer, be[i], 0, 0)
    grid_spec = pltpu.PrefetchScalarGridSpec(
        num_scalar_prefetch=2,
        grid=(nb,),
        in_specs=[pl.BlockSpec((MOE_ROWS, D), lambda i, be, nu: (i, 0)),
                  pl.BlockSpec((1, 1, D, DE), wmap),
                  pl.BlockSpec((1, 1, D, DE), wmap),
                  pl.BlockSpec((1, 1, DE, D), wmap)],
        out_specs=pl.BlockSpec((MOE_ROWS, D), lambda i, be, nu: (i, 0)),
        scratch_shapes=[pltpu.VMEM((D, DE), BF16), pltpu.VMEM((D, DE), BF16), pltpu.VMEM((DE, D), BF16)],
    )
    return pl.pallas_call(
        _expert_kernel,
        grid_spec=grid_spec,
        out_shape=jax.ShapeDtypeStruct((P, D), BF16),
        compiler_params=_cparams(("arbitrary",), 48),
        name="moe_experts",
    )(block_e, n_used, xs, w_gate, w_up, w_down)


def _combine_kernel(tc8_ref, toff_ref, tg_ref, ps_ref, ls_ref, rw_ref, x_ref, gt_ref, fg_ref, ys_ref, o_ref,
                    stage, sem, *, final):
    i = pl.program_id(0)
    tm = x_ref.shape[0]

    def runs(tile, buf, wait):
        _run_copies(tc8_ref, toff_ref, tg_ref, ps_ref, tile, ys_ref, stage.at[buf], sem.at[buf], to_hbm=False, wait=wait)

    @pl.when(i == 0)
    def _():
        stage[...] = jnp.zeros_like(stage)
        runs(0, 0, wait=False)

    cur = i % 2
    pl.when(i + 1 < pl.num_programs(0))(lambda: runs(i + 1, 1 - cur, wait=False))
    runs(i, cur, wait=True)
    rows = stage[cur]
    liota = lax.broadcasted_iota(I32, (tm, stage.shape[1]), 1)
    slots = ls_ref[...]
    w = rw_ref[...]
    y = None
    for k in range(TOP_K):
        pick = jnp.where(liota == slots[:, k:k + 1], 1.0, 0.0).astype(BF16)
        yk = jnp.dot(pick, rows, preferred_element_type=F32) * w[:, k:k + 1]
        y = yk if y is None else y + yk
    x2 = x_ref[...] + gt_ref[0] * y
    if final:
        x2 = (x2 * lax.rsqrt(jnp.mean(x2 * x2, axis=-1, keepdims=True) + EPS)) * fg_ref[...]
    o_ref[...] = x2


def _combine(tables, pstart, ls_t, rw_t, x1, gt, final_g, ys, T, final):
    N, D = x1.shape
    tm = TM_MOE
    tpb = T // tm
    grid_spec = pltpu.PrefetchScalarGridSpec(
        num_scalar_prefetch=len(tables) + 1,
        grid=(N // tm,),
        in_specs=[pl.BlockSpec((tm, SUBLANES), lambda i, *_: (i, 0)),
                  pl.BlockSpec((tm, SUBLANES), lambda i, *_: (i, 0)),
                  pl.BlockSpec((tm, D), lambda i, *_: (i, 0)),
                  pl.BlockSpec((1, 1, D), lambda i, *_: (i // tpb, 0, 0)),
                  pl.BlockSpec((1, D), lambda i, *_: (0, 0)),
                  pl.BlockSpec(memory_space=pl.ANY)],
        out_specs=pl.BlockSpec((tm, D), lambda i, *_: (i, 0)),
        scratch_shapes=[pltpu.VMEM((2, MOE_STAGE_ROWS, D), BF16), pltpu.SemaphoreType.DMA((2,))],
    )
    return pl.pallas_call(
        functools.partial(_combine_kernel, final=final),
        grid_spec=grid_spec,
        out_shape=jax.ShapeDtypeStruct((N, D), F32),
        compiler_params=_cparams(("arbitrary",), 40),
        name="moe_combine",
    )(*tables, pstart, ls_t, rw_t, x1, gt, final_g, ys)


def _moe_plan(cnt, nb):
    G, S = N_EXPERT_GROUPS, EXPERTS_PER_GROUP
    counts = cnt[:, 0]
    padded = (counts + MOE_ROWS - 1) // MOE_ROWS * MOE_ROWS
    pend = jnp.cumsum(padded).astype(I32)
    pstart = pend - padded
    pad_blk = jnp.where(padded > 0, pend // MOE_ROWS - 1, -1).astype(I32)
    block_start = jnp.arange(nb, dtype=I32) * MOE_ROWS
    block_row = jnp.minimum(jnp.sum((pend[None, :] <= block_start[:, None]).astype(I32), axis=1), G * S - 1)
    block_e = (block_row % G) * S + block_row // G
    n_used = pend[-1:] // MOE_ROWS
    return pstart, pad_blk, block_e.astype(I32), n_used.astype(I32)


def _block_diag(w):
    nb, bw, _ = w.shape
    eye = jnp.eye(nb, dtype=w.dtype)
    return jnp.einsum('nij,nm->nimj', w, eye).reshape(nb * bw, nb * bw)


def kernel(x, c, ada_w, ada_b, norm_mix_g, norm_ffn_g, w_in, w_branch_gate, b_branch_gate,
           na_rpb, lru_conv_w, lru_conv_b, lru_w_r, lru_b_r, lru_w_i, lru_b_i, lru_lambda,
           w_proj_na, w_proj_lru, w_proj_fnet, w_out, w_router, router_bias,
           w_exp_gate, w_exp_up, w_exp_down, final_g):
    B, T, D = x.shape
    L = ada_w.shape[0]
    N = B * T
    E = N_EXPERTS
    G, S = N_EXPERT_GROUPS, EXPERTS_PER_GROUP
    A = N * TOP_K
    assert TM_MERGE == TM_MOE
    max_rows = A + (N // TM_MOE) * E * (RUN_ALIGN - 1) + E * (MOE_ROWS - 1)
    nb = -(-max_rows // MOE_ROWS)
    P = nb * MOE_ROWS

    mod = _ada_mod(c, ada_w, ada_b)
    m1_np, cs_np = _fnet_tables(T)
    m1 = jnp.asarray(m1_np).astype(BF16)
    cs = jnp.asarray(cs_np).astype(BF16)
    tri = (lax.broadcasted_iota(I32, (TM_MERGE, TM_MERGE), 0)
           <= lax.broadcasted_iota(I32, (TM_MERGE, TM_MERGE), 1)).astype(BF16)
    wr = w_router.T.reshape(G, S, D).transpose(1, 0, 2).reshape(E, D)
    rb = router_bias.reshape(G, S).T.reshape(E, 1)
    fg = final_g.reshape(1, D)

    x2 = x.reshape(N, D)
    for l in range(L):
        sh_mix, sc_mix, gt_mix, sh_ffn, sc_ffn, gt_ffn = [m.reshape(B, 1, D) for m in jnp.split(mod[l], 6, axis=-1)]
        wcat = jnp.concatenate([w_in[l], w_branch_gate[l]], axis=1).astype(BF16)
        q, k, v, xr, gy, fr, gates = _mix_in(x2, norm_mix_g[l].reshape(1, D), sh_mix, sc_mix, wcat,
                                             b_branch_gate[l].reshape(1, -1), B, T)
        y_na = _na(q, k, v, _na_bias_table(na_rpb[l]), B, T)

        wg = jnp.stack([jnp.concatenate([_block_diag(lru_w_r[l, d]), _block_diag(lru_w_i[l, d])], axis=1)
                        for d in range(2)]).astype(BF16)
        bgl = jnp.stack([jnp.concatenate([lru_b_r[l, d], lru_b_i[l, d]]) for d in range(2)])[:, None, :]
        af, bf, ab, bb = _lru_gates(xr, lru_conv_w[l], lru_conv_b[l].reshape(1, -1), wg, bgl,
                                    lru_lambda[l][:, None, :], B, T)
        y_lru = _lru_scans(af, bf, ab, bb, gy)
        y_f = _fnet(fr, m1, cs, B, T)

        x1, u2, ls, rw, tc8, toff, tg, cnt = _merge(
            gates, y_na, y_lru, y_f, x2,
            w_proj_na[l].astype(BF16), w_proj_lru[l].astype(BF16),
            w_proj_fnet[l].astype(BF16), w_out[l].astype(BF16),
            gt_mix, norm_ffn_g[l].reshape(1, D), sh_ffn, sc_ffn, wr, rb, tri, B, T)
        tables = [t[:, :, 0].reshape(-1) for t in (tc8, toff, tg)]
        pstart, pad_blk, block_e, n_used = _moe_plan(cnt, nb)
        xs = _dispatch(tables, pstart, pad_blk, n_used, ls, u2, P, nb - A // MOE_ROWS)
        ys = _experts(block_e, n_used, xs, w_exp_gate, w_exp_up, w_exp_down, l)
        x2 = _combine(tables, pstart, ls.T, rw.T, x1, gt_ffn, fg, ys, T, final=(l == L - 1))
    return x2.reshape(B, T, D)
```

```python
import functools

import numpy as np
import jax
import jax.numpy as jnp
from jax import lax
from jax.experimental import pallas as pl
from jax.experimental.pallas import tpu as pltpu

F32 = jnp.float32
BF16 = jnp.bfloat16
I32 = jnp.int32

GRID_W = 64
NA_HEAD_DIM = 64
NA_WIN_ROWS = 8
NA_WIN_COLS = 16
NA_HEADS_PER_TILE = 4
LRU_BLOCKS = 8
LRU_C = 8.0
CONV_W = 4
FNET_GROUP_W = 128
N_EXPERTS = 32
N_EXPERT_GROUPS = 8
EXPERTS_PER_GROUP = 4
TOP_K = 2
EPS = 1e-6
MASK_VALUE = -1e30

V7X_VMEM_BYTES = 64 * 1024 * 1024
LANES = 128
SUBLANES = 8

TM_MIX = 1024
TN_MIX = 1024
MIX_ROW_CHUNK = 512
NA_ROWS_PER_STEP = 8
TM_LRU = 512
T_SCAN = 512
TM_MERGE = 512
TM_MOE = 512
MOE_ROWS = 512
RUN_CHUNK = 64
RUN_ALIGN = 16
MOE_STAGE_ROWS = -(-(TM_MOE * TOP_K + N_EXPERTS * (RUN_ALIGN - 1)) // 256) * 256


def _cparams(semantics, vmem_mb):
    assert vmem_mb * 1024 * 1024 < V7X_VMEM_BYTES
    return pltpu.CompilerParams(dimension_semantics=semantics, vmem_limit_bytes=vmem_mb * 1024 * 1024)


def _split2(a):
    hi = a.astype(BF16)
    lo = (a - hi.astype(F32)).astype(BF16)
    return hi, lo


def _dot_hi(a, b, dims):
    a_hi, a_lo = _split2(a)
    b_hi, b_lo = _split2(b)
    dn = (dims, ((), ()))
    m = a.shape[0]
    both = lax.dot_general(jnp.concatenate([a_hi, a_lo], axis=0), b_hi, dn, preferred_element_type=F32)
    return (both[:m] + both[m:]) + lax.dot_general(a_hi, b_lo, dn, preferred_element_type=F32)


def _sigmoid(x):
    return 0.5 * jnp.tanh(0.5 * x) + 0.5


def _rmsnorm_mod(x, g, sh, sc):
    y = x * lax.rsqrt(jnp.mean(x * x, axis=-1, keepdims=True) + EPS)
    return (y * g) * (1.0 + sc) + sh


def _ada_kernel(c_ref, w_ref, b_ref, o_ref):
    c = c_ref[...]
    ca = c * jax.nn.sigmoid(c)
    o_ref[0] = _dot_hi(ca, w_ref[0], ((1,), (0,))) + b_ref[0]


def _ada_mod(c, ada_w, ada_b):
    L, D, D6 = ada_w.shape
    B = c.shape[0]
    bp = -(-B // SUBLANES) * SUBLANES
    cp = jnp.pad(c, ((0, bp - B), (0, 0)))
    tn = D6 // 4
    out = pl.pallas_call(
        _ada_kernel,
        grid=(L, D6 // tn),
        in_specs=[pl.BlockSpec((bp, D), lambda l, j: (0, 0)),
                  pl.BlockSpec((1, D, tn), lambda l, j: (l, 0, j)),
                  pl.BlockSpec((1, 1, tn), lambda l, j: (l, 0, j))],
        out_specs=pl.BlockSpec((1, bp, tn), lambda l, j: (l, 0, j)),
        out_shape=jax.ShapeDtypeStruct((L, bp, D6), F32),
        compiler_params=_cparams(("parallel", "parallel"), 40),
        name="ada_mod",
    )(cp, ada_w, ada_b.reshape(L, 1, D6))
    return out[:, :B]


def _mixin_kernel(x_ref, g_ref, sh_ref, sc_ref, w_ref, bg_ref,
                  q_ref, k_ref, v_ref, xr_ref, gy_ref, fr_ref, gate_ref, u_sc):
    j = pl.program_id(1)

    @pl.when(j == 0)
    def _():
        u_sc[...] = _rmsnorm_mod(x_ref[...], g_ref[...], sh_ref[0], sc_ref[0]).astype(BF16)

    tm = u_sc.shape[0]
    W = q_ref.shape[1]
    lo, hi = slice(0, W), slice(W, 2 * W)

    def per_chunk(store):
        for c in range(tm // MIX_ROW_CHUNK):
            rows = slice(c * MIX_ROW_CHUNK, (c + 1) * MIX_ROW_CHUNK)
            store(rows, jnp.dot(u_sc[rows, :], w_ref[...], preferred_element_type=F32))

    @pl.when(j == 0)
    def _():
        def store(rows, acc):
            q_ref[rows, :] = acc[:, lo].astype(BF16)
            k_ref[rows, :] = acc[:, hi].astype(BF16)
        per_chunk(store)

    @pl.when(j == 1)
    def _():
        def store(rows, acc):
            v_ref[rows, :] = acc[:, lo].astype(BF16)
            xr_ref[rows, :] = acc[:, hi]
        per_chunk(store)

    @pl.when(j == 2)
    def _():
        def store(rows, acc):
            gy_ref[rows, :] = jax.nn.gelu(acc[:, lo]).astype(BF16)
            fr_ref[rows, :] = acc[:, hi].astype(BF16)
        per_chunk(store)

    @pl.when(j >= 3)
    def _():
        def store(rows, acc):
            gate_ref[rows, :] = _sigmoid(acc + bg_ref[...]).astype(BF16)
        per_chunk(store)


def _mix_in(x2, g, sh, sc, wcat, bgate, B, T):
    N, D = x2.shape
    tm, tn = TM_MIX, TN_MIX
    tpb = T // tm
    W = tn // 2
    n_proj = 3
    n_gate = bgate.shape[1] // tn
    tok = lambda i, j: (i, 0)
    scan = lambda i, j: (i % tpb, i // tpb)
    per_b = lambda i, j: (i // tpb, 0, 0)
    out_shape = [jax.ShapeDtypeStruct((N, W), BF16)] * 3 + [
        jax.ShapeDtypeStruct((N, W), F32),
        jax.ShapeDtypeStruct((T, B * W), BF16),
        jax.ShapeDtypeStruct((N, W), BF16),
        jax.ShapeDtypeStruct((N, n_gate * tn), BF16)]
    out_specs = [pl.BlockSpec((tm, W), tok)] * 3 + [
        pl.BlockSpec((tm, W), tok),
        pl.BlockSpec((tm, W), scan),
        pl.BlockSpec((tm, W), tok),
        pl.BlockSpec((tm, tn), lambda i, j: (i, jnp.maximum(j - n_proj, 0)))]
    return pl.pallas_call(
        _mixin_kernel,
        grid=(N // tm, n_proj + n_gate),
        in_specs=[pl.BlockSpec((tm, D), tok),
                  pl.BlockSpec((1, D), lambda i, j: (0, 0)),
                  pl.BlockSpec((1, 1, D), per_b),
                  pl.BlockSpec((1, 1, D), per_b),
                  pl.BlockSpec((D, tn), lambda i, j: (0, j)),
                  pl.BlockSpec((1, tn), lambda i, j: (0, jnp.maximum(j - n_proj, 0)))],
        out_specs=out_specs,
        out_shape=out_shape,
        scratch_shapes=[pltpu.VMEM((tm, D), BF16)],
        compiler_params=_cparams(("parallel", "arbitrary"), 48),
        name="mix_in",
    )(x2, g, sh, sc, wcat, bgate)


def _na_kernel(q_ref, kp_ref, kc_ref, kn_ref, vp_ref, vc_ref, vn_ref, bias_ref, o_ref, kb, vb):
    j = pl.program_id(1)
    nj = pl.num_programs(1)
    R = NA_ROWS_PER_STEP
    half = R // 2
    W = GRID_W
    hw = half * W
    tq = R * W
    win = NA_WIN_ROWS * W
    HT = NA_HEADS_PER_TILE
    lt = HT * NA_HEAD_DIM
    kb[0:hw] = kp_ref[...]
    kb[hw:hw + tq] = kc_ref[...]
    kb[hw + tq:2 * hw + tq] = kn_ref[...]
    vb[0:hw] = vp_ref[...]
    vb[hw:hw + tq] = vc_ref[...]
    vb[hw + tq:2 * hw + tq] = vn_ref[...]
    head_of_lane = lax.broadcasted_iota(I32, (W, lt), 1) // NA_HEAD_DIM
    n_tiles = q_ref.shape[1] // lt
    scale = jnp.asarray(NA_HEAD_DIM ** -0.5, BF16)
    for dr in range(R):
        lo = jnp.where(j == 0, max(dr, half), jnp.where(j == nj - 1, min(dr, half), dr))
        var = lo - dr + (half - 1)
        start = pl.multiple_of(lo * W, W)
        for ht in range(n_tiles):
            cs = slice(ht * lt, (ht + 1) * lt)
            qrow = q_ref[dr * W:(dr + 1) * W, cs] * scale
            qbd = jnp.concatenate(
                [jnp.where(head_of_lane == h, qrow, jnp.zeros_like(qrow)) for h in range(HT)], axis=0)
            kwin = kb[pl.ds(start, win), cs]
            s = lax.dot_general(qbd, kwin, (((1,), (1,)), ((), ())), preferred_element_type=F32)
            s = s + bias_ref[var, ht]
            m = jnp.max(s, axis=-1, keepdims=True)
            p = jnp.exp(s - m)
            l = jnp.sum(p, axis=-1, keepdims=True)
            vwin = vb[pl.ds(start, win), cs]
            o = jnp.dot(p.astype(BF16), vwin, preferred_element_type=F32) / l
            orow = jnp.where(head_of_lane == 0, o[0:W], 0.0)
            for h in range(1, HT):
                orow = orow + jnp.where(head_of_lane == h, o[h * W:(h + 1) * W], 0.0)
            o_ref[dr * W:(dr + 1) * W, cs] = orow.astype(BF16)


def _na_bias_table(rpb):
    H = rpb.shape[0]
    W, KR, KC = GRID_W, NA_WIN_ROWS, NA_WIN_COLS
    w = np.arange(W)
    cstart = np.clip(w - KC // 2, 0, W - KC)
    wk = np.arange(W)
    col_ok = (wk[None, :] >= cstart[:, None]) & (wk[None, :] < cstart[:, None] + KC)
    cb = wk[None, :] - w[:, None] + (KC - 1)
    onehot = ((cb[None] == np.arange(2 * KC - 1)[:, None, None]) & col_ok[None]).astype(np.float32)
    toep = jnp.einsum('hac,cwx->hawx', rpb.astype(F32), jnp.asarray(onehot), precision=lax.Precision.HIGHEST)
    toep = jnp.where(jnp.asarray(col_ok)[None, None], toep, MASK_VALUE)
    tab = jnp.stack([toep[:, v:v + KR] for v in range(KR)])
    tab = tab.transpose(0, 1, 3, 2, 4)
    HT = NA_HEADS_PER_TILE
    return tab.reshape(KR, H // HT, HT * W, KR * W)


def _na(q, k, v, bias, B, T):
    N, C = q.shape
    R = NA_ROWS_PER_STEP
    tq = R * GRID_W
    hw = tq // 2
    nj = T // tq
    nh = T // hw
    qmap = lambda b, j: (b * nj + j, 0)
    pmap = lambda b, j: (b * nh + jnp.maximum(2 * j - 1, 0), 0)
    nmap = lambda b, j: (b * nh + jnp.minimum(2 * j + 2, nh - 1), 0)
    return pl.pallas_call(
        _na_kernel,
        grid=(B, nj),
        in_specs=[pl.BlockSpec((tq, C), qmap),
                  pl.BlockSpec((hw, C), pmap), pl.BlockSpec((tq, C), qmap), pl.BlockSpec((hw, C), nmap),
                  pl.BlockSpec((hw, C), pmap), pl.BlockSpec((tq, C), qmap), pl.BlockSpec((hw, C), nmap),
                  pl.BlockSpec(bias.shape, lambda b, j: (0, 0, 0, 0))],
        out_specs=pl.BlockSpec((tq, C), qmap),
        out_shape=jax.ShapeDtypeStruct((N, C), BF16),
        scratch_shapes=[pltpu.VMEM((2 * tq, C), BF16), pltpu.VMEM((2 * tq, C), BF16)],
        compiler_params=_cparams(("parallel", "parallel"), 48),
        name="na_attention",
    )(q, k, k, k, v, v, v, bias)


def _lru_gate_kernel(x_ref, xp_ref, xn_ref, cw_ref, cb_ref, wg_ref, bg_ref, lam_ref,
                     af_ref, bf_ref, ab_ref, bb_ref, xpad, *, tpb):
    i = pl.program_id(0)
    tm, C = x_ref.shape
    H = SUBLANES
    first = (i % tpb) == 0
    last = (i % tpb) == tpb - 1
    xpad[0:H] = jnp.where(first, 0.0, xp_ref[...])
    xpad[H:H + tm] = x_ref[...]
    xpad[H + tm:2 * H + tm] = jnp.where(last, 0.0, xn_ref[...])
    left = CONV_W // 2
    xc = xpad[H - left:H - left + tm] * cw_ref[0:1]
    for t in range(1, CONV_W):
        xc = xc + xpad[H - left + t:H - left + t + tm] * cw_ref[t:t + 1]
    xc = xc + cb_ref[...]
    xcb = xc.astype(BF16)
    hx = 0.5 * xc
    for d, (a_ref, b_ref) in enumerate(((af_ref, bf_ref), (ab_ref, bb_ref))):
        t = jnp.tanh(jnp.dot(xcb, wg_ref[d], preferred_element_type=F32) + bg_ref[d])
        c = (-0.5 * LRU_C) * jax.nn.softplus(-lam_ref[d])
        log_a = c * t[:, :C] + c
        a = jnp.exp(log_a)
        a_ref[...] = a
        b_ref[...] = jnp.sqrt(1.0 - a * a) * (t[:, C:] * hx + hx)


def _lru_gates(xr, conv_w, conv_b, wg, bg, lam, B, T):
    N, C = xr.shape
    tm = TM_LRU
    tpb = T // tm
    nb8 = N // SUBLANES
    r8 = tm // SUBLANES
    scan = lambda i: (i % tpb, i // tpb)
    const2 = lambda i: (0, 0)
    const3 = lambda i: (0, 0, 0)
    outs = pl.pallas_call(
        functools.partial(_lru_gate_kernel, tpb=tpb),
        grid=(N // tm,),
        in_specs=[pl.BlockSpec((tm, C), lambda i: (i, 0)),
                  pl.BlockSpec((SUBLANES, C), lambda i: (jnp.maximum(i * r8 - 1, 0), 0)),
                  pl.BlockSpec((SUBLANES, C), lambda i: (jnp.minimum((i + 1) * r8, nb8 - 1), 0)),
                  pl.BlockSpec((CONV_W, C), const2),
                  pl.BlockSpec((1, C), const2),
                  pl.BlockSpec((2, C, 2 * C), const3),
                  pl.BlockSpec((2, 1, 2 * C), const3),
                  pl.BlockSpec((2, 1, C), const3)],
        out_specs=[pl.BlockSpec((tm, C), scan)] * 4,
        out_shape=[jax.ShapeDtypeStruct((T, B * C), F32)] * 4,
        scratch_shapes=[pltpu.VMEM((tm + 2 * SUBLANES, C), F32)],
        compiler_params=_cparams(("parallel",), 48),
        name="lru_gates",
    )(xr, xr, xr, conv_w, conv_b, wg, bg, lam)
    return outs


def _scan_bwd_kernel(a_ref, b_ref, h_ref, carry):
    @pl.when(pl.program_id(0) == 0)
    def _():
        carry[...] = jnp.zeros_like(carry)

    tc = a_ref.shape[0]

    def body(s, h):
        t = tc - 1 - s
        h = a_ref[t] * h + b_ref[t]
        h_ref[t] = h
        return h

    carry[...] = lax.fori_loop(0, tc, body, carry[...], unroll=8)


def _scan_fwd_kernel(a_ref, b_ref, hb_ref, gy_ref, y_ref, carry):
    @pl.when(pl.program_id(0) == 0)
    def _():
        carry[...] = jnp.zeros_like(carry)

    tc = a_ref.shape[0]

    def body(t, h):
        h = a_ref[t] * h + b_ref[t]
        y_ref[t] = ((h + hb_ref[t]) * gy_ref[t].astype(F32)).astype(BF16)
        return h

    carry[...] = lax.fori_loop(0, tc, body, carry[...], unroll=8)


def _lru_scans(af, bf, ab, bb, gy):
    T, BC = af.shape
    S = BC // LANES
    tc = T_SCAN
    nc = T // tc
    v3 = lambda z: z.reshape(T, S, LANES)
    blk = (tc, S, LANES)
    fwd = lambda c: (c, 0, 0)
    rev = lambda c: (nc - 1 - c, 0, 0)
    hb = pl.pallas_call(
        _scan_bwd_kernel,
        grid=(nc,),
        in_specs=[pl.BlockSpec(blk, rev), pl.BlockSpec(blk, rev)],
        out_specs=pl.BlockSpec(blk, rev),
        out_shape=jax.ShapeDtypeStruct((T, S, LANES), F32),
        scratch_shapes=[pltpu.VMEM((S, LANES), F32)],
        compiler_params=_cparams(("arbitrary",), 48),
        name="lru_scan_bwd",
    )(v3(ab), v3(bb))
    y = pl.pallas_call(
        _scan_fwd_kernel,
        grid=(nc,),
        in_specs=[pl.BlockSpec(blk, fwd)] * 4,
        out_specs=pl.BlockSpec(blk, fwd),
        out_shape=jax.ShapeDtypeStruct((T, S, LANES), BF16),
        scratch_shapes=[pltpu.VMEM((S, LANES), F32)],
        compiler_params=_cparams(("arbitrary",), 48),
        name="lru_scan_fwd",
    )(v3(af), v3(bf), hb, v3(gy))
    return y.reshape(T, BC)


def _fnet_tables(T):
    T2 = FNET_GROUP_W
    T1 = T // T2
    k1 = np.arange(T1)[:, None]
    t1 = np.arange(T1)[None, :]
    stage1 = np.zeros((T2, 2 * T1, 2 * T1), np.float32)
    for t2 in range(T2):
        ph = 2.0 * np.pi * ((k1 * (T2 * t1 + t2)) % T) / T
        c, s = np.cos(ph), np.sin(ph)
        stage1[t2] = np.block([[c, s], [-s, c]])
    n = np.arange(T2)
    ph2 = 2.0 * np.pi * ((n[:, None] * n[None, :]) % T2) / T2
    c2, s2 = np.cos(ph2).astype(np.float32), np.sin(ph2).astype(np.float32)
    return stage1, np.concatenate([c2, s2], axis=1)


def _fnet_kernel(x_ref, m1_ref, cs_ref, o_ref, zr, zi, yo, *, T1):
    T2 = FNET_GROUP_W
    T = T1 * T2
    cs = cs_ref[...]
    for k1 in range(T1):
        blk = slice(k1 * T2, (k1 + 1) * T2)
        z = jnp.dot(x_ref[blk, :], cs, preferred_element_type=F32)
        zr[blk, :] = z[:, :T2]
        zi[blk, :] = -z[:, T2:]
    for t2 in range(T2):
        rows = pl.ds(t2, T1, stride=T2)
        zin = jnp.concatenate([zr[rows, :], zi[rows, :]], axis=0).astype(BF16)
        a = jnp.dot(m1_ref[t2], zin, preferred_element_type=F32)
        zr[rows, :] = a[:T1]
        zi[rows, :] = a[T1:]
    inv = 1.0 / np.sqrt(float(T) * T2)
    for k1 in range(T1):
        blk = slice(k1 * T2, (k1 + 1) * T2)
        ain = jnp.concatenate([zr[blk, :], zi[blk, :]], axis=0).astype(BF16)
        y = jnp.dot(cs, ain, preferred_element_type=F32)
        yo[pl.ds(k1, T2, stride=T1), :] = y * inv
    o_ref[...] = yo[...].astype(BF16)


def _fnet(fr, m1, cs, B, T):
    N, C = fr.shape
    G = C // FNET_GROUP_W
    T1 = T // FNET_GROUP_W
    blk = pl.BlockSpec((T, FNET_GROUP_W), lambda b, g: (b, g))
    return pl.pallas_call(
        functools.partial(_fnet_kernel, T1=T1),
        grid=(B, G),
        in_specs=[blk,
                  pl.BlockSpec(m1.shape, lambda b, g: (0, 0, 0)),
                  pl.BlockSpec(cs.shape, lambda b, g: (0, 0))],
        out_specs=blk,
        out_shape=jax.ShapeDtypeStruct((N, C), BF16),
        scratch_shapes=[pltpu.VMEM((T, FNET_GROUP_W), F32)] * 3,
        compiler_params=_cparams(("parallel", "parallel"), 48),
        name="fnet",
    )(fr, m1, cs)


def _merge_kernel(gate_ref, yna_ref, ylru_ref, yf_ref, x_ref, wna_ref, wlru_ref, wf_ref, wout_ref,
                  gt_ref, g_ref, sh_ref, sc_ref, wr_ref, rb_ref, tri_ref,
                  x1_ref, u2_ref, ls_ref, rw_ref, tc8_ref, toff_ref, tg_ref, cnt_ref, cnt_sc):
    i = pl.program_id(0)

    @pl.when(i == 0)
    def _():
        cnt_sc[...] = jnp.zeros_like(cnt_sc)

    tm, D = x_ref.shape
    gates = gate_ref[...]
    merged = gates[:, 0:D].astype(F32) * jnp.dot(yna_ref[...], wna_ref[...], preferred_element_type=F32)
    merged = merged + gates[:, D:2 * D].astype(F32) * jnp.dot(ylru_ref[...], wlru_ref[...], preferred_element_type=F32)
    merged = merged + gates[:, 2 * D:3 * D].astype(F32) * jnp.dot(yf_ref[...], wf_ref[...], preferred_element_type=F32)
    out = jnp.dot(merged.astype(BF16), wout_ref[...], preferred_element_type=F32)
    x1 = x_ref[...] + gt_ref[0] * out
    x1_ref[...] = x1
    u2 = _rmsnorm_mod(x1, g_ref[...], sh_ref[0], sc_ref[0])
    u2_ref[...] = u2.astype(BF16)

    G, S = N_EXPERT_GROUPS, EXPERTS_PER_GROUP
    logits = _dot_hi(wr_ref[...], u2, ((1,), (1,)))
    score = jax.nn.sigmoid(logits)
    sel = score + rb_ref[...]
    sel_s = [sel[s * G:(s + 1) * G] for s in range(S)]
    sc_s = [score[s * G:(s + 1) * G] for s in range(S)]
    gscore = None
    for a in range(S):
        for b in range(a + 1, S):
            pair = sel_s[a] + sel_s[b]
            gscore = pair if gscore is None else jnp.maximum(gscore, pair)
    giota = lax.broadcasted_iota(I32, (G, tm), 0)
    gmax = jnp.max(gscore, axis=0, keepdims=True)
    gidx = jnp.min(jnp.where(gscore == gmax, giota, G), axis=0, keepdims=True)
    gm = giota == gidx
    cand = [jnp.sum(jnp.where(gm, sel_s[s], 0.0), axis=0, keepdims=True) for s in range(S)]
    raw = [jnp.sum(jnp.where(gm, sc_s[s], 0.0), axis=0, keepdims=True) for s in range(S)]

    def first_argmax(vals):
        m = vals[0]
        for s in range(1, S):
            m = jnp.maximum(m, vals[s])
        idx = jnp.full(m.shape, S - 1, I32)
        for s in range(S - 2, -1, -1):
            idx = jnp.where(vals[s] == m, s, idx)
        return idx

    l0 = first_argmax(cand)
    l1 = first_argmax([jnp.where(l0 == s, -jnp.inf, cand[s]) for s in range(S)])

    def pick(vals, idx):
        out = vals[S - 1]
        for s in range(S - 2, -1, -1):
            out = jnp.where(idx == s, vals[s], out)
        return out

    w0 = pick(raw, l0)
    w1 = pick(raw, l1)
    wsum = w0 + w1
    w0 = w0 / wsum
    w1 = w1 / wsum

    E = G * S
    eiota = lax.broadcasted_iota(I32, (E, tm), 0)
    oh0 = eiota == (l0 * G + gidx)
    oh1 = eiota == (l1 * G + gidx)
    oh = jnp.where(oh0 | oh1, 1.0, 0.0)
    incl = jnp.dot(oh.astype(BF16), tri_ref[...], preferred_element_type=F32)
    cnt = jnp.sum(oh, axis=1, keepdims=True)
    c8 = jnp.floor((cnt + (RUN_ALIGN - 1)) * (1.0 / RUN_ALIGN)) * RUN_ALIGN
    c8b = jnp.broadcast_to(c8, (E, LANES))
    below = (lax.broadcasted_iota(I32, (E, E), 1) < lax.broadcasted_iota(I32, (E, E), 0)).astype(BF16)
    off = jnp.dot(below, c8b.astype(BF16), preferred_element_type=F32)
    local = incl - oh + off[:, 0:1]
    s0 = jnp.sum(jnp.where(oh0, local, 0.0), axis=0, keepdims=True).astype(I32)
    s1 = jnp.sum(jnp.where(oh1, local, 0.0), axis=0, keepdims=True).astype(I32)
    tc8_ref[0] = c8b.astype(I32)
    toff_ref[0] = off.astype(I32)
    tg_ref[0] = cnt_sc[...].astype(I32)
    cnt_sc[...] = cnt_sc[...] + c8b
    cnt_ref[...] = cnt_sc[...].astype(I32)

    row = lax.broadcasted_iota(I32, (SUBLANES, tm), 0)
    ls_ref[...] = jnp.where(row == 0, s0, jnp.where(row == 1, s1, 0))
    rw_ref[...] = jnp.where(row == 0, w0, jnp.where(row == 1, w1, 0.0))


def _merge(gates, y_na, y_lru, y_f, x2, wna, wlru, wf, wout, gt, g, sh, sc, wr, rb, tri, B, T):
    N, D = x2.shape
    C = y_na.shape[1]
    tm = TM_MERGE
    tpb = T // tm
    E = wr.shape[0]
    tok = lambda i: (i, 0)
    scan = lambda i: (i % tpb, i // tpb)
    per_b = lambda i: (i // tpb, 0, 0)
    const2 = lambda i: (0, 0)
    return pl.pallas_call(
        _merge_kernel,
        grid=(N // tm,),
        in_specs=[pl.BlockSpec((tm, 3 * D), tok),
                  pl.BlockSpec((tm, C), tok),
                  pl.BlockSpec((tm, C), scan),
                  pl.BlockSpec((tm, C), tok),
                  pl.BlockSpec((tm, D), tok),
                  pl.BlockSpec((C, D), const2), pl.BlockSpec((C, D), const2), pl.BlockSpec((C, D), const2),
                  pl.BlockSpec((D, D), const2),
                  pl.BlockSpec((1, 1, D), per_b),
                  pl.BlockSpec((1, D), const2),
                  pl.BlockSpec((1, 1, D), per_b),
                  pl.BlockSpec((1, 1, D), per_b),
                  pl.BlockSpec((E, D), const2),
                  pl.BlockSpec((E, 1), const2),
                  pl.BlockSpec((tm, tm), const2)],
        out_specs=[pl.BlockSpec((tm, D), tok),
                   pl.BlockSpec((tm, D), tok),
                   pl.BlockSpec((SUBLANES, tm), lambda i: (0, i)),
                   pl.BlockSpec((SUBLANES, tm), lambda i: (0, i)),
                   pl.BlockSpec((1, E, LANES), lambda i: (i, 0, 0)),
                   pl.BlockSpec((1, E, LANES), lambda i: (i, 0, 0)),
                   pl.BlockSpec((1, E, LANES), lambda i: (i, 0, 0)),
                   pl.BlockSpec((E, LANES), const2)],
        out_shape=[jax.ShapeDtypeStruct((N, D), F32),
                   jax.ShapeDtypeStruct((N, D), BF16),
                   jax.ShapeDtypeStruct((SUBLANES, N), I32),
                   jax.ShapeDtypeStruct((SUBLANES, N), F32),
                   jax.ShapeDtypeStruct((N // tm, E, LANES), I32),
                   jax.ShapeDtypeStruct((N // tm, E, LANES), I32),
                   jax.ShapeDtypeStruct((N // tm, E, LANES), I32),
                   jax.ShapeDtypeStruct((E, LANES), I32)],
        scratch_shapes=[pltpu.VMEM((E, LANES), F32)],
        compiler_params=_cparams(("arbitrary",), 48),
        name="merge_route",
    )(gates, y_na, y_lru, y_f, x2, wna, wlru, wf, wout, gt, g, sh, sc, wr, rb, tri)


def _for_run_pieces(n_rows, fn):
    def chunk(j, c):
        fn(j * RUN_CHUNK, RUN_CHUNK)
        return c

    lax.fori_loop(0, n_rows // RUN_CHUNK, chunk, 0)
    size = RUN_CHUNK // 2
    while size >= RUN_ALIGN:
        pl.when((n_rows & size) != 0)(functools.partial(fn, (n_rows // (2 * size)) * (2 * size), size))
        size //= 2


def _run_copies(tc8_ref, toff_ref, tg_ref, ps_ref, tile, hbm, stage, sem, to_hbm, wait):
    for e in range(N_EXPERTS):
        idx = tile * N_EXPERTS + e
        loc0 = toff_ref[idx]
        hbm0 = ps_ref[e] + tg_ref[idx]

        def piece(off, size, loc0=loc0, hbm0=hbm0):
            loc = stage.at[pl.ds(pl.multiple_of(loc0 + off, RUN_ALIGN), size)]
            glob = hbm.at[pl.ds(pl.multiple_of(hbm0 + off, RUN_ALIGN), size)]
            cp = pltpu.make_async_copy(loc, glob, sem) if to_hbm else pltpu.make_async_copy(glob, loc, sem)
            if wait:
                cp.wait()
            else:
                cp.start()

        _for_run_pieces(tc8_ref[idx], piece)


def _dispatch_kernel(tc8_ref, toff_ref, tg_ref, ps_ref, pad_ref, nu_ref, ls_ref, u_ref, xs_out, stage, zbuf, sem,
                     *, n_tail):
    i = pl.program_id(0)
    tm = u_ref.shape[0]
    nb = xs_out.shape[0] // MOE_ROWS

    @pl.when(i == 0)
    def _():
        zbuf[...] = jnp.zeros_like(zbuf)

        def zero_copy(blk):
            return pltpu.make_async_copy(zbuf, xs_out.at[pl.ds(pl.multiple_of(blk * MOE_ROWS, MOE_ROWS), MOE_ROWS)],
                                         sem.at[0])

        def guarded(fn):
            for e in range(N_EXPERTS):
                pl.when(pad_ref[e] >= 0)(lambda e=e: fn(zero_copy(pad_ref[e])))
            for k in range(n_tail):
                pl.when(nb - 1 - k >= nu_ref[0])(lambda k=k: fn(zero_copy(nb - 1 - k)))

        guarded(lambda cp: cp.start())
        guarded(lambda cp: cp.wait())

    slots = ls_ref[...]
    riota = lax.broadcasted_iota(I32, (stage.shape[1], tm), 0)
    hit = (riota == slots[0:1]) | (riota == slots[1:2])
    perm = jnp.where(hit, 1.0, 0.0).astype(BF16)
    cur = i % 2
    stage.at[cur][...] = jnp.dot(perm, u_ref[...], preferred_element_type=F32).astype(BF16)

    def runs(tile, buf, wait):
        _run_copies(tc8_ref, toff_ref, tg_ref, ps_ref, tile, xs_out, stage.at[buf], sem.at[buf], to_hbm=True, wait=wait)

    runs(i, cur, wait=False)
    pl.when(i > 0)(lambda: runs(i - 1, 1 - cur, wait=True))
    pl.when(i == pl.num_programs(0) - 1)(lambda: runs(i, cur, wait=True))


def _dispatch(tables, pstart, pad_blk, n_used, ls, u2, P, n_tail):
    N, D = u2.shape
    tm = TM_MOE
    n_pref = len(tables) + 3
    grid_spec = pltpu.PrefetchScalarGridSpec(
        num_scalar_prefetch=n_pref,
        grid=(N // tm,),
        in_specs=[pl.BlockSpec((SUBLANES, tm), lambda i, *_: (0, i)),
                  pl.BlockSpec((tm, D), lambda i, *_: (i, 0))],
        out_specs=pl.BlockSpec(memory_space=pl.ANY),
        scratch_shapes=[pltpu.VMEM((2, MOE_STAGE_ROWS, D), BF16), pltpu.VMEM((MOE_ROWS, D), BF16),
                        pltpu.SemaphoreType.DMA((2,))],
    )
    return pl.pallas_call(
        functools.partial(_dispatch_kernel, n_tail=n_tail),
        grid_spec=grid_spec,
        out_shape=jax.ShapeDtypeStruct((P, D), BF16),
        compiler_params=_cparams(("arbitrary",), 40),
        name="moe_dispatch",
    )(*tables, pstart, pad_blk, n_used, ls, u2)


def _expert_kernel(be_ref, nu_ref, xs_ref, wg_ref, wu_ref, wd_ref, ys_ref, wg_sc, wu_sc, wd_sc):
    i = pl.program_id(0)
    prev = be_ref[jnp.maximum(i - 1, 0)]
    fresh = (i == 0) | (be_ref[i] != prev)

    @pl.when(fresh)
    def _():
        wg_sc[...] = wg_ref[0, 0].astype(BF16)
        wu_sc[...] = wu_ref[0, 0].astype(BF16)
        wd_sc[...] = wd_ref[0, 0].astype(BF16)

    @pl.when(i < nu_ref[0])
    def _():
        xb = xs_ref[...]
        hg = jnp.dot(xb, wg_sc[...], preferred_element_type=F32)
        hu = jnp.dot(xb, wu_sc[...], preferred_element_type=F32)
        h = (hg * _sigmoid(hg)) * hu
        ys_ref[...] = jnp.dot(h.astype(BF16), wd_sc[...], preferred_element_type=F32).astype(BF16)

    @pl.when(i >= nu_ref[0])
    def _():
        ys_ref[...] = jnp.zeros_like(ys_ref)


def _experts(block_e, n_used, xs, w_gate, w_up, w_down, layer):
    P, D = xs.shape
    _, E, _, DE = w_gate.shape
    nb = P // MOE_ROWS
    wmap = lambda i, be, nu: (layer, be[i], 0, 0)
    grid_spec = pltpu.PrefetchScalarGridSpec(
        num_scalar_prefetch=2,
        grid=(nb,),
        in_specs=[pl.BlockSpec((MOE_ROWS, D), lambda i, be, nu: (i, 0)),
                  pl.BlockSpec((1, 1, D, DE), wmap),
                  pl.BlockSpec((1, 1, D, DE), wmap),
                  pl.BlockSpec((1, 1, DE, D), wmap)],
        out_specs=pl.BlockSpec((MOE_ROWS, D), lambda i, be, nu: (i, 0)),
        scratch_shapes=[pltpu.VMEM((D, DE), BF16), pltpu.VMEM((D, DE), BF16), pltpu.VMEM((DE, D), BF16)],
    )
    return pl.pallas_call(
        _expert_kernel,
        grid_spec=grid_spec,
        out_shape=jax.ShapeDtypeStruct((P, D), BF16),
        compiler_params=_cparams(("arbitrary",), 48),
        name="moe_experts",
    )(block_e, n_used, xs, w_gate, w_up, w_down)


def _combine_kernel(tc8_ref, toff_ref, tg_ref, ps_ref, ls_ref, rw_ref, x_ref, gt_ref, fg_ref, ys_ref, o_ref,
                    stage, sem, *, final):
    i = pl.program_id(0)
    tm = x_ref.shape[0]

    def runs(tile, buf, wait):
        _run_copies(tc8_ref, toff_ref, tg_ref, ps_ref, tile, ys_ref, stage.at[buf], sem.at[buf], to_hbm=False, wait=wait)

    @pl.when(i == 0)
    def _():
        stage[...] = jnp.zeros_like(stage)
        runs(0, 0, wait=False)

    cur = i % 2
    pl.when(i + 1 < pl.num_programs(0))(lambda: runs(i + 1, 1 - cur, wait=False))
    runs(i, cur, wait=True)
    rows = stage[cur]
    liota = lax.broadcasted_iota(I32, (tm, stage.shape[1]), 1)
    slots = ls_ref[...]
    w = rw_ref[...]
    pick = jnp.where(liota == slots[:, 0:1], w[:, 0:1], 0.0)
    for k in range(1, TOP_K):
        pick = pick + jnp.where(liota == slots[:, k:k + 1], w[:, k:k + 1], 0.0)
    y = jnp.dot(pick.astype(BF16), rows, preferred_element_type=F32)
    x2 = x_ref[...] + gt_ref[0] * y
    if final:
        x2 = (x2 * lax.rsqrt(jnp.mean(x2 * x2, axis=-1, keepdims=True) + EPS)) * fg_ref[...]
    o_ref[...] = x2


def _combine(tables, pstart, ls_t, rw_t, x1, gt, final_g, ys, T, final):
    N, D = x1.shape
    tm = TM_MOE
    tpb = T // tm
    grid_spec = pltpu.PrefetchScalarGridSpec(
        num_scalar_prefetch=len(tables) + 1,
        grid=(N // tm,),
        in_specs=[pl.BlockSpec((tm, SUBLANES), lambda i, *_: (i, 0)),
                  pl.BlockSpec((tm, SUBLANES), lambda i, *_: (i, 0)),
                  pl.BlockSpec((tm, D), lambda i, *_: (i, 0)),
                  pl.BlockSpec((1, 1, D), lambda i, *_: (i // tpb, 0, 0)),
                  pl.BlockSpec((1, D), lambda i, *_: (0, 0)),
                  pl.BlockSpec(memory_space=pl.ANY)],
        out_specs=pl.BlockSpec((tm, D), lambda i, *_: (i, 0)),
        scratch_shapes=[pltpu.VMEM((2, MOE_STAGE_ROWS, D), BF16), pltpu.SemaphoreType.DMA((2,))],
    )
    return pl.pallas_call(
        functools.partial(_combine_kernel, final=final),
        grid_spec=grid_spec,
        out_shape=jax.ShapeDtypeStruct((N, D), F32),
        compiler_params=_cparams(("arbitrary",), 40),
        name="moe_combine",
    )(*tables, pstart, ls_t, rw_t, x1, gt, final_g, ys)


def _moe_plan(cnt, nb):
    G, S = N_EXPERT_GROUPS, EXPERTS_PER_GROUP
    counts = cnt[:, 0]
    padded = (counts + MOE_ROWS - 1) // MOE_ROWS * MOE_ROWS
    pend = jnp.cumsum(padded).astype(I32)
    pstart = pend - padded
    pad_blk = jnp.where(padded > 0, pend // MOE_ROWS - 1, -1).astype(I32)
    block_start = jnp.arange(nb, dtype=I32) * MOE_ROWS
    block_row = jnp.minimum(jnp.sum((pend[None, :] <= block_start[:, None]).astype(I32), axis=1), G * S - 1)
    block_e = (block_row % G) * S + block_row // G
    n_used = pend[-1:] // MOE_ROWS
    return pstart, pad_blk, block_e.astype(I32), n_used.astype(I32)


def _block_diag(w):
    nb, bw, _ = w.shape
    eye = jnp.eye(nb, dtype=w.dtype)
    return jnp.einsum('nij,nm->nimj', w, eye).reshape(nb * bw, nb * bw)


def kernel(x, c, ada_w, ada_b, norm_mix_g, norm_ffn_g, w_in, w_branch_gate, b_branch_gate,
           na_rpb, lru_conv_w, lru_conv_b, lru_w_r, lru_b_r, lru_w_i, lru_b_i, lru_lambda,
           w_proj_na, w_proj_lru, w_proj_fnet, w_out, w_router, router_bias,
           w_exp_gate, w_exp_up, w_exp_down, final_g):
    B, T, D = x.shape
    L = ada_w.shape[0]
    N = B * T
    E = N_EXPERTS
    G, S = N_EXPERT_GROUPS, EXPERTS_PER_GROUP
    A = N * TOP_K
    assert TM_MERGE == TM_MOE
    max_rows = A + (N // TM_MOE) * E * (RUN_ALIGN - 1) + E * (MOE_ROWS - 1)
    nb = -(-max_rows // MOE_ROWS)
    P = nb * MOE_ROWS

    mod = _ada_mod(c, ada_w, ada_b)
    m1_np, cs_np = _fnet_tables(T)
    m1 = jnp.asarray(m1_np).astype(BF16)
    cs = jnp.asarray(cs_np).astype(BF16)
    tri = (lax.broadcasted_iota(I32, (TM_MERGE, TM_MERGE), 0)
           <= lax.broadcasted_iota(I32, (TM_MERGE, TM_MERGE), 1)).astype(BF16)
    wr = w_router.T.reshape(G, S, D).transpose(1, 0, 2).reshape(E, D)
    rb = router_bias.reshape(G, S).T.reshape(E, 1)
    fg = final_g.reshape(1, D)

    x2 = x.reshape(N, D)
    for l in range(L):
        sh_mix, sc_mix, gt_mix, sh_ffn, sc_ffn, gt_ffn = [m.reshape(B, 1, D) for m in jnp.split(mod[l], 6, axis=-1)]
        wcat = jnp.concatenate([w_in[l], w_branch_gate[l]], axis=1).astype(BF16)
        q, k, v, xr, gy, fr, gates = _mix_in(x2, norm_mix_g[l].reshape(1, D), sh_mix, sc_mix, wcat,
                                             b_branch_gate[l].reshape(1, -1), B, T)
        y_na = _na(q, k, v, _na_bias_table(na_rpb[l]), B, T)

        wg = (0.5 * jnp.stack([jnp.concatenate([_block_diag(lru_w_r[l, d]), _block_diag(lru_w_i[l, d])], axis=1)
                               for d in range(2)])).astype(BF16)
        bgl = 0.5 * jnp.stack([jnp.concatenate([lru_b_r[l, d], lru_b_i[l, d]]) for d in range(2)])[:, None, :]
        af, bf, ab, bb = _lru_gates(xr, lru_conv_w[l], lru_conv_b[l].reshape(1, -1), wg, bgl,
                                    lru_lambda[l][:, None, :], B, T)
        y_lru = _lru_scans(af, bf, ab, bb, gy)
        y_f = _fnet(fr, m1, cs, B, T)

        x1, u2, ls, rw, tc8, toff, tg, cnt = _merge(
            gates, y_na, y_lru, y_f, x2,
            w_proj_na[l].astype(BF16), w_proj_lru[l].astype(BF16),
            w_proj_fnet[l].astype(BF16), w_out[l].astype(BF16),
            gt_mix, norm_ffn_g[l].reshape(1, D), sh_ffn, sc_ffn, wr, rb, tri, B, T)
        tables = [t[:, :, 0].reshape(-1) for t in (tc8, toff, tg)]
        pstart, pad_blk, block_e, n_used = _moe_plan(cnt, nb)
        xs = _dispatch(tables, pstart, pad_blk, n_used, ls, u2, P, nb - A // MOE_ROWS)
        ys = _experts(block_e, n_used, xs, w_exp_gate, w_exp_up, w_exp_down, l)
        x2 = _combine(tables, pstart, ls.T, rw.T, x1, gt_ffn, fg, ys, T, final=(l == L - 1))
    return x2.reshape(B, T, D)
```

```python
import functools

import numpy as np
import jax
import jax.numpy as jnp
from jax import lax
from jax.experimental import pallas as pl
from jax.experimental.pallas import tpu as pltpu

F32 = jnp.float32
BF16 = jnp.bfloat16
I32 = jnp.int32

GRID_W = 64
NA_HEAD_DIM = 64
NA_WIN_ROWS = 8
NA_WIN_COLS = 16
NA_HEADS_PER_TILE = 4
LRU_BLOCKS = 8
LRU_C = 8.0
CONV_W = 4
FNET_GROUP_W = 128
N_EXPERTS = 32
N_EXPERT_GROUPS = 8
EXPERTS_PER_GROUP = 4
TOP_K = 2
EPS = 1e-6
MASK_VALUE = -1e30

V7X_VMEM_BYTES = 64 * 1024 * 1024
LANES = 128
SUBLANES = 8

TM_MIX = 1024
TN_MIX = 1024
MIX_ROW_CHUNK = 512
NA_ROWS_PER_STEP = 8
TM_LRU = 512
T_SCAN = 512
TM_MERGE = 512
TM_MOE = 512
MOE_ROWS = 512
RUN_CHUNK = 64
RUN_ALIGN = 16
MOE_STAGE_ROWS = -(-(TM_MOE * TOP_K + N_EXPERTS * (RUN_ALIGN - 1)) // 256) * 256


def _cparams(semantics, vmem_mb):
    assert vmem_mb * 1024 * 1024 < V7X_VMEM_BYTES
    return pltpu.CompilerParams(dimension_semantics=semantics, vmem_limit_bytes=vmem_mb * 1024 * 1024)


def _split2(a):
    hi = a.astype(BF16)
    lo = (a - hi.astype(F32)).astype(BF16)
    return hi, lo


def _dot_hi(a, b, dims):
    a_hi, a_lo = _split2(a)
    b_hi, b_lo = _split2(b)
    dn = (dims, ((), ()))
    m = a.shape[0]
    both = lax.dot_general(jnp.concatenate([a_hi, a_lo], axis=0), b_hi, dn, preferred_element_type=F32)
    return (both[:m] + both[m:]) + lax.dot_general(a_hi, b_lo, dn, preferred_element_type=F32)


def _sigmoid(x):
    return 0.5 * jnp.tanh(0.5 * x) + 0.5


def _rmsnorm_mod(x, g, sh, sc):
    y = x * lax.rsqrt(jnp.mean(x * x, axis=-1, keepdims=True) + EPS)
    return (y * g) * (1.0 + sc) + sh


def _ada_kernel(c_ref, w_ref, b_ref, o_ref):
    c = c_ref[...]
    ca = c * jax.nn.sigmoid(c)
    o_ref[0] = _dot_hi(ca, w_ref[0], ((1,), (0,))) + b_ref[0]


def _ada_mod(c, ada_w, ada_b):
    L, D, D6 = ada_w.shape
    B = c.shape[0]
    bp = -(-B // SUBLANES) * SUBLANES
    cp = jnp.pad(c, ((0, bp - B), (0, 0)))
    tn = D6 // 4
    out = pl.pallas_call(
        _ada_kernel,
        grid=(L, D6 // tn),
        in_specs=[pl.BlockSpec((bp, D), lambda l, j: (0, 0)),
                  pl.BlockSpec((1, D, tn), lambda l, j: (l, 0, j)),
                  pl.BlockSpec((1, 1, tn), lambda l, j: (l, 0, j))],
        out_specs=pl.BlockSpec((1, bp, tn), lambda l, j: (l, 0, j)),
        out_shape=jax.ShapeDtypeStruct((L, bp, D6), F32),
        compiler_params=_cparams(("parallel", "parallel"), 40),
        name="ada_mod",
    )(cp, ada_w, ada_b.reshape(L, 1, D6))
    return out[:, :B]


def _mixin_kernel(x_ref, g_ref, sh_ref, sc_ref, w_ref, bg_ref,
                  q_ref, k_ref, v_ref, xr_ref, gy_ref, fr_ref, gate_ref, u_sc):
    j = pl.program_id(1)

    @pl.when(j == 0)
    def _():
        u_sc[...] = _rmsnorm_mod(x_ref[...], g_ref[...], sh_ref[0], sc_ref[0]).astype(BF16)

    tm = u_sc.shape[0]
    W = q_ref.shape[1]
    lo, hi = slice(0, W), slice(W, 2 * W)

    def per_chunk(store):
        for c in range(tm // MIX_ROW_CHUNK):
            rows = slice(c * MIX_ROW_CHUNK, (c + 1) * MIX_ROW_CHUNK)
            store(rows, jnp.dot(u_sc[rows, :], w_ref[...], preferred_element_type=F32))

    @pl.when(j == 0)
    def _():
        def store(rows, acc):
            q_ref[rows, :] = acc[:, lo].astype(BF16)
            k_ref[rows, :] = acc[:, hi].astype(BF16)
        per_chunk(store)

    @pl.when(j == 1)
    def _():
        def store(rows, acc):
            v_ref[rows, :] = acc[:, lo].astype(BF16)
            xr_ref[rows, :] = acc[:, hi]
        per_chunk(store)

    @pl.when(j == 2)
    def _():
        def store(rows, acc):
            gy_ref[rows, :] = jax.nn.gelu(acc[:, lo]).astype(BF16)
            fr_ref[rows, :] = acc[:, hi].astype(BF16)
        per_chunk(store)

    @pl.when(j >= 3)
    def _():
        def store(rows, acc):
            gate_ref[rows, :] = _sigmoid(acc + bg_ref[...]).astype(BF16)
        per_chunk(store)


def _mix_in(x2, g, sh, sc, wcat, bgate, B, T):
    N, D = x2.shape
    tm, tn = TM_MIX, TN_MIX
    tpb = T // tm
    W = tn // 2
    n_proj = 3
    n_gate = bgate.shape[1] // tn
    tok = lambda i, j: (i, 0)
    scan = lambda i, j: (i % tpb, i // tpb)
    per_b = lambda i, j: (i // tpb, 0, 0)
    out_shape = [jax.ShapeDtypeStruct((N, W), BF16)] * 3 + [
        jax.ShapeDtypeStruct((N, W), F32),
        jax.ShapeDtypeStruct((T, B * W), BF16),
        jax.ShapeDtypeStruct((N, W), BF16),
        jax.ShapeDtypeStruct((N, n_gate * tn), BF16)]
    out_specs = [pl.BlockSpec((tm, W), tok)] * 3 + [
        pl.BlockSpec((tm, W), tok),
        pl.BlockSpec((tm, W), scan),
        pl.BlockSpec((tm, W), tok),
        pl.BlockSpec((tm, tn), lambda i, j: (i, jnp.maximum(j - n_proj, 0)))]
    return pl.pallas_call(
        _mixin_kernel,
        grid=(N // tm, n_proj + n_gate),
        in_specs=[pl.BlockSpec((tm, D), tok),
                  pl.BlockSpec((1, D), lambda i, j: (0, 0)),
                  pl.BlockSpec((1, 1, D), per_b),
                  pl.BlockSpec((1, 1, D), per_b),
                  pl.BlockSpec((D, tn), lambda i, j: (0, j)),
                  pl.BlockSpec((1, tn), lambda i, j: (0, jnp.maximum(j - n_proj, 0)))],
        out_specs=out_specs,
        out_shape=out_shape,
        scratch_shapes=[pltpu.VMEM((tm, D), BF16)],
        compiler_params=_cparams(("parallel", "arbitrary"), 48),
        name="mix_in",
    )(x2, g, sh, sc, wcat, bgate)


def _na_kernel(q_ref, kp_ref, kc_ref, kn_ref, vp_ref, vc_ref, vn_ref, bias_ref, o_ref, kb, vb):
    j = pl.program_id(1)
    nj = pl.num_programs(1)
    R = NA_ROWS_PER_STEP
    half = R // 2
    W = GRID_W
    hw = half * W
    tq = R * W
    win = NA_WIN_ROWS * W
    HT = NA_HEADS_PER_TILE
    lt = HT * NA_HEAD_DIM
    kb[0:hw] = kp_ref[...]
    kb[hw:hw + tq] = kc_ref[...]
    kb[hw + tq:2 * hw + tq] = kn_ref[...]
    vb[0:hw] = vp_ref[...]
    vb[hw:hw + tq] = vc_ref[...]
    vb[hw + tq:2 * hw + tq] = vn_ref[...]
    head_of_lane = lax.broadcasted_iota(I32, (W, lt), 1) // NA_HEAD_DIM
    n_tiles = q_ref.shape[1] // lt
    scale = jnp.asarray(NA_HEAD_DIM ** -0.5, BF16)
    for dr in range(R):
        lo = jnp.where(j == 0, max(dr, half), jnp.where(j == nj - 1, min(dr, half), dr))
        var = lo - dr + (half - 1)
        start = pl.multiple_of(lo * W, W)
        for ht in range(n_tiles):
            cs = slice(ht * lt, (ht + 1) * lt)
            qrow = q_ref[dr * W:(dr + 1) * W, cs] * scale
            qbd = jnp.concatenate(
                [jnp.where(head_of_lane == h, qrow, jnp.zeros_like(qrow)) for h in range(HT)], axis=0)
            kwin = kb[pl.ds(start, win), cs]
            s = lax.dot_general(qbd, kwin, (((1,), (1,)), ((), ())), preferred_element_type=F32)
            s = s + bias_ref[var, ht]
            m = jnp.max(s, axis=-1, keepdims=True)
            p = jnp.exp(s - m)
            l = jnp.sum(p, axis=-1, keepdims=True)
            vwin = vb[pl.ds(start, win), cs]
            o = jnp.dot(p.astype(BF16), vwin, preferred_element_type=F32) * (1.0 / l)
            orow = jnp.where(head_of_lane == 0, o[0:W], 0.0)
            for h in range(1, HT):
                orow = orow + jnp.where(head_of_lane == h, o[h * W:(h + 1) * W], 0.0)
            o_ref[dr * W:(dr + 1) * W, cs] = orow.astype(BF16)


def _na_bias_table(rpb):
    H = rpb.shape[0]
    W, KR, KC = GRID_W, NA_WIN_ROWS, NA_WIN_COLS
    w = np.arange(W)
    cstart = np.clip(w - KC // 2, 0, W - KC)
    wk = np.arange(W)
    col_ok = (wk[None, :] >= cstart[:, None]) & (wk[None, :] < cstart[:, None] + KC)
    cb = wk[None, :] - w[:, None] + (KC - 1)
    onehot = ((cb[None] == np.arange(2 * KC - 1)[:, None, None]) & col_ok[None]).astype(np.float32)
    toep = jnp.einsum('hac,cwx->hawx', rpb.astype(F32), jnp.asarray(onehot), precision=lax.Precision.HIGHEST)
    toep = jnp.where(jnp.asarray(col_ok)[None, None], toep, MASK_VALUE)
    tab = jnp.stack([toep[:, v:v + KR] for v in range(KR)])
    tab = tab.transpose(0, 1, 3, 2, 4)
    HT = NA_HEADS_PER_TILE
    return tab.reshape(KR, H // HT, HT * W, KR * W)


def _na(q, k, v, bias, B, T):
    N, C = q.shape
    R = NA_ROWS_PER_STEP
    tq = R * GRID_W
    hw = tq // 2
    nj = T // tq
    nh = T // hw
    qmap = lambda b, j: (b * nj + j, 0)
    pmap = lambda b, j: (b * nh + jnp.maximum(2 * j - 1, 0), 0)
    nmap = lambda b, j: (b * nh + jnp.minimum(2 * j + 2, nh - 1), 0)
    return pl.pallas_call(
        _na_kernel,
        grid=(B, nj),
        in_specs=[pl.BlockSpec((tq, C), qmap),
                  pl.BlockSpec((hw, C), pmap), pl.BlockSpec((tq, C), qmap), pl.BlockSpec((hw, C), nmap),
                  pl.BlockSpec((hw, C), pmap), pl.BlockSpec((tq, C), qmap), pl.BlockSpec((hw, C), nmap),
                  pl.BlockSpec(bias.shape, lambda b, j: (0, 0, 0, 0))],
        out_specs=pl.BlockSpec((tq, C), qmap),
        out_shape=jax.ShapeDtypeStruct((N, C), BF16),
        scratch_shapes=[pltpu.VMEM((2 * tq, C), BF16), pltpu.VMEM((2 * tq, C), BF16)],
        compiler_params=_cparams(("parallel", "parallel"), 48),
        name="na_attention",
    )(q, k, k, k, v, v, v, bias)


def _lru_gate_kernel(x_ref, xp_ref, xn_ref, cw_ref, cb_ref, wg_ref, bg_ref, lam_ref,
                     af_ref, bf_ref, ab_ref, bb_ref, xpad, *, tpb):
    i = pl.program_id(0)
    tm, C = x_ref.shape
    H = SUBLANES
    first = (i % tpb) == 0
    last = (i % tpb) == tpb - 1
    xpad[0:H] = jnp.where(first, 0.0, xp_ref[...])
    xpad[H:H + tm] = x_ref[...]
    xpad[H + tm:2 * H + tm] = jnp.where(last, 0.0, xn_ref[...])
    left = CONV_W // 2
    xc = xpad[H - left:H - left + tm] * cw_ref[0:1]
    for t in range(1, CONV_W):
        xc = xc + xpad[H - left + t:H - left + t + tm] * cw_ref[t:t + 1]
    xc = xc + cb_ref[...]
    xcb = xc.astype(BF16)
    hx = 0.5 * xc
    for d, (a_ref, b_ref) in enumerate(((af_ref, bf_ref), (ab_ref, bb_ref))):
        t = jnp.tanh(jnp.dot(xcb, wg_ref[d], preferred_element_type=F32) + bg_ref[d])
        c = (-0.5 * LRU_C) * jax.nn.softplus(-lam_ref[d])
        log_a = c * t[:, :C] + c
        a = jnp.exp(log_a)
        a_ref[...] = a
        b_ref[...] = jnp.sqrt(1.0 - a * a) * (t[:, C:] * hx + hx)


def _lru_gates(xr, conv_w, conv_b, wg, bg, lam, B, T):
    N, C = xr.shape
    tm = TM_LRU
    tpb = T // tm
    nb8 = N // SUBLANES
    r8 = tm // SUBLANES
    scan = lambda i: (i % tpb, i // tpb)
    const2 = lambda i: (0, 0)
    const3 = lambda i: (0, 0, 0)
    outs = pl.pallas_call(
        functools.partial(_lru_gate_kernel, tpb=tpb),
        grid=(N // tm,),
        in_specs=[pl.BlockSpec((tm, C), lambda i: (i, 0)),
                  pl.BlockSpec((SUBLANES, C), lambda i: (jnp.maximum(i * r8 - 1, 0), 0)),
                  pl.BlockSpec((SUBLANES, C), lambda i: (jnp.minimum((i + 1) * r8, nb8 - 1), 0)),
                  pl.BlockSpec((CONV_W, C), const2),
                  pl.BlockSpec((1, C), const2),
                  pl.BlockSpec((2, C, 2 * C), const3),
                  pl.BlockSpec((2, 1, 2 * C), const3),
                  pl.BlockSpec((2, 1, C), const3)],
        out_specs=[pl.BlockSpec((tm, C), scan)] * 4,
        out_shape=[jax.ShapeDtypeStruct((T, B * C), F32)] * 4,
        scratch_shapes=[pltpu.VMEM((tm + 2 * SUBLANES, C), F32)],
        compiler_params=_cparams(("parallel",), 48),
        name="lru_gates",
    )(xr, xr, xr, conv_w, conv_b, wg, bg, lam)
    return outs


def _scan_bwd_kernel(a_ref, b_ref, h_ref, carry):
    @pl.when(pl.program_id(0) == 0)
    def _():
        carry[...] = jnp.zeros_like(carry)

    tc = a_ref.shape[0]

    def body(s, h):
        t = tc - 1 - s
        h = a_ref[t] * h + b_ref[t]
        h_ref[t] = h
        return h

    carry[...] = lax.fori_loop(0, tc, body, carry[...], unroll=8)


def _scan_fwd_kernel(a_ref, b_ref, hb_ref, gy_ref, y_ref, carry):
    @pl.when(pl.program_id(0) == 0)
    def _():
        carry[...] = jnp.zeros_like(carry)

    tc = a_ref.shape[0]

    def body(t, h):
        h = a_ref[t] * h + b_ref[t]
        y_ref[t] = ((h + hb_ref[t]) * gy_ref[t].astype(F32)).astype(BF16)
        return h

    carry[...] = lax.fori_loop(0, tc, body, carry[...], unroll=8)


def _lru_scans(af, bf, ab, bb, gy):
    T, BC = af.shape
    S = BC // LANES
    tc = T_SCAN
    nc = T // tc
    v3 = lambda z: z.reshape(T, S, LANES)
    blk = (tc, S, LANES)
    fwd = lambda c: (c, 0, 0)
    rev = lambda c: (nc - 1 - c, 0, 0)
    hb = pl.pallas_call(
        _scan_bwd_kernel,
        grid=(nc,),
        in_specs=[pl.BlockSpec(blk, rev), pl.BlockSpec(blk, rev)],
        out_specs=pl.BlockSpec(blk, rev),
        out_shape=jax.ShapeDtypeStruct((T, S, LANES), F32),
        scratch_shapes=[pltpu.VMEM((S, LANES), F32)],
        compiler_params=_cparams(("arbitrary",), 48),
        name="lru_scan_bwd",
    )(v3(ab), v3(bb))
    y = pl.pallas_call(
        _scan_fwd_kernel,
        grid=(nc,),
        in_specs=[pl.BlockSpec(blk, fwd)] * 4,
        out_specs=pl.BlockSpec(blk, fwd),
        out_shape=jax.ShapeDtypeStruct((T, S, LANES), BF16),
        scratch_shapes=[pltpu.VMEM((S, LANES), F32)],
        compiler_params=_cparams(("arbitrary",), 48),
        name="lru_scan_fwd",
    )(v3(af), v3(bf), hb, v3(gy))
    return y.reshape(T, BC)


def _fnet_tables(T):
    T2 = FNET_GROUP_W
    T1 = T // T2
    k1 = np.arange(T1)[:, None]
    t1 = np.arange(T1)[None, :]
    stage1 = np.zeros((T2, 2 * T1, 2 * T1), np.float32)
    for t2 in range(T2):
        ph = 2.0 * np.pi * ((k1 * (T2 * t1 + t2)) % T) / T
        c, s = np.cos(ph), np.sin(ph)
        stage1[t2] = np.block([[c, s], [-s, c]])
    n = np.arange(T2)
    ph2 = 2.0 * np.pi * ((n[:, None] * n[None, :]) % T2) / T2
    c2, s2 = np.cos(ph2).astype(np.float32), np.sin(ph2).astype(np.float32)
    return stage1, np.concatenate([c2, s2], axis=1)


def _fnet_kernel(x_ref, m1_ref, cs_ref, o_ref, zr, zi, ar, ai, y8, yo, *, T1):
    T2 = FNET_GROUP_W
    T = T1 * T2
    S = SUBLANES
    cs = cs_ref[...]
    for t1 in range(T1):
        z = jnp.dot(x_ref[t1 * T2:(t1 + 1) * T2, :], cs, preferred_element_type=F32)
        zr[t1] = z[:, :T2]
        zi[t1] = -z[:, T2:]
    for g in range(T2 // S):
        grp = slice(g * S, (g + 1) * S)
        cr = jnp.swapaxes(zr[:, grp, :], 0, 1)
        ci = jnp.swapaxes(zi[:, grp, :], 0, 1)
        for j in range(S):
            t2 = g * S + j
            zin = jnp.concatenate([cr[j], ci[j]], axis=0).astype(BF16)
            a = jnp.dot(m1_ref[t2], zin, preferred_element_type=F32)
            ar[t2] = a[:T1]
            ai[t2] = a[T1:]
    inv = 1.0 / np.sqrt(float(T) * T2)
    for g in range(T1 // S):
        grp = slice(g * S, (g + 1) * S)
        cr = jnp.swapaxes(ar[:, grp, :], 0, 1)
        ci = jnp.swapaxes(ai[:, grp, :], 0, 1)
        for j in range(S):
            ain = jnp.concatenate([cr[j], ci[j]], axis=0).astype(BF16)
            y8[j] = jnp.dot(cs, ain, preferred_element_type=F32) * inv
        yo[:, grp, :] = jnp.swapaxes(y8[...], 0, 1)
    o_ref[...] = yo[...].reshape(T, T2).astype(BF16)


def _fnet(fr, m1, cs, B, T):
    N, C = fr.shape
    G = C // FNET_GROUP_W
    T1 = T // FNET_GROUP_W
    blk = pl.BlockSpec((T, FNET_GROUP_W), lambda b, g: (b, g))
    return pl.pallas_call(
        functools.partial(_fnet_kernel, T1=T1),
        grid=(B, G),
        in_specs=[blk,
                  pl.BlockSpec(m1.shape, lambda b, g: (0, 0, 0)),
                  pl.BlockSpec(cs.shape, lambda b, g: (0, 0))],
        out_specs=blk,
        out_shape=jax.ShapeDtypeStruct((N, C), BF16),
        scratch_shapes=[pltpu.VMEM((T1, FNET_GROUP_W, FNET_GROUP_W), F32)] * 2
        + [pltpu.VMEM((FNET_GROUP_W, T1, FNET_GROUP_W), F32)] * 2
        + [pltpu.VMEM((SUBLANES, FNET_GROUP_W, FNET_GROUP_W), F32),
           pltpu.VMEM((FNET_GROUP_W, T1, FNET_GROUP_W), F32)],
        compiler_params=_cparams(("parallel", "parallel"), 48),
        name="fnet",
    )(fr, m1, cs)


def _merge_kernel(gate_ref, yna_ref, ylru_ref, yf_ref, x_ref, wna_ref, wlru_ref, wf_ref, wout_ref,
                  gt_ref, g_ref, sh_ref, sc_ref, wr_ref, rb_ref, tri_ref,
                  x1_ref, u2_ref, ls_ref, rw_ref, tc8_ref, toff_ref, tg_ref, cnt_ref, cnt_sc):
    i = pl.program_id(0)

    @pl.when(i == 0)
    def _():
        cnt_sc[...] = jnp.zeros_like(cnt_sc)

    tm, D = x_ref.shape
    gates = gate_ref[...]
    merged = gates[:, 0:D].astype(F32) * jnp.dot(yna_ref[...], wna_ref[...], preferred_element_type=F32)
    merged = merged + gates[:, D:2 * D].astype(F32) * jnp.dot(ylru_ref[...], wlru_ref[...], preferred_element_type=F32)
    merged = merged + gates[:, 2 * D:3 * D].astype(F32) * jnp.dot(yf_ref[...], wf_ref[...], preferred_element_type=F32)
    out = jnp.dot(merged.astype(BF16), wout_ref[...], preferred_element_type=F32)
    x1 = x_ref[...] + gt_ref[0] * out
    x1_ref[...] = x1
    u2 = _rmsnorm_mod(x1, g_ref[...], sh_ref[0], sc_ref[0])
    u2_ref[...] = u2.astype(BF16)

    G, S = N_EXPERT_GROUPS, EXPERTS_PER_GROUP
    logits = _dot_hi(wr_ref[...], u2, ((1,), (1,)))
    score = jax.nn.sigmoid(logits)
    sel = score + rb_ref[...]
    sel_s = [sel[s * G:(s + 1) * G] for s in range(S)]
    sc_s = [score[s * G:(s + 1) * G] for s in range(S)]
    gscore = None
    for a in range(S):
        for b in range(a + 1, S):
            pair = sel_s[a] + sel_s[b]
            gscore = pair if gscore is None else jnp.maximum(gscore, pair)
    giota = lax.broadcasted_iota(I32, (G, tm), 0)
    gmax = jnp.max(gscore, axis=0, keepdims=True)
    gidx = jnp.min(jnp.where(gscore == gmax, giota, G), axis=0, keepdims=True)
    gm = giota == gidx
    cand = [jnp.sum(jnp.where(gm, sel_s[s], 0.0), axis=0, keepdims=True) for s in range(S)]
    raw = [jnp.sum(jnp.where(gm, sc_s[s], 0.0), axis=0, keepdims=True) for s in range(S)]

    def first_argmax(vals):
        m = vals[0]
        for s in range(1, S):
            m = jnp.maximum(m, vals[s])
        idx = jnp.full(m.shape, S - 1, I32)
        for s in range(S - 2, -1, -1):
            idx = jnp.where(vals[s] == m, s, idx)
        return idx

    l0 = first_argmax(cand)
    l1 = first_argmax([jnp.where(l0 == s, -jnp.inf, cand[s]) for s in range(S)])

    def pick(vals, idx):
        out = vals[S - 1]
        for s in range(S - 2, -1, -1):
            out = jnp.where(idx == s, vals[s], out)
        return out

    w0 = pick(raw, l0)
    w1 = pick(raw, l1)
    wsum = w0 + w1
    w0 = w0 / wsum
    w1 = w1 / wsum

    E = G * S
    eiota = lax.broadcasted_iota(I32, (E, tm), 0)
    oh0 = eiota == (l0 * G + gidx)
    oh1 = eiota == (l1 * G + gidx)
    oh = jnp.where(oh0 | oh1, 1.0, 0.0)
    incl = jnp.dot(oh.astype(BF16), tri_ref[...], preferred_element_type=F32)
    cnt = jnp.sum(oh, axis=1, keepdims=True)
    c8 = jnp.floor((cnt + (RUN_ALIGN - 1)) * (1.0 / RUN_ALIGN)) * RUN_ALIGN
    c8b = jnp.broadcast_to(c8, (E, LANES))
    below = (lax.broadcasted_iota(I32, (E, E), 1) < lax.broadcasted_iota(I32, (E, E), 0)).astype(BF16)
    off = jnp.dot(below, c8b.astype(BF16), preferred_element_type=F32)
    local = incl - oh + off[:, 0:1]
    s0 = jnp.sum(jnp.where(oh0, local, 0.0), axis=0, keepdims=True).astype(I32)
    s1 = jnp.sum(jnp.where(oh1, local, 0.0), axis=0, keepdims=True).astype(I32)
    tc8_ref[0] = c8b.astype(I32)
    toff_ref[0] = off.astype(I32)
    tg_ref[0] = cnt_sc[...].astype(I32)
    cnt_sc[...] = cnt_sc[...] + c8b
    cnt_ref[...] = cnt_sc[...].astype(I32)

    row = lax.broadcasted_iota(I32, (SUBLANES, tm), 0)
    ls_ref[...] = jnp.where(row == 0, s0, jnp.where(row == 1, s1, 0))
    rw_ref[...] = jnp.where(row == 0, w0, jnp.where(row == 1, w1, 0.0))


def _merge(gates, y_na, y_lru, y_f, x2, wna, wlru, wf, wout, gt, g, sh, sc, wr, rb, tri, B, T):
    N, D = x2.shape
    C = y_na.shape[1]
    tm = TM_MERGE
    tpb = T // tm
    E = wr.shape[0]
    tok = lambda i: (i, 0)
    scan = lambda i: (i % tpb, i // tpb)
    per_b = lambda i: (i // tpb, 0, 0)
    const2 = lambda i: (0, 0)
    return pl.pallas_call(
        _merge_kernel,
        grid=(N // tm,),
        in_specs=[pl.BlockSpec((tm, 3 * D), tok),
                  pl.BlockSpec((tm, C), tok),
                  pl.BlockSpec((tm, C), scan),
                  pl.BlockSpec((tm, C), tok),
                  pl.BlockSpec((tm, D), tok),
                  pl.BlockSpec((C, D), const2), pl.BlockSpec((C, D), const2), pl.BlockSpec((C, D), const2),
                  pl.BlockSpec((D, D), const2),
                  pl.BlockSpec((1, 1, D), per_b),
                  pl.BlockSpec((1, D), const2),
                  pl.BlockSpec((1, 1, D), per_b),
                  pl.BlockSpec((1, 1, D), per_b),
                  pl.BlockSpec((E, D), const2),
                  pl.BlockSpec((E, 1), const2),
                  pl.BlockSpec((tm, tm), const2)],
        out_specs=[pl.BlockSpec((tm, D), tok),
                   pl.BlockSpec((tm, D), tok),
                   pl.BlockSpec((SUBLANES, tm), lambda i: (0, i)),
                   pl.BlockSpec((SUBLANES, tm), lambda i: (0, i)),
                   pl.BlockSpec((1, E, LANES), lambda i: (i, 0, 0)),
                   pl.BlockSpec((1, E, LANES), lambda i: (i, 0, 0)),
                   pl.BlockSpec((1, E, LANES), lambda i: (i, 0, 0)),
                   pl.BlockSpec((E, LANES), const2)],
        out_shape=[jax.ShapeDtypeStruct((N, D), F32),
                   jax.ShapeDtypeStruct((N, D), BF16),
                   jax.ShapeDtypeStruct((SUBLANES, N), I32),
                   jax.ShapeDtypeStruct((SUBLANES, N), F32),
                   jax.ShapeDtypeStruct((N // tm, E, LANES), I32),
                   jax.ShapeDtypeStruct((N // tm, E, LANES), I32),
                   jax.ShapeDtypeStruct((N // tm, E, LANES), I32),
                   jax.ShapeDtypeStruct((E, LANES), I32)],
        scratch_shapes=[pltpu.VMEM((E, LANES), F32)],
        compiler_params=_cparams(("arbitrary",), 48),
        name="merge_route",
    )(gates, y_na, y_lru, y_f, x2, wna, wlru, wf, wout, gt, g, sh, sc, wr, rb, tri)


def _for_run_pieces(n_rows, fn):
    def chunk(j, c):
        fn(j * RUN_CHUNK, RUN_CHUNK)
        return c

    lax.fori_loop(0, n_rows // RUN_CHUNK, chunk, 0)
    size = RUN_CHUNK // 2
    while size >= RUN_ALIGN:
        pl.when((n_rows & size) != 0)(functools.partial(fn, (n_rows // (2 * size)) * (2 * size), size))
        size //= 2


def _run_copies(tc8_ref, toff_ref, tg_ref, ps_ref, tile, hbm, stage, sem, to_hbm, wait):
    for e in range(N_EXPERTS):
        idx = tile * N_EXPERTS + e
        loc0 = toff_ref[idx]
        hbm0 = ps_ref[e] + tg_ref[idx]

        def piece(off, size, loc0=loc0, hbm0=hbm0):
            loc = stage.at[pl.ds(pl.multiple_of(loc0 + off, RUN_ALIGN), size)]
            glob = hbm.at[pl.ds(pl.multiple_of(hbm0 + off, RUN_ALIGN), size)]
            cp = pltpu.make_async_copy(loc, glob, sem) if to_hbm else pltpu.make_async_copy(glob, loc, sem)
            if wait:
                cp.wait()
            else:
                cp.start()

        _for_run_pieces(tc8_ref[idx], piece)


def _dispatch_kernel(tc8_ref, toff_ref, tg_ref, ps_ref, pad_ref, nu_ref, ls_ref, u_ref, xs_out, stage, zbuf, sem,
                     *, n_tail):
    i = pl.program_id(0)
    tm = u_ref.shape[0]
    nb = xs_out.shape[0] // MOE_ROWS

    @pl.when(i == 0)
    def _():
        zbuf[...] = jnp.zeros_like(zbuf)

        def zero_copy(blk):
            return pltpu.make_async_copy(zbuf, xs_out.at[pl.ds(pl.multiple_of(blk * MOE_ROWS, MOE_ROWS), MOE_ROWS)],
                                         sem.at[0])

        def guarded(fn):
            for e in range(N_EXPERTS):
                pl.when(pad_ref[e] >= 0)(lambda e=e: fn(zero_copy(pad_ref[e])))
            for k in range(n_tail):
                pl.when(nb - 1 - k >= nu_ref[0])(lambda k=k: fn(zero_copy(nb - 1 - k)))

        guarded(lambda cp: cp.start())
        guarded(lambda cp: cp.wait())

    slots = ls_ref[...]
    riota = lax.broadcasted_iota(I32, (stage.shape[1], tm), 0)
    hit = (riota == slots[0:1]) | (riota == slots[1:2])
    perm = jnp.where(hit, 1.0, 0.0).astype(BF16)
    cur = i % 2
    stage.at[cur][...] = jnp.dot(perm, u_ref[...], preferred_element_type=F32).astype(BF16)

    def runs(tile, buf, wait):
        _run_copies(tc8_ref, toff_ref, tg_ref, ps_ref, tile, xs_out, stage.at[buf], sem.at[buf], to_hbm=True, wait=wait)

    runs(i, cur, wait=False)
    pl.when(i > 0)(lambda: runs(i - 1, 1 - cur, wait=True))
    pl.when(i == pl.num_programs(0) - 1)(lambda: runs(i, cur, wait=True))


def _dispatch(tables, pstart, pad_blk, n_used, ls, u2, P, n_tail):
    N, D = u2.shape
    tm = TM_MOE
    n_pref = len(tables) + 3
    grid_spec = pltpu.PrefetchScalarGridSpec(
        num_scalar_prefetch=n_pref,
        grid=(N // tm,),
        in_specs=[pl.BlockSpec((SUBLANES, tm), lambda i, *_: (0, i)),
                  pl.BlockSpec((tm, D), lambda i, *_: (i, 0))],
        out_specs=pl.BlockSpec(memory_space=pl.ANY),
        scratch_shapes=[pltpu.VMEM((2, MOE_STAGE_ROWS, D), BF16), pltpu.VMEM((MOE_ROWS, D), BF16),
                        pltpu.SemaphoreType.DMA((2,))],
    )
    return pl.pallas_call(
        functools.partial(_dispatch_kernel, n_tail=n_tail),
        grid_spec=grid_spec,
        out_shape=jax.ShapeDtypeStruct((P, D), BF16),
        compiler_params=_cparams(("arbitrary",), 40),
        name="moe_dispatch",
    )(*tables, pstart, pad_blk, n_used, ls, u2)


def _expert_kernel(be_ref, nu_ref, xs_ref, wg_ref, wu_ref, wd_ref, ys_ref, wg_sc, wu_sc, wd_sc):
    i = pl.program_id(0)
    prev = be_ref[jnp.maximum(i - 1, 0)]
    fresh = (i == 0) | (be_ref[i] != prev)

    @pl.when(fresh)
    def _():
        wg_sc[...] = wg_ref[0, 0].astype(BF16)
        wu_sc[...] = wu_ref[0, 0].astype(BF16)
        wd_sc[...] = wd_ref[0, 0].astype(BF16)

    @pl.when(i < nu_ref[0])
    def _():
        xb = xs_ref[...]
        hg = jnp.dot(xb, wg_sc[...], preferred_element_type=F32)
        hu = jnp.dot(xb, wu_sc[...], preferred_element_type=F32)
        h = (hg * _sigmoid(hg)) * hu
        ys_ref[...] = jnp.dot(h.astype(BF16), wd_sc[...], preferred_element_type=F32).astype(BF16)

    @pl.when(i >= nu_ref[0])
    def _():
        ys_ref[...] = jnp.zeros_like(ys_ref)


def _experts(block_e, n_used, xs, w_gate, w_up, w_down, layer):
    P, D = xs.shape
    _, E, _, DE = w_gate.shape
    nb = P // MOE_ROWS
    wmap = lambda i, be, nu: (layer, be[i], 0, 0)
    grid_spec = pltpu.PrefetchScalarGridSpec(
        num_scalar_prefetch=2,
        grid=(nb,),
        in_specs=[pl.BlockSpec((MOE_ROWS, D), lambda i, be, nu: (i, 0)),
                  pl.BlockSpec((1, 1, D, DE), wmap),
                  pl.BlockSpec((1, 1, D, DE), wmap),
                  pl.BlockSpec((1, 1, DE, D), wmap)],
        out_specs=pl.BlockSpec((MOE_ROWS, D), lambda i, be, nu: (i, 0)),
        scratch_shapes=[pltpu.VMEM((D, DE), BF16), pltpu.VMEM((D, DE), BF16), pltpu.VMEM((DE, D), BF16)],
    )
    return pl.pallas_call(
        _expert_kernel,
        grid_spec=grid_spec,
        out_shape=jax.ShapeDtypeStruct((P, D), BF16),
        compiler_params=_cparams(("arbitrary",), 48),
        name="moe_experts",
    )(block_e, n_used, xs, w_gate, w_up, w_down)


def _combine_kernel(tc8_ref, toff_ref, tg_ref, ps_ref, ls_ref, rw_ref, x_ref, gt_ref, fg_ref, ys_ref, o_ref,
                    stage, sem, *, final):
    i = pl.program_id(0)
    tm = x_ref.shape[0]

    def runs(tile, buf, wait):
        _run_copies(tc8_ref, toff_ref, tg_ref, ps_ref, tile, ys_ref, stage.at[buf], sem.at[buf], to_hbm=False, wait=wait)

    @pl.when(i == 0)
    def _():
        stage[...] = jnp.zeros_like(stage)
        runs(0, 0, wait=False)

    cur = i % 2
    pl.when(i + 1 < pl.num_programs(0))(lambda: runs(i + 1, 1 - cur, wait=False))
    runs(i, cur, wait=True)
    rows = stage[cur]
    liota = lax.broadcasted_iota(I32, (tm, stage.shape[1]), 1)
    slots = ls_ref[...]
    w = rw_ref[...]
    pick = jnp.where(liota == slots[:, 0:1], w[:, 0:1], 0.0)
    for k in range(1, TOP_K):
        pick = pick + jnp.where(liota == slots[:, k:k + 1], w[:, k:k + 1], 0.0)
    y = jnp.dot(pick.astype(BF16), rows, preferred_element_type=F32)
    x2 = x_ref[...] + gt_ref[0] * y
    if final:
        x2 = (x2 * lax.rsqrt(jnp.mean(x2 * x2, axis=-1, keepdims=True) + EPS)) * fg_ref[...]
    o_ref[...] = x2


def _combine(tables, pstart, ls_t, rw_t, x1, gt, final_g, ys, T, final):
    N, D = x1.shape
    tm = TM_MOE
    tpb = T // tm
    grid_spec = pltpu.PrefetchScalarGridSpec(
        num_scalar_prefetch=len(tables) + 1,
        grid=(N // tm,),
        in_specs=[pl.BlockSpec((tm, SUBLANES), lambda i, *_: (i, 0)),
                  pl.BlockSpec((tm, SUBLANES), lambda i, *_: (i, 0)),
                  pl.BlockSpec((tm, D), lambda i, *_: (i, 0)),
                  pl.BlockSpec((1, 1, D), lambda i, *_: (i // tpb, 0, 0)),
                  pl.BlockSpec((1, D), lambda i, *_: (0, 0)),
                  pl.BlockSpec(memory_space=pl.ANY)],
        out_specs=pl.BlockSpec((tm, D), lambda i, *_: (i, 0)),
        scratch_shapes=[pltpu.VMEM((2, MOE_STAGE_ROWS, D), BF16), pltpu.SemaphoreType.DMA((2,))],
    )
    return pl.pallas_call(
        functools.partial(_combine_kernel, final=final),
        grid_spec=grid_spec,
        out_shape=jax.ShapeDtypeStruct((N, D), F32),
        compiler_params=_cparams(("arbitrary",), 40),
        name="moe_combine",
    )(*tables, pstart, ls_t, rw_t, x1, gt, final_g, ys)


def _moe_plan(cnt, nb):
    G, S = N_EXPERT_GROUPS, EXPERTS_PER_GROUP
    counts = cnt[:, 0]
    padded = (counts + MOE_ROWS - 1) // MOE_ROWS * MOE_ROWS
    pend = jnp.cumsum(padded).astype(I32)
    pstart = pend - padded
    pad_blk = jnp.where(padded > 0, pend // MOE_ROWS - 1, -1).astype(I32)
    block_start = jnp.arange(nb, dtype=I32) * MOE_ROWS
    block_row = jnp.minimum(jnp.sum((pend[None, :] <= block_start[:, None]).astype(I32), axis=1), G * S - 1)
    block_e = (block_row % G) * S + block_row // G
    n_used = pend[-1:] // MOE_ROWS
    return pstart, pad_blk, block_e.astype(I32), n_used.astype(I32)


def _block_diag(w):
    nb, bw, _ = w.shape
    eye = jnp.eye(nb, dtype=w.dtype)
    return jnp.einsum('nij,nm->nimj', w, eye).reshape(nb * bw, nb * bw)


def kernel(x, c, ada_w, ada_b, norm_mix_g, norm_ffn_g, w_in, w_branch_gate, b_branch_gate,
           na_rpb, lru_conv_w, lru_conv_b, lru_w_r, lru_b_r, lru_w_i, lru_b_i, lru_lambda,
           w_proj_na, w_proj_lru, w_proj_fnet, w_out, w_router, router_bias,
           w_exp_gate, w_exp_up, w_exp_down, final_g):
    B, T, D = x.shape
    L = ada_w.shape[0]
    N = B * T
    E = N_EXPERTS
    G, S = N_EXPERT_GROUPS, EXPERTS_PER_GROUP
    A = N * TOP_K
    assert TM_MERGE == TM_MOE
    max_rows = A + (N // TM_MOE) * E * (RUN_ALIGN - 1) + E * (MOE_ROWS - 1)
    nb = -(-max_rows // MOE_ROWS)
    P = nb * MOE_ROWS

    mod = _ada_mod(c, ada_w, ada_b)
    m1_np, cs_np = _fnet_tables(T)
    m1 = jnp.asarray(m1_np).astype(BF16)
    cs = jnp.asarray(cs_np).astype(BF16)
    tri = (lax.broadcasted_iota(I32, (TM_MERGE, TM_MERGE), 0)
           <= lax.broadcasted_iota(I32, (TM_MERGE, TM_MERGE), 1)).astype(BF16)
    wr = w_router.T.reshape(G, S, D).transpose(1, 0, 2).reshape(E, D)
    rb = router_bias.reshape(G, S).T.reshape(E, 1)
    fg = final_g.reshape(1, D)

    x2 = x.reshape(N, D)
    for l in range(L):
        sh_mix, sc_mix, gt_mix, sh_ffn, sc_ffn, gt_ffn = [m.reshape(B, 1, D) for m in jnp.split(mod[l], 6, axis=-1)]
        wcat = jnp.concatenate([w_in[l], w_branch_gate[l]], axis=1).astype(BF16)
        q, k, v, xr, gy, fr, gates = _mix_in(x2, norm_mix_g[l].reshape(1, D), sh_mix, sc_mix, wcat,
                                             b_branch_gate[l].reshape(1, -1), B, T)
        y_na = _na(q, k, v, _na_bias_table(na_rpb[l]), B, T)

        wg = (0.5 * jnp.stack([jnp.concatenate([_block_diag(lru_w_r[l, d]), _block_diag(lru_w_i[l, d])], axis=1)
                               for d in range(2)])).astype(BF16)
        bgl = 0.5 * jnp.stack([jnp.concatenate([lru_b_r[l, d], lru_b_i[l, d]]) for d in range(2)])[:, None, :]
        af, bf, ab, bb = _lru_gates(xr, lru_conv_w[l], lru_conv_b[l].reshape(1, -1), wg, bgl,
                                    lru_lambda[l][:, None, :], B, T)
        y_lru = _lru_scans(af, bf, ab, bb, gy)
        y_f = _fnet(fr, m1, cs, B, T)

        x1, u2, ls, rw, tc8, toff, tg, cnt = _merge(
            gates, y_na, y_lru, y_f, x2,
            w_proj_na[l].astype(BF16), w_proj_lru[l].astype(BF16),
            w_proj_fnet[l].astype(BF16), w_out[l].astype(BF16),
            gt_mix, norm_ffn_g[l].reshape(1, D), sh_ffn, sc_ffn, wr, rb, tri, B, T)
        tables = [t[:, :, 0].reshape(-1) for t in (tc8, toff, tg)]
        pstart, pad_blk, block_e, n_used = _moe_plan(cnt, nb)
        xs = _dispatch(tables, pstart, pad_blk, n_used, ls, u2, P, nb - A // MOE_ROWS)
        ys = _experts(block_e, n_used, xs, w_exp_gate, w_exp_up, w_exp_down, l)
        x2 = _combine(tables, pstart, ls.T, rw.T, x1, gt_ffn, fg, ys, T, final=(l == L - 1))
    return x2.reshape(B, T, D)
```

```python
import functools

import numpy as np
import jax
import jax.numpy as jnp
from jax import lax
from jax.experimental import pallas as pl
from jax.experimental.pallas import tpu as pltpu

F32 = jnp.float32
BF16 = jnp.bfloat16
I32 = jnp.int32

GRID_W = 64
NA_HEAD_DIM = 64
NA_WIN_ROWS = 8
NA_WIN_COLS = 16
NA_HEADS_PER_TILE = 4
LRU_BLOCKS = 8
LRU_C = 8.0
CONV_W = 4
FNET_GROUP_W = 128
N_EXPERTS = 32
N_EXPERT_GROUPS = 8
EXPERTS_PER_GROUP = 4
TOP_K = 2
EPS = 1e-6
MASK_VALUE = -1e30

V7X_VMEM_BYTES = 64 * 1024 * 1024
LANES = 128
SUBLANES = 8

TM_MIX = 1024
TN_MIX = 1024
MIX_ROW_CHUNK = 512
NA_ROWS_PER_STEP = 8
TM_LRU = 512
T_SCAN = 512
TM_MERGE = 512
TM_MOE = 512
MOE_ROWS = 512
RUN_CHUNK = 64
RUN_ALIGN = 16
MOE_STAGE_ROWS = -(-(TM_MOE * TOP_K + N_EXPERTS * (RUN_ALIGN - 1)) // 256) * 256
MOE_STAGE_TAIL = 256

VMEM_MB = {
    "ada_mod": 40,
    "mix_in": 48,
    "na_attention": 48,
    "lru_gates": 48,
    "lru_scan": 48,
    "fnet": 48,
    "merge_route": 48,
    "moe_dispatch": 40,
    "moe_experts": 48,
    "moe_combine": 40,
}


def _cparams(name, semantics):
    limit = VMEM_MB[name] * 1024 * 1024
    assert limit < V7X_VMEM_BYTES
    return pltpu.CompilerParams(dimension_semantics=semantics, vmem_limit_bytes=limit)


def _split2(a):
    hi = a.astype(BF16)
    lo = (a - hi.astype(F32)).astype(BF16)
    return hi, lo


def _dot_hi(a, b, dims):
    a_hi, a_lo = _split2(a)
    b_hi, b_lo = _split2(b)
    dn = (dims, ((), ()))
    m = a.shape[0]
    both = lax.dot_general(jnp.concatenate([a_hi, a_lo], axis=0), b_hi, dn, preferred_element_type=F32)
    return (both[:m] + both[m:]) + lax.dot_general(a_hi, b_lo, dn, preferred_element_type=F32)


def _sigmoid(x):
    return 0.5 * jnp.tanh(0.5 * x) + 0.5


def _rmsnorm_mod(x, g, sh, sc):
    y = x * lax.rsqrt(jnp.mean(x * x, axis=-1, keepdims=True) + EPS)
    return (y * g) * (1.0 + sc) + sh


def _ada_kernel(c_ref, w_ref, b_ref, o_ref):
    c = c_ref[...]
    ca = c * jax.nn.sigmoid(c)
    o_ref[0] = _dot_hi(ca, w_ref[0], ((1,), (0,))) + b_ref[0]


def _ada_mod(c, ada_w, ada_b):
    L, D, D6 = ada_w.shape
    B = c.shape[0]
    bp = -(-B // SUBLANES) * SUBLANES
    cp = jnp.pad(c, ((0, bp - B), (0, 0)))
    tn = D6 // 4
    out = pl.pallas_call(
        _ada_kernel,
        grid=(L, D6 // tn),
        in_specs=[pl.BlockSpec((bp, D), lambda l, j: (0, 0)),
                  pl.BlockSpec((1, D, tn), lambda l, j: (l, 0, j)),
                  pl.BlockSpec((1, 1, tn), lambda l, j: (l, 0, j))],
        out_specs=pl.BlockSpec((1, bp, tn), lambda l, j: (l, 0, j)),
        out_shape=jax.ShapeDtypeStruct((L, bp, D6), F32),
        compiler_params=_cparams("ada_mod", ("parallel", "parallel")),
        name="ada_mod",
    )(cp, ada_w, ada_b.reshape(L, 1, D6))
    return out[:, :B]


def _mixin_kernel(x_ref, g_ref, sh_ref, sc_ref, w_ref, bg_ref,
                  q_ref, k_ref, v_ref, xr_ref, gy_ref, fr_ref, gate_ref, u_sc):
    j = pl.program_id(1)

    @pl.when(j == 0)
    def _():
        u_sc[...] = _rmsnorm_mod(x_ref[...], g_ref[...], sh_ref[0], sc_ref[0]).astype(BF16)

    tm = u_sc.shape[0]
    W = q_ref.shape[1]
    lo, hi = slice(0, W), slice(W, 2 * W)

    def per_chunk(store):
        for c in range(tm // MIX_ROW_CHUNK):
            rows = slice(c * MIX_ROW_CHUNK, (c + 1) * MIX_ROW_CHUNK)
            store(rows, jnp.dot(u_sc[rows, :], w_ref[...], preferred_element_type=F32))

    @pl.when(j == 0)
    def _():
        def store(rows, acc):
            q_ref[rows, :] = acc[:, lo].astype(BF16)
            k_ref[rows, :] = acc[:, hi].astype(BF16)
        per_chunk(store)

    @pl.when(j == 1)
    def _():
        def store(rows, acc):
            v_ref[rows, :] = acc[:, lo].astype(BF16)
            xr_ref[rows, :] = acc[:, hi]
        per_chunk(store)

    @pl.when(j == 2)
    def _():
        def store(rows, acc):
            gy_ref[rows, :] = jax.nn.gelu(acc[:, lo]).astype(BF16)
            fr_ref[rows, :] = acc[:, hi].astype(BF16)
        per_chunk(store)

    @pl.when(j >= 3)
    def _():
        def store(rows, acc):
            gate_ref[rows, :] = _sigmoid(acc + bg_ref[...]).astype(BF16)
        per_chunk(store)


def _mix_in(x2, g, sh, sc, wcat, bgate, B, T):
    N, D = x2.shape
    tm, tn = TM_MIX, TN_MIX
    tpb = T // tm
    W = tn // 2
    n_proj = 3
    n_gate = bgate.shape[1] // tn
    tok = lambda i, j: (i, 0)
    scan = lambda i, j: (i % tpb, i // tpb)
    per_b = lambda i, j: (i // tpb, 0, 0)
    out_shape = [jax.ShapeDtypeStruct((N, W), BF16)] * 3 + [
        jax.ShapeDtypeStruct((N, W), F32),
        jax.ShapeDtypeStruct((T, B * W), BF16),
        jax.ShapeDtypeStruct((N, W), BF16),
        jax.ShapeDtypeStruct((N, n_gate * tn), BF16)]
    out_specs = [pl.BlockSpec((tm, W), tok)] * 3 + [
        pl.BlockSpec((tm, W), tok),
        pl.BlockSpec((tm, W), scan),
        pl.BlockSpec((tm, W), tok),
        pl.BlockSpec((tm, tn), lambda i, j: (i, jnp.maximum(j - n_proj, 0)))]
    return pl.pallas_call(
        _mixin_kernel,
        grid=(N // tm, n_proj + n_gate),
        in_specs=[pl.BlockSpec((tm, D), tok),
                  pl.BlockSpec((1, D), lambda i, j: (0, 0)),
                  pl.BlockSpec((1, 1, D), per_b),
                  pl.BlockSpec((1, 1, D), per_b),
                  pl.BlockSpec((D, tn), lambda i, j: (0, j)),
                  pl.BlockSpec((1, tn), lambda i, j: (0, jnp.maximum(j - n_proj, 0)))],
        out_specs=out_specs,
        out_shape=out_shape,
        scratch_shapes=[pltpu.VMEM((tm, D), BF16)],
        compiler_params=_cparams("mix_in", ("parallel", "arbitrary")),
        name="mix_in",
    )(x2, g, sh, sc, wcat, bgate)


def _na_kernel(q_ref, kp_ref, kc_ref, kn_ref, vp_ref, vc_ref, vn_ref, bias_ref, o_ref, kb, vb):
    j = pl.program_id(1)
    nj = pl.num_programs(1)
    R = NA_ROWS_PER_STEP
    half = R // 2
    W = GRID_W
    hw = half * W
    tq = R * W
    win = NA_WIN_ROWS * W
    HT = NA_HEADS_PER_TILE
    lt = HT * NA_HEAD_DIM
    kb[0:hw] = kp_ref[...]
    kb[hw:hw + tq] = kc_ref[...]
    kb[hw + tq:2 * hw + tq] = kn_ref[...]
    vb[0:hw] = vp_ref[...]
    vb[hw:hw + tq] = vc_ref[...]
    vb[hw + tq:2 * hw + tq] = vn_ref[...]
    head_of_lane = lax.broadcasted_iota(I32, (W, lt), 1) // NA_HEAD_DIM
    n_tiles = q_ref.shape[1] // lt
    scale = jnp.asarray(NA_HEAD_DIM ** -0.5, BF16)
    for dr in range(R):
        lo = jnp.where(j == 0, max(dr, half), jnp.where(j == nj - 1, min(dr, half), dr))
        var = lo - dr + (half - 1)
        start = pl.multiple_of(lo * W, W)
        for ht in range(n_tiles):
            cs = slice(ht * lt, (ht + 1) * lt)
            qrow = q_ref[dr * W:(dr + 1) * W, cs] * scale
            qbd = jnp.concatenate(
                [jnp.where(head_of_lane == h, qrow, jnp.zeros_like(qrow)) for h in range(HT)], axis=0)
            kwin = kb[pl.ds(start, win), cs]
            s = lax.dot_general(qbd, kwin, (((1,), (1,)), ((), ())), preferred_element_type=F32)
            s = s + bias_ref[var, ht]
            m = jnp.max(s, axis=-1, keepdims=True)
            p = jnp.exp(s - m)
            l = jnp.sum(p, axis=-1, keepdims=True)
            vwin = vb[pl.ds(start, win), cs]
            o = jnp.dot(p.astype(BF16), vwin, preferred_element_type=F32) * (1.0 / l)
            orow = jnp.where(head_of_lane == 0, o[0:W], 0.0)
            for h in range(1, HT):
                orow = orow + jnp.where(head_of_lane == h, o[h * W:(h + 1) * W], 0.0)
            o_ref[dr * W:(dr + 1) * W, cs] = orow.astype(BF16)


def _na_bias_table(rpb):
    H = rpb.shape[0]
    W, KR, KC = GRID_W, NA_WIN_ROWS, NA_WIN_COLS
    w = np.arange(W)
    cstart = np.clip(w - KC // 2, 0, W - KC)
    wk = np.arange(W)
    col_ok = (wk[None, :] >= cstart[:, None]) & (wk[None, :] < cstart[:, None] + KC)
    cb = wk[None, :] - w[:, None] + (KC - 1)
    onehot = ((cb[None] == np.arange(2 * KC - 1)[:, None, None]) & col_ok[None]).astype(np.float32)
    toep = jnp.einsum('hac,cwx->hawx', rpb.astype(F32), jnp.asarray(onehot), precision=lax.Precision.HIGHEST)
    toep = jnp.where(jnp.asarray(col_ok)[None, None], toep, MASK_VALUE)
    tab = jnp.stack([toep[:, v:v + KR] for v in range(KR)])
    tab = tab.transpose(0, 1, 3, 2, 4)
    HT = NA_HEADS_PER_TILE
    return tab.reshape(KR, H // HT, HT * W, KR * W)


def _na(q, k, v, bias, B, T):
    N, C = q.shape
    R = NA_ROWS_PER_STEP
    tq = R * GRID_W
    hw = tq // 2
    nj = T // tq
    nh = T // hw
    qmap = lambda b, j: (b * nj + j, 0)
    pmap = lambda b, j: (b * nh + jnp.maximum(2 * j - 1, 0), 0)
    nmap = lambda b, j: (b * nh + jnp.minimum(2 * j + 2, nh - 1), 0)
    return pl.pallas_call(
        _na_kernel,
        grid=(B, nj),
        in_specs=[pl.BlockSpec((tq, C), qmap),
                  pl.BlockSpec((hw, C), pmap), pl.BlockSpec((tq, C), qmap), pl.BlockSpec((hw, C), nmap),
                  pl.BlockSpec((hw, C), pmap), pl.BlockSpec((tq, C), qmap), pl.BlockSpec((hw, C), nmap),
                  pl.BlockSpec(bias.shape, lambda b, j: (0, 0, 0, 0))],
        out_specs=pl.BlockSpec((tq, C), qmap),
        out_shape=jax.ShapeDtypeStruct((N, C), BF16),
        scratch_shapes=[pltpu.VMEM((2 * tq, C), BF16), pltpu.VMEM((2 * tq, C), BF16)],
        compiler_params=_cparams("na_attention", ("parallel", "parallel")),
        name="na_attention",
    )(q, k, k, k, v, v, v, bias)


def _lru_gate_kernel(x_ref, xp_ref, xn_ref, cw_ref, cb_ref, wg_ref, bg_ref, lam_ref,
                     af_ref, bf_ref, ab_ref, bb_ref, xpad, *, tpb):
    i = pl.program_id(0)
    tm, C = x_ref.shape
    H = SUBLANES
    first = (i % tpb) == 0
    last = (i % tpb) == tpb - 1
    xpad[0:H] = jnp.where(first, 0.0, xp_ref[...])
    xpad[H:H + tm] = x_ref[...]
    xpad[H + tm:2 * H + tm] = jnp.where(last, 0.0, xn_ref[...])
    left = CONV_W // 2
    xc = xpad[H - left:H - left + tm] * cw_ref[0:1]
    for t in range(1, CONV_W):
        xc = xc + xpad[H - left + t:H - left + t + tm] * cw_ref[t:t + 1]
    xc = xc + cb_ref[...]
    xcb = xc.astype(BF16)
    hx = 0.5 * xc
    for d, (a_ref, b_ref) in enumerate(((af_ref, bf_ref), (ab_ref, bb_ref))):
        t = jnp.tanh(jnp.dot(xcb, wg_ref[d], preferred_element_type=F32) + bg_ref[d])
        c = (-0.5 * LRU_C) * jax.nn.softplus(-lam_ref[d])
        log_a = c * t[:, :C] + c
        a = jnp.exp(log_a)
        a_ref[...] = a
        b_ref[...] = jnp.sqrt(1.0 - a * a) * (t[:, C:] * hx + hx)


def _lru_gates(xr, conv_w, conv_b, wg, bg, lam, B, T):
    N, C = xr.shape
    tm = TM_LRU
    tpb = T // tm
    nb8 = N // SUBLANES
    r8 = tm // SUBLANES
    scan = lambda i: (i % tpb, i // tpb)
    const2 = lambda i: (0, 0)
    const3 = lambda i: (0, 0, 0)
    outs = pl.pallas_call(
        functools.partial(_lru_gate_kernel, tpb=tpb),
        grid=(N // tm,),
        in_specs=[pl.BlockSpec((tm, C), lambda i: (i, 0)),
                  pl.BlockSpec((SUBLANES, C), lambda i: (jnp.maximum(i * r8 - 1, 0), 0)),
                  pl.BlockSpec((SUBLANES, C), lambda i: (jnp.minimum((i + 1) * r8, nb8 - 1), 0)),
                  pl.BlockSpec((CONV_W, C), const2),
                  pl.BlockSpec((1, C), const2),
                  pl.BlockSpec((2, C, 2 * C), const3),
                  pl.BlockSpec((2, 1, 2 * C), const3),
                  pl.BlockSpec((2, 1, C), const3)],
        out_specs=[pl.BlockSpec((tm, C), scan)] * 4,
        out_shape=[jax.ShapeDtypeStruct((T, B * C), F32)] * 4,
        scratch_shapes=[pltpu.VMEM((tm + 2 * SUBLANES, C), F32)],
        compiler_params=_cparams("lru_gates", ("parallel",)),
        name="lru_gates",
    )(xr, xr, xr, conv_w, conv_b, wg, bg, lam)
    return outs


def _scan_bwd_kernel(a_ref, b_ref, h_ref, carry):
    @pl.when(pl.program_id(0) == 0)
    def _():
        carry[...] = jnp.zeros_like(carry)

    tc = a_ref.shape[0]

    def body(s, h):
        t = tc - 1 - s
        h = a_ref[t] * h + b_ref[t]
        h_ref[t] = h
        return h

    carry[...] = lax.fori_loop(0, tc, body, carry[...], unroll=8)


def _scan_fwd_kernel(a_ref, b_ref, hb_ref, gy_ref, y_ref, carry):
    @pl.when(pl.program_id(0) == 0)
    def _():
        carry[...] = jnp.zeros_like(carry)

    tc = a_ref.shape[0]

    def body(t, h):
        h = a_ref[t] * h + b_ref[t]
        y_ref[t] = ((h + hb_ref[t]) * gy_ref[t].astype(F32)).astype(BF16)
        return h

    carry[...] = lax.fori_loop(0, tc, body, carry[...], unroll=8)


def _lru_scans(af, bf, ab, bb, gy):
    T, BC = af.shape
    S = BC // LANES
    tc = T_SCAN
    nc = T // tc
    v3 = lambda z: z.reshape(T, S, LANES)
    blk = (tc, S, LANES)
    fwd = lambda c: (c, 0, 0)
    rev = lambda c: (nc - 1 - c, 0, 0)
    hb = pl.pallas_call(
        _scan_bwd_kernel,
        grid=(nc,),
        in_specs=[pl.BlockSpec(blk, rev), pl.BlockSpec(blk, rev)],
        out_specs=pl.BlockSpec(blk, rev),
        out_shape=jax.ShapeDtypeStruct((T, S, LANES), F32),
        scratch_shapes=[pltpu.VMEM((S, LANES), F32)],
        compiler_params=_cparams("lru_scan", ("arbitrary",)),
        name="lru_scan_bwd",
    )(v3(ab), v3(bb))
    y = pl.pallas_call(
        _scan_fwd_kernel,
        grid=(nc,),
        in_specs=[pl.BlockSpec(blk, fwd)] * 4,
        out_specs=pl.BlockSpec(blk, fwd),
        out_shape=jax.ShapeDtypeStruct((T, S, LANES), BF16),
        scratch_shapes=[pltpu.VMEM((S, LANES), F32)],
        compiler_params=_cparams("lru_scan", ("arbitrary",)),
        name="lru_scan_fwd",
    )(v3(af), v3(bf), hb, v3(gy))
    return y.reshape(T, BC)


def _fnet_tables(T):
    T2 = FNET_GROUP_W
    T1 = T // T2
    k1 = np.arange(T1)[:, None]
    t1 = np.arange(T1)[None, :]
    stage1 = np.zeros((T2, 2 * T1, 2 * T1), np.float32)
    for t2 in range(T2):
        ph = 2.0 * np.pi * ((k1 * (T2 * t1 + t2)) % T) / T
        c, s = np.cos(ph), np.sin(ph)
        stage1[t2] = np.block([[c, s], [-s, c]])
    n = np.arange(T2)
    ph2 = 2.0 * np.pi * ((n[:, None] * n[None, :]) % T2) / T2
    c2, s2 = np.cos(ph2).astype(np.float32), np.sin(ph2).astype(np.float32)
    return stage1, np.concatenate([c2, s2], axis=1)


def _fnet_kernel(x_ref, m1_ref, cs_ref, o_ref, zr, zi, ar, ai, y8, yo, *, T1):
    T2 = FNET_GROUP_W
    T = T1 * T2
    S = SUBLANES
    cs = cs_ref[...]
    for t1 in range(T1):
        z = jnp.dot(x_ref[t1 * T2:(t1 + 1) * T2, :], cs, preferred_element_type=F32)
        zr[t1] = z[:, :T2]
        zi[t1] = -z[:, T2:]
    for g in range(T2 // S):
        grp = slice(g * S, (g + 1) * S)
        cr = jnp.swapaxes(zr[:, grp, :], 0, 1)
        ci = jnp.swapaxes(zi[:, grp, :], 0, 1)
        for j in range(S):
            t2 = g * S + j
            zin = jnp.concatenate([cr[j], ci[j]], axis=0).astype(BF16)
            a = jnp.dot(m1_ref[t2], zin, preferred_element_type=F32)
            ar[t2] = a[:T1]
            ai[t2] = a[T1:]
    inv = 1.0 / np.sqrt(float(T) * T2)
    for g in range(T1 // S):
        grp = slice(g * S, (g + 1) * S)
        cr = jnp.swapaxes(ar[:, grp, :], 0, 1)
        ci = jnp.swapaxes(ai[:, grp, :], 0, 1)
        for j in range(S):
            ain = jnp.concatenate([cr[j], ci[j]], axis=0).astype(BF16)
            y8[j] = jnp.dot(cs, ain, preferred_element_type=F32) * inv
        yo[:, grp, :] = jnp.swapaxes(y8[...], 0, 1)
    o_ref[...] = yo[...].reshape(T, T2).astype(BF16)


def _fnet(fr, m1, cs, B, T):
    N, C = fr.shape
    G = C // FNET_GROUP_W
    T1 = T // FNET_GROUP_W
    blk = pl.BlockSpec((T, FNET_GROUP_W), lambda b, g: (b, g))
    return pl.pallas_call(
        functools.partial(_fnet_kernel, T1=T1),
        grid=(B, G),
        in_specs=[blk,
                  pl.BlockSpec(m1.shape, lambda b, g: (0, 0, 0)),
                  pl.BlockSpec(cs.shape, lambda b, g: (0, 0))],
        out_specs=blk,
        out_shape=jax.ShapeDtypeStruct((N, C), BF16),
        scratch_shapes=[pltpu.VMEM((T1, FNET_GROUP_W, FNET_GROUP_W), F32)] * 2
        + [pltpu.VMEM((FNET_GROUP_W, T1, FNET_GROUP_W), F32)] * 2
        + [pltpu.VMEM((SUBLANES, FNET_GROUP_W, FNET_GROUP_W), F32),
           pltpu.VMEM((FNET_GROUP_W, T1, FNET_GROUP_W), F32)],
        compiler_params=_cparams("fnet", ("parallel", "parallel")),
        name="fnet",
    )(fr, m1, cs)


def _merge_kernel(gate_ref, yna_ref, ylru_ref, yf_ref, x_ref, wna_ref, wlru_ref, wf_ref, wout_ref,
                  gt_ref, g_ref, sh_ref, sc_ref, wr_ref, rb_ref, tri_ref,
                  x1_ref, u2_ref, ls_ref, rw_ref, tc8_ref, toff_ref, tg_ref, cnt_ref, cnt_sc):
    i = pl.program_id(0)

    @pl.when(i == 0)
    def _():
        cnt_sc[...] = jnp.zeros_like(cnt_sc)

    tm, D = x_ref.shape
    gates = gate_ref[...]
    merged = gates[:, 0:D].astype(F32) * jnp.dot(yna_ref[...], wna_ref[...], preferred_element_type=F32)
    merged = merged + gates[:, D:2 * D].astype(F32) * jnp.dot(ylru_ref[...], wlru_ref[...], preferred_element_type=F32)
    merged = merged + gates[:, 2 * D:3 * D].astype(F32) * jnp.dot(yf_ref[...], wf_ref[...], preferred_element_type=F32)
    out = jnp.dot(merged.astype(BF16), wout_ref[...], preferred_element_type=F32)
    x1 = x_ref[...] + gt_ref[0] * out
    x1_ref[...] = x1
    u2 = _rmsnorm_mod(x1, g_ref[...], sh_ref[0], sc_ref[0])
    u2_ref[...] = u2.astype(BF16)

    G, S = N_EXPERT_GROUPS, EXPERTS_PER_GROUP
    logits = _dot_hi(wr_ref[...], u2, ((1,), (1,)))
    score = jax.nn.sigmoid(logits)
    sel = score + rb_ref[...]
    sel_s = [sel[s * G:(s + 1) * G] for s in range(S)]
    sc_s = [score[s * G:(s + 1) * G] for s in range(S)]
    gscore = None
    for a in range(S):
        for b in range(a + 1, S):
            pair = sel_s[a] + sel_s[b]
            gscore = pair if gscore is None else jnp.maximum(gscore, pair)
    giota = lax.broadcasted_iota(I32, (G, tm), 0)
    gmax = jnp.max(gscore, axis=0, keepdims=True)
    gidx = jnp.min(jnp.where(gscore == gmax, giota, G), axis=0, keepdims=True)
    gm = giota == gidx
    cand = [jnp.sum(jnp.where(gm, sel_s[s], 0.0), axis=0, keepdims=True) for s in range(S)]
    raw = [jnp.sum(jnp.where(gm, sc_s[s], 0.0), axis=0, keepdims=True) for s in range(S)]

    def first_argmax(vals):
        m = vals[0]
        for s in range(1, S):
            m = jnp.maximum(m, vals[s])
        idx = jnp.full(m.shape, S - 1, I32)
        for s in range(S - 2, -1, -1):
            idx = jnp.where(vals[s] == m, s, idx)
        return idx

    l0 = first_argmax(cand)
    l1 = first_argmax([jnp.where(l0 == s, -jnp.inf, cand[s]) for s in range(S)])

    def pick(vals, idx):
        out = vals[S - 1]
        for s in range(S - 2, -1, -1):
            out = jnp.where(idx == s, vals[s], out)
        return out

    w0 = pick(raw, l0)
    w1 = pick(raw, l1)
    wsum = w0 + w1
    w0 = w0 / wsum
    w1 = w1 / wsum

    E = G * S
    eiota = lax.broadcasted_iota(I32, (E, tm), 0)
    oh0 = eiota == (l0 * G + gidx)
    oh1 = eiota == (l1 * G + gidx)
    oh = jnp.where(oh0 | oh1, 1.0, 0.0)
    incl = jnp.dot(oh.astype(BF16), tri_ref[...], preferred_element_type=F32)
    cnt = jnp.sum(oh, axis=1, keepdims=True)
    c8 = jnp.floor((cnt + (RUN_ALIGN - 1)) * (1.0 / RUN_ALIGN)) * RUN_ALIGN
    c8b = jnp.broadcast_to(c8, (E, LANES))
    below = (lax.broadcasted_iota(I32, (E, E), 1) < lax.broadcasted_iota(I32, (E, E), 0)).astype(BF16)
    off = jnp.dot(below, c8b.astype(BF16), preferred_element_type=F32)
    local = incl - oh + off[:, 0:1]
    s0 = jnp.sum(jnp.where(oh0, local, 0.0), axis=0, keepdims=True).astype(I32)
    s1 = jnp.sum(jnp.where(oh1, local, 0.0), axis=0, keepdims=True).astype(I32)
    tc8_ref[0] = c8b.astype(I32)
    toff_ref[0] = off.astype(I32)
    tg_ref[0] = cnt_sc[...].astype(I32)
    cnt_sc[...] = cnt_sc[...] + c8b
    cnt_ref[...] = cnt_sc[...].astype(I32)

    row = lax.broadcasted_iota(I32, (SUBLANES, tm), 0)
    ls_ref[...] = jnp.where(row == 0, s0, jnp.where(row == 1, s1, 0))
    rw_ref[...] = jnp.where(row == 0, w0, jnp.where(row == 1, w1, 0.0))


def _merge(gates, y_na, y_lru, y_f, x2, wna, wlru, wf, wout, gt, g, sh, sc, wr, rb, tri, B, T):
    N, D = x2.shape
    C = y_na.shape[1]
    tm = TM_MERGE
    tpb = T // tm
    E = wr.shape[0]
    tok = lambda i: (i, 0)
    scan = lambda i: (i % tpb, i // tpb)
    per_b = lambda i: (i // tpb, 0, 0)
    const2 = lambda i: (0, 0)
    return pl.pallas_call(
        _merge_kernel,
        grid=(N // tm,),
        in_specs=[pl.BlockSpec((tm, 3 * D), tok),
                  pl.BlockSpec((tm, C), tok),
                  pl.BlockSpec((tm, C), scan),
                  pl.BlockSpec((tm, C), tok),
                  pl.BlockSpec((tm, D), tok),
                  pl.BlockSpec((C, D), const2), pl.BlockSpec((C, D), const2), pl.BlockSpec((C, D), const2),
                  pl.BlockSpec((D, D), const2),
                  pl.BlockSpec((1, 1, D), per_b),
                  pl.BlockSpec((1, D), const2),
                  pl.BlockSpec((1, 1, D), per_b),
                  pl.BlockSpec((1, 1, D), per_b),
                  pl.BlockSpec((E, D), const2),
                  pl.BlockSpec((E, 1), const2),
                  pl.BlockSpec((tm, tm), const2)],
        out_specs=[pl.BlockSpec((tm, D), tok),
                   pl.BlockSpec((tm, D), tok),
                   pl.BlockSpec((SUBLANES, tm), lambda i: (0, i)),
                   pl.BlockSpec((SUBLANES, tm), lambda i: (0, i)),
                   pl.BlockSpec((1, E, LANES), lambda i: (i, 0, 0)),
                   pl.BlockSpec((1, E, LANES), lambda i: (i, 0, 0)),
                   pl.BlockSpec((1, E, LANES), lambda i: (i, 0, 0)),
                   pl.BlockSpec((E, LANES), const2)],
        out_shape=[jax.ShapeDtypeStruct((N, D), F32),
                   jax.ShapeDtypeStruct((N, D), BF16),
                   jax.ShapeDtypeStruct((SUBLANES, N), I32),
                   jax.ShapeDtypeStruct((SUBLANES, N), F32),
                   jax.ShapeDtypeStruct((N // tm, E, LANES), I32),
                   jax.ShapeDtypeStruct((N // tm, E, LANES), I32),
                   jax.ShapeDtypeStruct((N // tm, E, LANES), I32),
                   jax.ShapeDtypeStruct((E, LANES), I32)],
        scratch_shapes=[pltpu.VMEM((E, LANES), F32)],
        compiler_params=_cparams("merge_route", ("arbitrary",)),
        name="merge_route",
    )(gates, y_na, y_lru, y_f, x2, wna, wlru, wf, wout, gt, g, sh, sc, wr, rb, tri)


def _for_run_pieces(n_rows, fn):
    def chunk(j, c):
        fn(j * RUN_CHUNK, RUN_CHUNK)
        return c

    lax.fori_loop(0, n_rows // RUN_CHUNK, chunk, 0)
    size = RUN_CHUNK // 2
    while size >= RUN_ALIGN:
        pl.when((n_rows & size) != 0)(functools.partial(fn, (n_rows // (2 * size)) * (2 * size), size))
        size //= 2


def _run_copies(tc8_ref, toff_ref, tg_ref, ps_ref, tile, hbm, stage, sem, to_hbm, wait):
    for e in range(N_EXPERTS):
        idx = tile * N_EXPERTS + e
        loc0 = toff_ref[idx]
        hbm0 = ps_ref[e] + tg_ref[idx]

        def piece(off, size, loc0=loc0, hbm0=hbm0):
            loc = stage.at[pl.ds(pl.multiple_of(loc0 + off, RUN_ALIGN), size)]
            glob = hbm.at[pl.ds(pl.multiple_of(hbm0 + off, RUN_ALIGN), size)]
            cp = pltpu.make_async_copy(loc, glob, sem) if to_hbm else pltpu.make_async_copy(glob, loc, sem)
            if wait:
                cp.wait()
            else:
                cp.start()

        _for_run_pieces(tc8_ref[idx], piece)


def _dispatch_kernel(tc8_ref, toff_ref, tg_ref, ps_ref, pad_ref, nu_ref, ls_ref, u_ref, xs_out, stage, zbuf, sem,
                     *, n_tail):
    i = pl.program_id(0)
    tm = u_ref.shape[0]
    nb = xs_out.shape[0] // MOE_ROWS

    @pl.when(i == 0)
    def _():
        zbuf[...] = jnp.zeros_like(zbuf)

        def zero_copy(blk):
            return pltpu.make_async_copy(zbuf, xs_out.at[pl.ds(pl.multiple_of(blk * MOE_ROWS, MOE_ROWS), MOE_ROWS)],
                                         sem.at[0])

        def guarded(fn):
            for e in range(N_EXPERTS):
                pl.when(pad_ref[e] >= 0)(lambda e=e: fn(zero_copy(pad_ref[e])))
            for k in range(n_tail):
                pl.when(nb - 1 - k >= nu_ref[0])(lambda k=k: fn(zero_copy(nb - 1 - k)))

        guarded(lambda cp: cp.start())
        guarded(lambda cp: cp.wait())

    slots = ls_ref[...]
    cur = i % 2

    def sort_slots(lo, hi):
        riota = lax.broadcasted_iota(I32, (hi - lo, tm), 0) + lo
        hit = (riota == slots[0:1]) | (riota == slots[1:2])
        perm = jnp.where(hit, 1.0, 0.0).astype(BF16)
        stage[cur, lo:hi, :] = jnp.dot(perm, u_ref[...], preferred_element_type=F32).astype(BF16)

    n_slots = stage.shape[1]
    used = toff_ref[i * N_EXPERTS + N_EXPERTS - 1] + tc8_ref[i * N_EXPERTS + N_EXPERTS - 1]
    sort_slots(0, n_slots - MOE_STAGE_TAIL)
    pl.when(used > n_slots - MOE_STAGE_TAIL)(lambda: sort_slots(n_slots - MOE_STAGE_TAIL, n_slots))

    def runs(tile, buf, wait):
        _run_copies(tc8_ref, toff_ref, tg_ref, ps_ref, tile, xs_out, stage.at[buf], sem.at[buf], to_hbm=True, wait=wait)

    runs(i, cur, wait=False)
    pl.when(i > 0)(lambda: runs(i - 1, 1 - cur, wait=True))
    pl.when(i == pl.num_programs(0) - 1)(lambda: runs(i, cur, wait=True))


def _dispatch(tables, pstart, pad_blk, n_used, ls, u2, P, n_tail):
    N, D = u2.shape
    tm = TM_MOE
    n_pref = len(tables) + 3
    grid_spec = pltpu.PrefetchScalarGridSpec(
        num_scalar_prefetch=n_pref,
        grid=(N // tm,),
        in_specs=[pl.BlockSpec((SUBLANES, tm), lambda i, *_: (0, i)),
                  pl.BlockSpec((tm, D), lambda i, *_: (i, 0))],
        out_specs=pl.BlockSpec(memory_space=pl.ANY),
        scratch_shapes=[pltpu.VMEM((2, MOE_STAGE_ROWS, D), BF16), pltpu.VMEM((MOE_ROWS, D), BF16),
                        pltpu.SemaphoreType.DMA((2,))],
    )
    return pl.pallas_call(
        functools.partial(_dispatch_kernel, n_tail=n_tail),
        grid_spec=grid_spec,
        out_shape=jax.ShapeDtypeStruct((P, D), BF16),
        compiler_params=_cparams("moe_dispatch", ("arbitrary",)),
        name="moe_dispatch",
    )(*tables, pstart, pad_blk, n_used, ls, u2)


def _expert_kernel(be_ref, bv_ref, xs_ref, wg_ref, wu_ref, wd_ref, ys_ref, wg_sc, wu_sc, wd_sc):
    i = pl.program_id(0)
    prev = be_ref[jnp.maximum(i - 1, 0)]
    fresh = (i == 0) | (be_ref[i] != prev)

    @pl.when(fresh)
    def _():
        wg_sc[...] = wg_ref[0, 0].astype(BF16)
        wu_sc[...] = wu_ref[0, 0].astype(BF16)
        wd_sc[...] = wd_ref[0, 0].astype(BF16)

    def ffn(n_rows):
        xb = xs_ref[0:n_rows, :]
        hg = jnp.dot(xb, wg_sc[...], preferred_element_type=F32)
        hu = jnp.dot(xb, wu_sc[...], preferred_element_type=F32)
        h = (hg * _sigmoid(hg)) * hu
        ys_ref[0:n_rows, :] = jnp.dot(h.astype(BF16), wd_sc[...], preferred_element_type=F32).astype(BF16)
        if n_rows < MOE_ROWS:
            ys_ref[n_rows:, :] = jnp.zeros((MOE_ROWS - n_rows, ys_ref.shape[1]), ys_ref.dtype)

    occupied = bv_ref[i]
    half = MOE_ROWS // 2
    pl.when(occupied > half)(lambda: ffn(MOE_ROWS))
    pl.when((occupied > 0) & (occupied <= half))(lambda: ffn(half))

    @pl.when(occupied == 0)
    def _():
        ys_ref[...] = jnp.zeros_like(ys_ref)


def _experts(block_e, block_rows, xs, w_gate, w_up, w_down, layer):
    P, D = xs.shape
    _, E, _, DE = w_gate.shape
    nb = P // MOE_ROWS
    wmap = lambda i, be, nu: (layer, be[i], 0, 0)
    grid_spec = pltpu.PrefetchScalarGridSpec(
        num_scalar_prefetch=2,
        grid=(nb,),
        in_specs=[pl.BlockSpec((MOE_ROWS, D), lambda i, be, nu: (i, 0)),
                  pl.BlockSpec((1, 1, D, DE), wmap),
                  pl.BlockSpec((1, 1, D, DE), wmap),
                  pl.BlockSpec((1, 1, DE, D), wmap)],
        out_specs=pl.BlockSpec((MOE_ROWS, D), lambda i, be, nu: (i, 0)),
        scratch_shapes=[pltpu.VMEM((D, DE), BF16), pltpu.VMEM((D, DE), BF16), pltpu.VMEM((DE, D), BF16)],
    )
    return pl.pallas_call(
        _expert_kernel,
        grid_spec=grid_spec,
        out_shape=jax.ShapeDtypeStruct((P, D), BF16),
        compiler_params=_cparams("moe_experts", ("arbitrary",)),
        name="moe_experts",
    )(block_e, block_rows, xs, w_gate, w_up, w_down)


def _combine_kernel(tc8_ref, toff_ref, tg_ref, ps_ref, ls_ref, rw_ref, x_ref, gt_ref, fg_ref, ys_ref, o_ref,
                    stage, sem, *, final):
    i = pl.program_id(0)
    tm = x_ref.shape[0]

    def runs(tile, buf, wait):
        _run_copies(tc8_ref, toff_ref, tg_ref, ps_ref, tile, ys_ref, stage.at[buf], sem.at[buf], to_hbm=False, wait=wait)

    @pl.when(i == 0)
    def _():
        stage[...] = jnp.zeros_like(stage)
        runs(0, 0, wait=False)

    cur = i % 2
    pl.when(i + 1 < pl.num_programs(0))(lambda: runs(i + 1, 1 - cur, wait=False))
    runs(i, cur, wait=True)
    slots = ls_ref[...]
    w = rw_ref[...]

    def gather_slots(lo, hi):
        liota = lax.broadcasted_iota(I32, (tm, hi - lo), 1) + lo
        pick = jnp.where(liota == slots[:, 0:1], w[:, 0:1], 0.0)
        for k in range(1, TOP_K):
            pick = pick + jnp.where(liota == slots[:, k:k + 1], w[:, k:k + 1], 0.0)
        return jnp.dot(pick.astype(BF16), stage[cur, lo:hi, :], preferred_element_type=F32)

    n_slots = stage.shape[1]
    used = toff_ref[i * N_EXPERTS + N_EXPERTS - 1] + tc8_ref[i * N_EXPERTS + N_EXPERTS - 1]
    y = gather_slots(0, n_slots - MOE_STAGE_TAIL)
    y = y + lax.cond(used > n_slots - MOE_STAGE_TAIL,
                     lambda: gather_slots(n_slots - MOE_STAGE_TAIL, n_slots),
                     lambda: jnp.zeros_like(y))
    x2 = x_ref[...] + gt_ref[0] * y
    if final:
        x2 = (x2 * lax.rsqrt(jnp.mean(x2 * x2, axis=-1, keepdims=True) + EPS)) * fg_ref[...]
    o_ref[...] = x2


def _combine(tables, pstart, ls_t, rw_t, x1, gt, final_g, ys, T, final):
    N, D = x1.shape
    tm = TM_MOE
    tpb = T // tm
    grid_spec = pltpu.PrefetchScalarGridSpec(
        num_scalar_prefetch=len(tables) + 1,
        grid=(N // tm,),
        in_specs=[pl.BlockSpec((tm, SUBLANES), lambda i, *_: (i, 0)),
                  pl.BlockSpec((tm, SUBLANES), lambda i, *_: (i, 0)),
                  pl.BlockSpec((tm, D), lambda i, *_: (i, 0)),
                  pl.BlockSpec((1, 1, D), lambda i, *_: (i // tpb, 0, 0)),
                  pl.BlockSpec((1, D), lambda i, *_: (0, 0)),
                  pl.BlockSpec(memory_space=pl.ANY)],
        out_specs=pl.BlockSpec((tm, D), lambda i, *_: (i, 0)),
        scratch_shapes=[pltpu.VMEM((2, MOE_STAGE_ROWS, D), BF16), pltpu.SemaphoreType.DMA((2,))],
    )
    return pl.pallas_call(
        functools.partial(_combine_kernel, final=final),
        grid_spec=grid_spec,
        out_shape=jax.ShapeDtypeStruct((N, D), F32),
        compiler_params=_cparams("moe_combine", ("arbitrary",)),
        name="moe_combine",
    )(*tables, pstart, ls_t, rw_t, x1, gt, final_g, ys)


def _moe_plan(cnt, nb):
    G, S = N_EXPERT_GROUPS, EXPERTS_PER_GROUP
    counts = cnt[:, 0]
    padded = (counts + MOE_ROWS - 1) // MOE_ROWS * MOE_ROWS
    pend = jnp.cumsum(padded).astype(I32)
    pstart = pend - padded
    pad_blk = jnp.where(padded > 0, pend // MOE_ROWS - 1, -1).astype(I32)
    block_start = jnp.arange(nb, dtype=I32) * MOE_ROWS
    block_row = jnp.minimum(jnp.sum((pend[None, :] <= block_start[:, None]).astype(I32), axis=1), G * S - 1)
    block_e = (block_row % G) * S + block_row // G
    block_rows = jnp.clip((pstart + counts)[block_row] - block_start, 0, MOE_ROWS)
    n_used = pend[-1:] // MOE_ROWS
    return pstart, pad_blk, block_e.astype(I32), block_rows.astype(I32), n_used.astype(I32)


def _block_diag(w):
    nb, bw, _ = w.shape
    eye = jnp.eye(nb, dtype=w.dtype)
    return jnp.einsum('nij,nm->nimj', w, eye).reshape(nb * bw, nb * bw)


def kernel(x, c, ada_w, ada_b, norm_mix_g, norm_ffn_g, w_in, w_branch_gate, b_branch_gate,
           na_rpb, lru_conv_w, lru_conv_b, lru_w_r, lru_b_r, lru_w_i, lru_b_i, lru_lambda,
           w_proj_na, w_proj_lru, w_proj_fnet, w_out, w_router, router_bias,
           w_exp_gate, w_exp_up, w_exp_down, final_g):
    B, T, D = x.shape
    L = ada_w.shape[0]
    N = B * T
    E = N_EXPERTS
    G, S = N_EXPERT_GROUPS, EXPERTS_PER_GROUP
    A = N * TOP_K
    assert TM_MERGE == TM_MOE
    max_rows = A + (N // TM_MOE) * E * (RUN_ALIGN - 1) + E * (MOE_ROWS - 1)
    nb = -(-max_rows // MOE_ROWS)
    P = nb * MOE_ROWS

    mod = _ada_mod(c, ada_w, ada_b)
    m1_np, cs_np = _fnet_tables(T)
    m1 = jnp.asarray(m1_np).astype(BF16)
    cs = jnp.asarray(cs_np).astype(BF16)
    tri = (lax.broadcasted_iota(I32, (TM_MERGE, TM_MERGE), 0)
           <= lax.broadcasted_iota(I32, (TM_MERGE, TM_MERGE), 1)).astype(BF16)
    wr = w_router.T.reshape(G, S, D).transpose(1, 0, 2).reshape(E, D)
    rb = router_bias.reshape(G, S).T.reshape(E, 1)
    fg = final_g.reshape(1, D)

    x2 = x.reshape(N, D)
    for l in range(L):
        sh_mix, sc_mix, gt_mix, sh_ffn, sc_ffn, gt_ffn = [m.reshape(B, 1, D) for m in jnp.split(mod[l], 6, axis=-1)]
        wcat = jnp.concatenate([w_in[l], w_branch_gate[l]], axis=1).astype(BF16)
        q, k, v, xr, gy, fr, gates = _mix_in(x2, norm_mix_g[l].reshape(1, D), sh_mix, sc_mix, wcat,
                                             b_branch_gate[l].reshape(1, -1), B, T)
        y_na = _na(q, k, v, _na_bias_table(na_rpb[l]), B, T)

        wg = (0.5 * jnp.stack([jnp.concatenate([_block_diag(lru_w_r[l, d]), _block_diag(lru_w_i[l, d])], axis=1)
                               for d in range(2)])).astype(BF16)
        bgl = 0.5 * jnp.stack([jnp.concatenate([lru_b_r[l, d], lru_b_i[l, d]]) for d in range(2)])[:, None, :]
        af, bf, ab, bb = _lru_gates(xr, lru_conv_w[l], lru_conv_b[l].reshape(1, -1), wg, bgl,
                                    lru_lambda[l][:, None, :], B, T)
        y_lru = _lru_scans(af, bf, ab, bb, gy)
        y_f = _fnet(fr, m1, cs, B, T)

        x1, u2, ls, rw, tc8, toff, tg, cnt = _merge(
            gates, y_na, y_lru, y_f, x2,
            w_proj_na[l].astype(BF16), w_proj_lru[l].astype(BF16),
            w_proj_fnet[l].astype(BF16), w_out[l].astype(BF16),
            gt_mix, norm_ffn_g[l].reshape(1, D), sh_ffn, sc_ffn, wr, rb, tri, B, T)
        tables = [t[:, :, 0].reshape(-1) for t in (tc8, toff, tg)]
        pstart, pad_blk, block_e, block_rows, n_used = _moe_plan(cnt, nb)
        xs = _dispatch(tables, pstart, pad_blk, n_used, ls, u2, P, nb - A // MOE_ROWS)
        ys = _experts(block_e, block_rows, xs, w_exp_gate, w_exp_up, w_exp_down, l)
        x2 = _combine(tables, pstart, ls.T, rw.T, x1, gt_ffn, fg, ys, T, final=(l == L - 1))
    return x2.reshape(B, T, D)
```

```python
import functools

import numpy as np
import jax
import jax.numpy as jnp
from jax import lax
from jax.experimental import pallas as pl
from jax.experimental.pallas import tpu as pltpu

F32 = jnp.float32
BF16 = jnp.bfloat16
I32 = jnp.int32

GRID_W = 64
NA_HEAD_DIM = 64
NA_WIN_ROWS = 8
NA_WIN_COLS = 16
NA_HEADS_PER_TILE = 4
LRU_BLOCKS = 8
LRU_C = 8.0
CONV_W = 4
FNET_GROUP_W = 128
N_EXPERTS = 32
N_EXPERT_GROUPS = 8
EXPERTS_PER_GROUP = 4
TOP_K = 2
EPS = 1e-6
MASK_VALUE = -1e30

V7X_VMEM_BYTES = 64 * 1024 * 1024
LANES = 128
SUBLANES = 8

TM_MIX = 1024
TN_MIX = 1024
MIX_ROW_CHUNK = 512
NA_ROWS_PER_STEP = 8
TM_LRU = 1024
T_SCAN = 512
TM_MERGE = 512
TM_MOE = 512
MOE_ROWS = 512
RUN_CHUNK = 64
RUN_ALIGN = 16
MOE_STAGE_ROWS = -(-(TM_MOE * TOP_K + N_EXPERTS * (RUN_ALIGN - 1)) // 256) * 256
MOE_STAGE_TAIL = 256

VMEM_MB = {
    "ada_mod": 40,
    "mix_in": 48,
    "na_attention": 48,
    "lru_gates": 48,
    "lru_scan": 48,
    "fnet": 48,
    "merge_route": 48,
    "moe_dispatch": 40,
    "moe_experts": 48,
    "moe_combine": 40,
}


def _cparams(name, semantics):
    limit = VMEM_MB[name] * 1024 * 1024
    assert limit < V7X_VMEM_BYTES
    return pltpu.CompilerParams(dimension_semantics=semantics, vmem_limit_bytes=limit)


def _split2(a):
    hi = a.astype(BF16)
    lo = (a - hi.astype(F32)).astype(BF16)
    return hi, lo


def _dot_hi(a, b, dims):
    a_hi, a_lo = _split2(a)
    b_hi, b_lo = _split2(b)
    dn = (dims, ((), ()))
    m = a.shape[0]
    both = lax.dot_general(jnp.concatenate([a_hi, a_lo], axis=0), b_hi, dn, preferred_element_type=F32)
    return (both[:m] + both[m:]) + lax.dot_general(a_hi, b_lo, dn, preferred_element_type=F32)


def _sigmoid(x):
    return 0.5 * jnp.tanh(0.5 * x) + 0.5


def _rmsnorm_mod(x, g, sh, sc):
    y = x * lax.rsqrt(jnp.mean(x * x, axis=-1, keepdims=True) + EPS)
    return (y * g) * (1.0 + sc) + sh


def _ada_kernel(c_ref, w_ref, b_ref, o_ref):
    c = c_ref[...]
    ca = c * jax.nn.sigmoid(c)
    o_ref[0] = _dot_hi(ca, w_ref[0], ((1,), (0,))) + b_ref[0]


def _ada_mod(c, ada_w, ada_b):
    L, D, D6 = ada_w.shape
    B = c.shape[0]
    bp = -(-B // SUBLANES) * SUBLANES
    cp = jnp.pad(c, ((0, bp - B), (0, 0)))
    tn = D6 // 4
    out = pl.pallas_call(
        _ada_kernel,
        grid=(L, D6 // tn),
        in_specs=[pl.BlockSpec((bp, D), lambda l, j: (0, 0)),
                  pl.BlockSpec((1, D, tn), lambda l, j: (l, 0, j)),
                  pl.BlockSpec((1, 1, tn), lambda l, j: (l, 0, j))],
        out_specs=pl.BlockSpec((1, bp, tn), lambda l, j: (l, 0, j)),
        out_shape=jax.ShapeDtypeStruct((L, bp, D6), F32),
        compiler_params=_cparams("ada_mod", ("parallel", "parallel")),
        name="ada_mod",
    )(cp, ada_w, ada_b.reshape(L, 1, D6))
    return out[:, :B]


def _mixin_kernel(x_ref, g_ref, sh_ref, sc_ref, w_ref, bg_ref,
                  q_ref, k_ref, v_ref, xr_ref, gy_ref, fr_ref, gate_ref, u_sc):
    j = pl.program_id(1)

    @pl.when(j == 0)
    def _():
        u_sc[...] = _rmsnorm_mod(x_ref[...], g_ref[...], sh_ref[0], sc_ref[0]).astype(BF16)

    tm = u_sc.shape[0]
    W = q_ref.shape[1]
    lo, hi = slice(0, W), slice(W, 2 * W)

    def per_chunk(store):
        for c in range(tm // MIX_ROW_CHUNK):
            rows = slice(c * MIX_ROW_CHUNK, (c + 1) * MIX_ROW_CHUNK)
            store(rows, jnp.dot(u_sc[rows, :], w_ref[...], preferred_element_type=F32))

    @pl.when(j == 0)
    def _():
        def store(rows, acc):
            q_ref[rows, :] = acc[:, lo].astype(BF16)
            k_ref[rows, :] = acc[:, hi].astype(BF16)
        per_chunk(store)

    @pl.when(j == 1)
    def _():
        def store(rows, acc):
            v_ref[rows, :] = acc[:, lo].astype(BF16)
            xr_ref[rows, :] = acc[:, hi]
        per_chunk(store)

    @pl.when(j == 2)
    def _():
        def store(rows, acc):
            gy_ref[rows, :] = jax.nn.gelu(acc[:, lo]).astype(BF16)
            fr_ref[rows, :] = acc[:, hi].astype(BF16)
        per_chunk(store)

    @pl.when(j >= 3)
    def _():
        def store(rows, acc):
            gate_ref[rows, :] = _sigmoid(acc + bg_ref[...]).astype(BF16)
        per_chunk(store)


def _mix_in(x2, g, sh, sc, wcat, bgate, B, T):
    N, D = x2.shape
    tm, tn = TM_MIX, TN_MIX
    tpb = T // tm
    W = tn // 2
    n_proj = 3
    n_gate = bgate.shape[1] // tn
    tok = lambda i, j: (i, 0)
    scan = lambda i, j: (i % tpb, i // tpb)
    per_b = lambda i, j: (i // tpb, 0, 0)
    out_shape = [jax.ShapeDtypeStruct((N, W), BF16)] * 3 + [
        jax.ShapeDtypeStruct((N, W), F32),
        jax.ShapeDtypeStruct((T, B * W), BF16),
        jax.ShapeDtypeStruct((N, W), BF16),
        jax.ShapeDtypeStruct((N, n_gate * tn), BF16)]
    out_specs = [pl.BlockSpec((tm, W), tok)] * 3 + [
        pl.BlockSpec((tm, W), tok),
        pl.BlockSpec((tm, W), scan),
        pl.BlockSpec((tm, W), tok),
        pl.BlockSpec((tm, tn), lambda i, j: (i, jnp.maximum(j - n_proj, 0)))]
    return pl.pallas_call(
        _mixin_kernel,
        grid=(N // tm, n_proj + n_gate),
        in_specs=[pl.BlockSpec((tm, D), tok),
                  pl.BlockSpec((1, D), lambda i, j: (0, 0)),
                  pl.BlockSpec((1, 1, D), per_b),
                  pl.BlockSpec((1, 1, D), per_b),
                  pl.BlockSpec((D, tn), lambda i, j: (0, j)),
                  pl.BlockSpec((1, tn), lambda i, j: (0, jnp.maximum(j - n_proj, 0)))],
        out_specs=out_specs,
        out_shape=out_shape,
        scratch_shapes=[pltpu.VMEM((tm, D), BF16)],
        compiler_params=_cparams("mix_in", ("parallel", "arbitrary")),
        name="mix_in",
    )(x2, g, sh, sc, wcat, bgate)


def _na_kernel(q_ref, kp_ref, kc_ref, kn_ref, vp_ref, vc_ref, vn_ref, bias_ref, o_ref, kb, vb):
    j = pl.program_id(1)
    nj = pl.num_programs(1)
    R = NA_ROWS_PER_STEP
    half = R // 2
    W = GRID_W
    hw = half * W
    tq = R * W
    win = NA_WIN_ROWS * W
    HT = NA_HEADS_PER_TILE
    lt = HT * NA_HEAD_DIM
    kb[0:hw] = kp_ref[...]
    kb[hw:hw + tq] = kc_ref[...]
    kb[hw + tq:2 * hw + tq] = kn_ref[...]
    vb[0:hw] = vp_ref[...]
    vb[hw:hw + tq] = vc_ref[...]
    vb[hw + tq:2 * hw + tq] = vn_ref[...]
    head_of_lane = lax.broadcasted_iota(I32, (W, lt), 1) // NA_HEAD_DIM
    n_tiles = q_ref.shape[1] // lt
    scale = jnp.asarray(NA_HEAD_DIM ** -0.5, BF16)
    for dr in range(R):
        lo = jnp.where(j == 0, max(dr, half), jnp.where(j == nj - 1, min(dr, half), dr))
        var = lo - dr + (half - 1)
        start = pl.multiple_of(lo * W, W)
        for ht in range(n_tiles):
            cs = slice(ht * lt, (ht + 1) * lt)
            qrow = q_ref[dr * W:(dr + 1) * W, cs] * scale
            qbd = jnp.concatenate(
                [jnp.where(head_of_lane == h, qrow, jnp.zeros_like(qrow)) for h in range(HT)], axis=0)
            kwin = kb[pl.ds(start, win), cs]
            s = lax.dot_general(qbd, kwin, (((1,), (1,)), ((), ())), preferred_element_type=F32)
            s = s + bias_ref[var, ht]
            m = jnp.max(s, axis=-1, keepdims=True)
            p = jnp.exp(s - m)
            l = jnp.sum(p, axis=-1, keepdims=True)
            vwin = vb[pl.ds(start, win), cs]
            o = jnp.dot(p.astype(BF16), vwin, preferred_element_type=F32) * (1.0 / l)
            orow = jnp.where(head_of_lane == 0, o[0:W], 0.0)
            for h in range(1, HT):
                orow = orow + jnp.where(head_of_lane == h, o[h * W:(h + 1) * W], 0.0)
            o_ref[dr * W:(dr + 1) * W, cs] = orow.astype(BF16)


def _na_bias_table(rpb):
    H = rpb.shape[0]
    W, KR, KC = GRID_W, NA_WIN_ROWS, NA_WIN_COLS
    w = np.arange(W)
    cstart = np.clip(w - KC // 2, 0, W - KC)
    wk = np.arange(W)
    col_ok = (wk[None, :] >= cstart[:, None]) & (wk[None, :] < cstart[:, None] + KC)
    cb = wk[None, :] - w[:, None] + (KC - 1)
    onehot = ((cb[None] == np.arange(2 * KC - 1)[:, None, None]) & col_ok[None]).astype(np.float32)
    toep = jnp.einsum('hac,cwx->hawx', rpb.astype(F32), jnp.asarray(onehot), precision=lax.Precision.HIGHEST)
    toep = jnp.where(jnp.asarray(col_ok)[None, None], toep, MASK_VALUE)
    tab = jnp.stack([toep[:, v:v + KR] for v in range(KR)])
    tab = tab.transpose(0, 1, 3, 2, 4)
    HT = NA_HEADS_PER_TILE
    return tab.reshape(KR, H // HT, HT * W, KR * W)


def _na(q, k, v, bias, B, T):
    N, C = q.shape
    R = NA_ROWS_PER_STEP
    tq = R * GRID_W
    hw = tq // 2
    nj = T // tq
    nh = T // hw
    qmap = lambda b, j: (b * nj + j, 0)
    pmap = lambda b, j: (b * nh + jnp.maximum(2 * j - 1, 0), 0)
    nmap = lambda b, j: (b * nh + jnp.minimum(2 * j + 2, nh - 1), 0)
    return pl.pallas_call(
        _na_kernel,
        grid=(B, nj),
        in_specs=[pl.BlockSpec((tq, C), qmap),
                  pl.BlockSpec((hw, C), pmap), pl.BlockSpec((tq, C), qmap), pl.BlockSpec((hw, C), nmap),
                  pl.BlockSpec((hw, C), pmap), pl.BlockSpec((tq, C), qmap), pl.BlockSpec((hw, C), nmap),
                  pl.BlockSpec(bias.shape, lambda b, j: (0, 0, 0, 0))],
        out_specs=pl.BlockSpec((tq, C), qmap),
        out_shape=jax.ShapeDtypeStruct((N, C), BF16),
        scratch_shapes=[pltpu.VMEM((2 * tq, C), BF16), pltpu.VMEM((2 * tq, C), BF16)],
        compiler_params=_cparams("na_attention", ("parallel", "parallel")),
        name="na_attention",
    )(q, k, k, k, v, v, v, bias)


def _lru_gate_kernel(x_ref, xp_ref, xn_ref, cw_ref, cb_ref, wg_ref, bg_ref, lam_ref,
                     af_ref, bf_ref, ab_ref, bb_ref, xpad, *, tpb):
    i = pl.program_id(0)
    tm, C = x_ref.shape
    H = SUBLANES
    first = (i % tpb) == 0
    last = (i % tpb) == tpb - 1
    xpad[0:H] = jnp.where(first, 0.0, xp_ref[...])
    xpad[H:H + tm] = x_ref[...]
    xpad[H + tm:2 * H + tm] = jnp.where(last, 0.0, xn_ref[...])
    left = CONV_W // 2
    xc = xpad[H - left:H - left + tm] * cw_ref[0:1]
    for t in range(1, CONV_W):
        xc = xc + xpad[H - left + t:H - left + t + tm] * cw_ref[t:t + 1]
    xc = xc + cb_ref[...]
    xcb = xc.astype(BF16)
    hx = 0.5 * xc
    for d, (a_ref, b_ref) in enumerate(((af_ref, bf_ref), (ab_ref, bb_ref))):
        t = jnp.tanh(jnp.dot(xcb, wg_ref[d], preferred_element_type=F32) + bg_ref[d])
        c = (-0.5 * LRU_C) * jax.nn.softplus(-lam_ref[d])
        log_a = c * t[:, :C] + c
        a = jnp.exp(log_a)
        a_ref[...] = a
        b_ref[...] = jnp.sqrt(1.0 - a * a) * (t[:, C:] * hx + hx)


def _lru_gates(xr, conv_w, conv_b, wg, bg, lam, B, T):
    N, C = xr.shape
    tm = TM_LRU
    tpb = T // tm
    nb8 = N // SUBLANES
    r8 = tm // SUBLANES
    scan = lambda i: (i % tpb, i // tpb)
    const2 = lambda i: (0, 0)
    const3 = lambda i: (0, 0, 0)
    outs = pl.pallas_call(
        functools.partial(_lru_gate_kernel, tpb=tpb),
        grid=(N // tm,),
        in_specs=[pl.BlockSpec((tm, C), lambda i: (i, 0)),
                  pl.BlockSpec((SUBLANES, C), lambda i: (jnp.maximum(i * r8 - 1, 0), 0)),
                  pl.BlockSpec((SUBLANES, C), lambda i: (jnp.minimum((i + 1) * r8, nb8 - 1), 0)),
                  pl.BlockSpec((CONV_W, C), const2),
                  pl.BlockSpec((1, C), const2),
                  pl.BlockSpec((2, C, 2 * C), const3),
                  pl.BlockSpec((2, 1, 2 * C), const3),
                  pl.BlockSpec((2, 1, C), const3)],
        out_specs=[pl.BlockSpec((tm, C), scan)] * 4,
        out_shape=[jax.ShapeDtypeStruct((T, B * C), F32)] * 4,
        scratch_shapes=[pltpu.VMEM((tm + 2 * SUBLANES, C), F32)],
        compiler_params=_cparams("lru_gates", ("parallel",)),
        name="lru_gates",
    )(xr, xr, xr, conv_w, conv_b, wg, bg, lam)
    return outs


def _scan_bwd_kernel(a_ref, b_ref, h_ref, carry):
    @pl.when(pl.program_id(0) == 0)
    def _():
        carry[...] = jnp.zeros_like(carry)

    tc = a_ref.shape[0]

    def body(s, h):
        t = tc - 1 - s
        h = a_ref[t] * h + b_ref[t]
        h_ref[t] = h
        return h

    carry[...] = lax.fori_loop(0, tc, body, carry[...], unroll=8)


def _scan_fwd_kernel(a_ref, b_ref, hb_ref, gy_ref, y_ref, carry):
    @pl.when(pl.program_id(0) == 0)
    def _():
        carry[...] = jnp.zeros_like(carry)

    tc = a_ref.shape[0]

    def body(t, h):
        h = a_ref[t] * h + b_ref[t]
        y_ref[t] = ((h + hb_ref[t]) * gy_ref[t].astype(F32)).astype(BF16)
        return h

    carry[...] = lax.fori_loop(0, tc, body, carry[...], unroll=8)


def _lru_scans(af, bf, ab, bb, gy):
    T, BC = af.shape
    S = BC // LANES
    tc = T_SCAN
    nc = T // tc
    v3 = lambda z: z.reshape(T, S, LANES)
    blk = (tc, S, LANES)
    fwd = lambda c: (c, 0, 0)
    rev = lambda c: (nc - 1 - c, 0, 0)
    hb = pl.pallas_call(
        _scan_bwd_kernel,
        grid=(nc,),
        in_specs=[pl.BlockSpec(blk, rev), pl.BlockSpec(blk, rev)],
        out_specs=pl.BlockSpec(blk, rev),
        out_shape=jax.ShapeDtypeStruct((T, S, LANES), F32),
        scratch_shapes=[pltpu.VMEM((S, LANES), F32)],
        compiler_params=_cparams("lru_scan", ("arbitrary",)),
        name="lru_scan_bwd",
    )(v3(ab), v3(bb))
    y = pl.pallas_call(
        _scan_fwd_kernel,
        grid=(nc,),
        in_specs=[pl.BlockSpec(blk, fwd)] * 4,
        out_specs=pl.BlockSpec(blk, fwd),
        out_shape=jax.ShapeDtypeStruct((T, S, LANES), BF16),
        scratch_shapes=[pltpu.VMEM((S, LANES), F32)],
        compiler_params=_cparams("lru_scan", ("arbitrary",)),
        name="lru_scan_fwd",
    )(v3(af), v3(bf), hb, v3(gy))
    return y.reshape(T, BC)


def _fnet_tables(T):
    T2 = FNET_GROUP_W
    T1 = T // T2
    k1 = np.arange(T1)[:, None]
    t1 = np.arange(T1)[None, :]
    stage1 = np.zeros((T2, 2 * T1, 2 * T1), np.float32)
    for t2 in range(T2):
        ph = 2.0 * np.pi * ((k1 * (T2 * t1 + t2)) % T) / T
        c, s = np.cos(ph), np.sin(ph)
        stage1[t2] = np.block([[c, s], [-s, c]])
    n = np.arange(T2)
    ph2 = 2.0 * np.pi * ((n[:, None] * n[None, :]) % T2) / T2
    c2, s2 = np.cos(ph2).astype(np.float32), np.sin(ph2).astype(np.float32)
    return stage1, np.concatenate([c2, s2], axis=1)


def _fnet_kernel(x_ref, m1_ref, cs_ref, o_ref, zr, zi, ar, ai, y8, yo, *, T1):
    T2 = FNET_GROUP_W
    T = T1 * T2
    S = SUBLANES
    cs = cs_ref[...]
    for t1 in range(T1):
        z = jnp.dot(x_ref[t1 * T2:(t1 + 1) * T2, :], cs, preferred_element_type=F32)
        zr[t1] = z[:, :T2]
        zi[t1] = -z[:, T2:]
    for g in range(T2 // S):
        grp = slice(g * S, (g + 1) * S)
        cr = jnp.swapaxes(zr[:, grp, :], 0, 1)
        ci = jnp.swapaxes(zi[:, grp, :], 0, 1)
        for j in range(S):
            t2 = g * S + j
            zin = jnp.concatenate([cr[j], ci[j]], axis=0).astype(BF16)
            a = jnp.dot(m1_ref[t2], zin, preferred_element_type=F32)
            ar[t2] = a[:T1]
            ai[t2] = a[T1:]
    inv = 1.0 / np.sqrt(float(T) * T2)
    for g in range(T1 // S):
        grp = slice(g * S, (g + 1) * S)
        cr = jnp.swapaxes(ar[:, grp, :], 0, 1)
        ci = jnp.swapaxes(ai[:, grp, :], 0, 1)
        for j in range(S):
            ain = jnp.concatenate([cr[j], ci[j]], axis=0).astype(BF16)
            y8[j] = jnp.dot(cs, ain, preferred_element_type=F32) * inv
        yo[:, grp, :] = jnp.swapaxes(y8[...], 0, 1)
    o_ref[...] = yo[...].reshape(T, T2).astype(BF16)


def _fnet(fr, m1, cs, B, T):
    N, C = fr.shape
    G = C // FNET_GROUP_W
    T1 = T // FNET_GROUP_W
    blk = pl.BlockSpec((T, FNET_GROUP_W), lambda b, g: (b, g))
    return pl.pallas_call(
        functools.partial(_fnet_kernel, T1=T1),
        grid=(B, G),
        in_specs=[blk,
                  pl.BlockSpec(m1.shape, lambda b, g: (0, 0, 0)),
                  pl.BlockSpec(cs.shape, lambda b, g: (0, 0))],
        out_specs=blk,
        out_shape=jax.ShapeDtypeStruct((N, C), BF16),
        scratch_shapes=[pltpu.VMEM((T1, FNET_GROUP_W, FNET_GROUP_W), F32)] * 2
        + [pltpu.VMEM((FNET_GROUP_W, T1, FNET_GROUP_W), F32)] * 2
        + [pltpu.VMEM((SUBLANES, FNET_GROUP_W, FNET_GROUP_W), F32),
           pltpu.VMEM((FNET_GROUP_W, T1, FNET_GROUP_W), F32)],
        compiler_params=_cparams("fnet", ("parallel", "parallel")),
        name="fnet",
    )(fr, m1, cs)


def _merge_kernel(gate_ref, yna_ref, ylru_ref, yf_ref, x_ref, wna_ref, wlru_ref, wf_ref, wout_ref,
                  gt_ref, g_ref, sh_ref, sc_ref, wr_ref, rb_ref, tri_ref,
                  x1_ref, u2_ref, ls_ref, rw_ref, tc8_ref, toff_ref, tg_ref, cnt_ref, cnt_sc):
    i = pl.program_id(0)

    @pl.when(i == 0)
    def _():
        cnt_sc[...] = jnp.zeros_like(cnt_sc)

    tm, D = x_ref.shape
    gates = gate_ref[...]
    merged = gates[:, 0:D].astype(F32) * jnp.dot(yna_ref[...], wna_ref[...], preferred_element_type=F32)
    merged = merged + gates[:, D:2 * D].astype(F32) * jnp.dot(ylru_ref[...], wlru_ref[...], preferred_element_type=F32)
    merged = merged + gates[:, 2 * D:3 * D].astype(F32) * jnp.dot(yf_ref[...], wf_ref[...], preferred_element_type=F32)
    out = jnp.dot(merged.astype(BF16), wout_ref[...], preferred_element_type=F32)
    x1 = x_ref[...] + gt_ref[0] * out
    x1_ref[...] = x1
    u2 = _rmsnorm_mod(x1, g_ref[...], sh_ref[0], sc_ref[0])
    u2_ref[...] = u2.astype(BF16)

    G, S = N_EXPERT_GROUPS, EXPERTS_PER_GROUP
    logits = _dot_hi(wr_ref[...], u2, ((1,), (1,)))
    score = jax.nn.sigmoid(logits)
    sel = score + rb_ref[...]
    sel_s = [sel[s * G:(s + 1) * G] for s in range(S)]
    sc_s = [score[s * G:(s + 1) * G] for s in range(S)]
    gscore = None
    for a in range(S):
        for b in range(a + 1, S):
            pair = sel_s[a] + sel_s[b]
            gscore = pair if gscore is None else jnp.maximum(gscore, pair)
    giota = lax.broadcasted_iota(I32, (G, tm), 0)
    gmax = jnp.max(gscore, axis=0, keepdims=True)
    gidx = jnp.min(jnp.where(gscore == gmax, giota, G), axis=0, keepdims=True)
    gm = giota == gidx
    cand = [jnp.sum(jnp.where(gm, sel_s[s], 0.0), axis=0, keepdims=True) for s in range(S)]
    raw = [jnp.sum(jnp.where(gm, sc_s[s], 0.0), axis=0, keepdims=True) for s in range(S)]

    def first_argmax(vals):
        m = vals[0]
        for s in range(1, S):
            m = jnp.maximum(m, vals[s])
        idx = jnp.full(m.shape, S - 1, I32)
        for s in range(S - 2, -1, -1):
            idx = jnp.where(vals[s] == m, s, idx)
        return idx

    l0 = first_argmax(cand)
    l1 = first_argmax([jnp.where(l0 == s, -jnp.inf, cand[s]) for s in range(S)])

    def pick(vals, idx):
        out = vals[S - 1]
        for s in range(S - 2, -1, -1):
            out = jnp.where(idx == s, vals[s], out)
        return out

    w0 = pick(raw, l0)
    w1 = pick(raw, l1)
    wsum = w0 + w1
    w0 = w0 / wsum
    w1 = w1 / wsum

    E = G * S
    eiota = lax.broadcasted_iota(I32, (E, tm), 0)
    oh0 = eiota == (l0 * G + gidx)
    oh1 = eiota == (l1 * G + gidx)
    oh = jnp.where(oh0 | oh1, 1.0, 0.0)
    incl = jnp.dot(oh.astype(BF16), tri_ref[...], preferred_element_type=F32)
    cnt = jnp.sum(oh, axis=1, keepdims=True)
    c8 = jnp.floor((cnt + (RUN_ALIGN - 1)) * (1.0 / RUN_ALIGN)) * RUN_ALIGN
    c8b = jnp.broadcast_to(c8, (E, LANES))
    below = (lax.broadcasted_iota(I32, (E, E), 1) < lax.broadcasted_iota(I32, (E, E), 0)).astype(BF16)
    off = jnp.dot(below, c8b.astype(BF16), preferred_element_type=F32)
    local = incl - oh + off[:, 0:1]
    s0 = jnp.sum(jnp.where(oh0, local, 0.0), axis=0, keepdims=True).astype(I32)
    s1 = jnp.sum(jnp.where(oh1, local, 0.0), axis=0, keepdims=True).astype(I32)
    tc8_ref[0] = c8b.astype(I32)
    toff_ref[0] = off.astype(I32)
    tg_ref[0] = cnt_sc[...].astype(I32)
    cnt_sc[...] = cnt_sc[...] + c8b
    cnt_ref[...] = cnt_sc[...].astype(I32)

    row = lax.broadcasted_iota(I32, (SUBLANES, tm), 0)
    ls_ref[...] = jnp.where(row == 0, s0, jnp.where(row == 1, s1, 0))
    rw_ref[...] = jnp.where(row == 0, w0, jnp.where(row == 1, w1, 0.0))


def _merge(gates, y_na, y_lru, y_f, x2, wna, wlru, wf, wout, gt, g, sh, sc, wr, rb, tri, B, T):
    N, D = x2.shape
    C = y_na.shape[1]
    tm = TM_MERGE
    tpb = T // tm
    E = wr.shape[0]
    tok = lambda i: (i, 0)
    scan = lambda i: (i % tpb, i // tpb)
    per_b = lambda i: (i // tpb, 0, 0)
    const2 = lambda i: (0, 0)
    return pl.pallas_call(
        _merge_kernel,
        grid=(N // tm,),
        in_specs=[pl.BlockSpec((tm, 3 * D), tok),
                  pl.BlockSpec((tm, C), tok),
                  pl.BlockSpec((tm, C), scan),
                  pl.BlockSpec((tm, C), tok),
                  pl.BlockSpec((tm, D), tok),
                  pl.BlockSpec((C, D), const2), pl.BlockSpec((C, D), const2), pl.BlockSpec((C, D), const2),
                  pl.BlockSpec((D, D), const2),
                  pl.BlockSpec((1, 1, D), per_b),
                  pl.BlockSpec((1, D), const2),
                  pl.BlockSpec((1, 1, D), per_b),
                  pl.BlockSpec((1, 1, D), per_b),
                  pl.BlockSpec((E, D), const2),
                  pl.BlockSpec((E, 1), const2),
                  pl.BlockSpec((tm, tm), const2)],
        out_specs=[pl.BlockSpec((tm, D), tok),
                   pl.BlockSpec((tm, D), tok),
                   pl.BlockSpec((SUBLANES, tm), lambda i: (0, i)),
                   pl.BlockSpec((SUBLANES, tm), lambda i: (0, i)),
                   pl.BlockSpec((1, E, LANES), lambda i: (i, 0, 0)),
                   pl.BlockSpec((1, E, LANES), lambda i: (i, 0, 0)),
                   pl.BlockSpec((1, E, LANES), lambda i: (i, 0, 0)),
                   pl.BlockSpec((E, LANES), const2)],
        out_shape=[jax.ShapeDtypeStruct((N, D), F32),
                   jax.ShapeDtypeStruct((N, D), BF16),
                   jax.ShapeDtypeStruct((SUBLANES, N), I32),
                   jax.ShapeDtypeStruct((SUBLANES, N), F32),
                   jax.ShapeDtypeStruct((N // tm, E, LANES), I32),
                   jax.ShapeDtypeStruct((N // tm, E, LANES), I32),
                   jax.ShapeDtypeStruct((N // tm, E, LANES), I32),
                   jax.ShapeDtypeStruct((E, LANES), I32)],
        scratch_shapes=[pltpu.VMEM((E, LANES), F32)],
        compiler_params=_cparams("merge_route", ("arbitrary",)),
        name="merge_route",
    )(gates, y_na, y_lru, y_f, x2, wna, wlru, wf, wout, gt, g, sh, sc, wr, rb, tri)


def _for_run_pieces(n_rows, fn):
    def chunk(j, c):
        fn(j * RUN_CHUNK, RUN_CHUNK)
        return c

    lax.fori_loop(0, n_rows // RUN_CHUNK, chunk, 0)
    size = RUN_CHUNK // 2
    while size >= RUN_ALIGN:
        pl.when((n_rows & size) != 0)(functools.partial(fn, (n_rows // (2 * size)) * (2 * size), size))
        size //= 2


def _run_copies(tc8_ref, toff_ref, tg_ref, ps_ref, tile, hbm, stage, sem, to_hbm, wait):
    for e in range(N_EXPERTS):
        idx = tile * N_EXPERTS + e
        loc0 = toff_ref[idx]
        hbm0 = ps_ref[e] + tg_ref[idx]

        def piece(off, size, loc0=loc0, hbm0=hbm0):
            loc = stage.at[pl.ds(pl.multiple_of(loc0 + off, RUN_ALIGN), size)]
            glob = hbm.at[pl.ds(pl.multiple_of(hbm0 + off, RUN_ALIGN), size)]
            cp = pltpu.make_async_copy(loc, glob, sem) if to_hbm else pltpu.make_async_copy(glob, loc, sem)
            if wait:
                cp.wait()
            else:
                cp.start()

        _for_run_pieces(tc8_ref[idx], piece)


def _dispatch_kernel(tc8_ref, toff_ref, tg_ref, ps_ref, pad_ref, nu_ref, ls_ref, u_ref, xs_out, stage, zbuf, sem,
                     *, n_tail):
    i = pl.program_id(0)
    tm = u_ref.shape[0]
    nb = xs_out.shape[0] // MOE_ROWS

    @pl.when(i == 0)
    def _():
        zbuf[...] = jnp.zeros_like(zbuf)

        def zero_copy(blk):
            return pltpu.make_async_copy(zbuf, xs_out.at[pl.ds(pl.multiple_of(blk * MOE_ROWS, MOE_ROWS), MOE_ROWS)],
                                         sem.at[0])

        def guarded(fn):
            for e in range(N_EXPERTS):
                pl.when(pad_ref[e] >= 0)(lambda e=e: fn(zero_copy(pad_ref[e])))
            for k in range(n_tail):
                pl.when(nb - 1 - k >= nu_ref[0])(lambda k=k: fn(zero_copy(nb - 1 - k)))

        guarded(lambda cp: cp.start())
        guarded(lambda cp: cp.wait())

    slots = ls_ref[...]
    cur = i % 2

    def sort_slots(lo, hi):
        riota = lax.broadcasted_iota(I32, (hi - lo, tm), 0) + lo
        hit = (riota == slots[0:1]) | (riota == slots[1:2])
        perm = jnp.where(hit, 1.0, 0.0).astype(BF16)
        stage[cur, lo:hi, :] = jnp.dot(perm, u_ref[...], preferred_element_type=F32).astype(BF16)

    n_slots = stage.shape[1]
    used = toff_ref[i * N_EXPERTS + N_EXPERTS - 1] + tc8_ref[i * N_EXPERTS + N_EXPERTS - 1]
    sort_slots(0, n_slots - MOE_STAGE_TAIL)
    pl.when(used > n_slots - MOE_STAGE_TAIL)(lambda: sort_slots(n_slots - MOE_STAGE_TAIL, n_slots))

    def runs(tile, buf, wait):
        _run_copies(tc8_ref, toff_ref, tg_ref, ps_ref, tile, xs_out, stage.at[buf], sem.at[buf], to_hbm=True, wait=wait)

    runs(i, cur, wait=False)
    pl.when(i > 0)(lambda: runs(i - 1, 1 - cur, wait=True))
    pl.when(i == pl.num_programs(0) - 1)(lambda: runs(i, cur, wait=True))


def _dispatch(tables, pstart, pad_blk, n_used, ls, u2, P, n_tail):
    N, D = u2.shape
    tm = TM_MOE
    n_pref = len(tables) + 3
    grid_spec = pltpu.PrefetchScalarGridSpec(
        num_scalar_prefetch=n_pref,
        grid=(N // tm,),
        in_specs=[pl.BlockSpec((SUBLANES, tm), lambda i, *_: (0, i)),
                  pl.BlockSpec((tm, D), lambda i, *_: (i, 0))],
        out_specs=pl.BlockSpec(memory_space=pl.ANY),
        scratch_shapes=[pltpu.VMEM((2, MOE_STAGE_ROWS, D), BF16), pltpu.VMEM((MOE_ROWS, D), BF16),
                        pltpu.SemaphoreType.DMA((2,))],
    )
    return pl.pallas_call(
        functools.partial(_dispatch_kernel, n_tail=n_tail),
        grid_spec=grid_spec,
        out_shape=jax.ShapeDtypeStruct((P, D), BF16),
        compiler_params=_cparams("moe_dispatch", ("arbitrary",)),
        name="moe_dispatch",
    )(*tables, pstart, pad_blk, n_used, ls, u2)


def _expert_kernel(be_ref, bv_ref, xs_ref, wg_ref, wu_ref, wd_ref, ys_ref, wg_sc, wu_sc, wd_sc):
    i = pl.program_id(0)
    prev = be_ref[jnp.maximum(i - 1, 0)]
    fresh = (i == 0) | (be_ref[i] != prev)

    @pl.when(fresh)
    def _():
        wg_sc[...] = wg_ref[0, 0].astype(BF16)
        wu_sc[...] = wu_ref[0, 0].astype(BF16)
        wd_sc[...] = wd_ref[0, 0].astype(BF16)

    @pl.when(bv_ref[i] > 0)
    def _():
        xb = xs_ref[...]
        hg = jnp.dot(xb, wg_sc[...], preferred_element_type=F32)
        hu = jnp.dot(xb, wu_sc[...], preferred_element_type=F32)
        h = (hg * _sigmoid(hg)) * hu
        ys_ref[...] = jnp.dot(h.astype(BF16), wd_sc[...], preferred_element_type=F32).astype(BF16)

    @pl.when(bv_ref[i] == 0)
    def _():
        ys_ref[...] = jnp.zeros_like(ys_ref)


def _experts(block_e, block_rows, xs, w_gate, w_up, w_down, layer):
    P, D = xs.shape
    _, E, _, DE = w_gate.shape
    nb = P // MOE_ROWS
    wmap = lambda i, be, nu: (layer, be[i], 0, 0)
    grid_spec = pltpu.PrefetchScalarGridSpec(
        num_scalar_prefetch=2,
        grid=(nb,),
        in_specs=[pl.BlockSpec((MOE_ROWS, D), lambda i, be, nu: (i, 0)),
                  pl.BlockSpec((1, 1, D, DE), wmap),
                  pl.BlockSpec((1, 1, D, DE), wmap),
                  pl.BlockSpec((1, 1, DE, D), wmap)],
        out_specs=pl.BlockSpec((MOE_ROWS, D), lambda i, be, nu: (i, 0)),
        scratch_shapes=[pltpu.VMEM((D, DE), BF16), pltpu.VMEM((D, DE), BF16), pltpu.VMEM((DE, D), BF16)],
    )
    return pl.pallas_call(
        _expert_kernel,
        grid_spec=grid_spec,
        out_shape=jax.ShapeDtypeStruct((P, D), BF16),
        compiler_params=_cparams("moe_experts", ("arbitrary",)),
        name="moe_experts",
    )(block_e, block_rows, xs, w_gate, w_up, w_down)


def _combine_kernel(tc8_ref, toff_ref, tg_ref, ps_ref, ls_ref, rw_ref, x_ref, gt_ref, fg_ref, ys_ref, o_ref,
                    stage, sem, *, final):
    i = pl.program_id(0)
    tm = x_ref.shape[0]

    def runs(tile, buf, wait):
        _run_copies(tc8_ref, toff_ref, tg_ref, ps_ref, tile, ys_ref, stage.at[buf], sem.at[buf], to_hbm=False, wait=wait)

    @pl.when(i == 0)
    def _():
        stage[...] = jnp.zeros_like(stage)
        runs(0, 0, wait=False)

    cur = i % 2
    pl.when(i + 1 < pl.num_programs(0))(lambda: runs(i + 1, 1 - cur, wait=False))
    runs(i, cur, wait=True)
    slots = ls_ref[...]
    w = rw_ref[...]
    liota = lax.broadcasted_iota(I32, (tm, stage.shape[1]), 1)
    pick = jnp.where(liota == slots[:, 0:1], w[:, 0:1], 0.0)
    for k in range(1, TOP_K):
        pick = pick + jnp.where(liota == slots[:, k:k + 1], w[:, k:k + 1], 0.0)
    y = jnp.dot(pick.astype(BF16), stage[cur], preferred_element_type=F32)
    x2 = x_ref[...] + gt_ref[0] * y
    if final:
        x2 = (x2 * lax.rsqrt(jnp.mean(x2 * x2, axis=-1, keepdims=True) + EPS)) * fg_ref[...]
    o_ref[...] = x2


def _combine(tables, pstart, ls_t, rw_t, x1, gt, final_g, ys, T, final):
    N, D = x1.shape
    tm = TM_MOE
    tpb = T // tm
    grid_spec = pltpu.PrefetchScalarGridSpec(
        num_scalar_prefetch=len(tables) + 1,
        grid=(N // tm,),
        in_specs=[pl.BlockSpec((tm, SUBLANES), lambda i, *_: (i, 0)),
                  pl.BlockSpec((tm, SUBLANES), lambda i, *_: (i, 0)),
                  pl.BlockSpec((tm, D), lambda i, *_: (i, 0)),
                  pl.BlockSpec((1, 1, D), lambda i, *_: (i // tpb, 0, 0)),
                  pl.BlockSpec((1, D), lambda i, *_: (0, 0)),
                  pl.BlockSpec(memory_space=pl.ANY)],
        out_specs=pl.BlockSpec((tm, D), lambda i, *_: (i, 0)),
        scratch_shapes=[pltpu.VMEM((2, MOE_STAGE_ROWS, D), BF16), pltpu.SemaphoreType.DMA((2,))],
    )
    return pl.pallas_call(
        functools.partial(_combine_kernel, final=final),
        grid_spec=grid_spec,
        out_shape=jax.ShapeDtypeStruct((N, D), F32),
        compiler_params=_cparams("moe_combine", ("arbitrary",)),
        name="moe_combine",
    )(*tables, pstart, ls_t, rw_t, x1, gt, final_g, ys)


def _moe_plan(cnt, nb):
    G, S = N_EXPERT_GROUPS, EXPERTS_PER_GROUP
    counts = cnt[:, 0]
    padded = (counts + MOE_ROWS - 1) // MOE_ROWS * MOE_ROWS
    pend = jnp.cumsum(padded).astype(I32)
    pstart = pend - padded
    pad_blk = jnp.where(padded > 0, pend // MOE_ROWS - 1, -1).astype(I32)
    block_start = jnp.arange(nb, dtype=I32) * MOE_ROWS
    block_row = jnp.minimum(jnp.sum((pend[None, :] <= block_start[:, None]).astype(I32), axis=1), G * S - 1)
    block_e = (block_row % G) * S + block_row // G
    block_rows = jnp.clip((pstart + counts)[block_row] - block_start, 0, MOE_ROWS)
    n_used = pend[-1:] // MOE_ROWS
    return pstart, pad_blk, block_e.astype(I32), block_rows.astype(I32), n_used.astype(I32)


def _block_diag(w):
    nb, bw, _ = w.shape
    eye = jnp.eye(nb, dtype=w.dtype)
    return jnp.einsum('nij,nm->nimj', w, eye).reshape(nb * bw, nb * bw)


def kernel(x, c, ada_w, ada_b, norm_mix_g, norm_ffn_g, w_in, w_branch_gate, b_branch_gate,
           na_rpb, lru_conv_w, lru_conv_b, lru_w_r, lru_b_r, lru_w_i, lru_b_i, lru_lambda,
           w_proj_na, w_proj_lru, w_proj_fnet, w_out, w_router, router_bias,
           w_exp_gate, w_exp_up, w_exp_down, final_g):
    B, T, D = x.shape
    L = ada_w.shape[0]
    N = B * T
    E = N_EXPERTS
    G, S = N_EXPERT_GROUPS, EXPERTS_PER_GROUP
    A = N * TOP_K
    assert TM_MERGE == TM_MOE
    max_rows = A + (N // TM_MOE) * E * (RUN_ALIGN - 1) + E * (MOE_ROWS - 1)
    nb = -(-max_rows // MOE_ROWS)
    P = nb * MOE_ROWS

    mod = _ada_mod(c, ada_w, ada_b)
    m1_np, cs_np = _fnet_tables(T)
    m1 = jnp.asarray(m1_np).astype(BF16)
    cs = jnp.asarray(cs_np).astype(BF16)
    tri = (lax.broadcasted_iota(I32, (TM_MERGE, TM_MERGE), 0)
           <= lax.broadcasted_iota(I32, (TM_MERGE, TM_MERGE), 1)).astype(BF16)
    wr = w_router.T.reshape(G, S, D).transpose(1, 0, 2).reshape(E, D)
    rb = router_bias.reshape(G, S).T.reshape(E, 1)
    fg = final_g.reshape(1, D)

    x2 = x.reshape(N, D)
    for l in range(L):
        sh_mix, sc_mix, gt_mix, sh_ffn, sc_ffn, gt_ffn = [m.reshape(B, 1, D) for m in jnp.split(mod[l], 6, axis=-1)]
        wcat = jnp.concatenate([w_in[l], w_branch_gate[l]], axis=1).astype(BF16)
        q, k, v, xr, gy, fr, gates = _mix_in(x2, norm_mix_g[l].reshape(1, D), sh_mix, sc_mix, wcat,
                                             b_branch_gate[l].reshape(1, -1), B, T)
        y_na = _na(q, k, v, _na_bias_table(na_rpb[l]), B, T)

        wg = (0.5 * jnp.stack([jnp.concatenate([_block_diag(lru_w_r[l, d]), _block_diag(lru_w_i[l, d])], axis=1)
                               for d in range(2)])).astype(BF16)
        bgl = 0.5 * jnp.stack([jnp.concatenate([lru_b_r[l, d], lru_b_i[l, d]]) for d in range(2)])[:, None, :]
        af, bf, ab, bb = _lru_gates(xr, lru_conv_w[l], lru_conv_b[l].reshape(1, -1), wg, bgl,
                                    lru_lambda[l][:, None, :], B, T)
        y_lru = _lru_scans(af, bf, ab, bb, gy)
        y_f = _fnet(fr, m1, cs, B, T)

        x1, u2, ls, rw, tc8, toff, tg, cnt = _merge(
            gates, y_na, y_lru, y_f, x2,
            w_proj_na[l].astype(BF16), w_proj_lru[l].astype(BF16),
            w_proj_fnet[l].astype(BF16), w_out[l].astype(BF16),
            gt_mix, norm_ffn_g[l].reshape(1, D), sh_ffn, sc_ffn, wr, rb, tri, B, T)
        tables = [t[:, :, 0].reshape(-1) for t in (tc8, toff, tg)]
        pstart, pad_blk, block_e, block_rows, n_used = _moe_plan(cnt, nb)
        xs = _dispatch(tables, pstart, pad_blk, n_used, ls, u2, P, nb - A // MOE_ROWS)
        ys = _experts(block_e, block_rows, xs, w_exp_gate, w_exp_up, w_exp_down, l)
        x2 = _combine(tables, pstart, ls.T, rw.T, x1, gt_ffn, fg, ys, T, final=(l == L - 1))
    return x2.reshape(B, T, D)
```

```python
import functools

import numpy as np
import jax
import jax.numpy as jnp
from jax import lax
from jax.experimental import pallas as pl
from jax.experimental.pallas import tpu as pltpu

F32 = jnp.float32
BF16 = jnp.bfloat16
I32 = jnp.int32

GRID_W = 64
NA_HEAD_DIM = 64
NA_WIN_ROWS = 8
NA_WIN_COLS = 16
NA_HEADS_PER_TILE = 4
LRU_BLOCKS = 8
LRU_C = 8.0
CONV_W = 4
FNET_GROUP_W = 128
N_EXPERTS = 32
N_EXPERT_GROUPS = 8
EXPERTS_PER_GROUP = 4
TOP_K = 2
EPS = 1e-6
MASK_VALUE = -1e30

V7X_VMEM_BYTES = 64 * 1024 * 1024
LANES = 128
SUBLANES = 8

TM_MIX = 1024
TN_MIX = 1024
MIX_ROW_CHUNK = 512
NA_ROWS_PER_STEP = 8
TM_LRU = 1024
T_SCAN = 512
TM_MERGE = 512
TM_MOE = 512
MOE_ROWS = 512
RUN_CHUNK = 64
RUN_ALIGN = 16
MOE_STAGE_ROWS = -(-(TM_MOE * TOP_K + N_EXPERTS * (RUN_ALIGN - 1)) // 256) * 256
MOE_STAGE_TAIL = 256

VMEM_MB = {
    "ada_mod": 40,
    "mix_in": 48,
    "na_attention": 48,
    "lru_gates": 48,
    "lru_scan": 48,
    "fnet": 48,
    "merge_route": 48,
    "moe_dispatch": 40,
    "moe_experts": 48,
    "moe_combine": 40,
}


def _cparams(name, semantics):
    limit = VMEM_MB[name] * 1024 * 1024
    assert limit < V7X_VMEM_BYTES
    return pltpu.CompilerParams(dimension_semantics=semantics, vmem_limit_bytes=limit)


def _split2(a):
    hi = a.astype(BF16)
    lo = (a - hi.astype(F32)).astype(BF16)
    return hi, lo


def _dot_hi(a, b, dims):
    a_hi, a_lo = _split2(a)
    b_hi, b_lo = _split2(b)
    dn = (dims, ((), ()))
    m = a.shape[0]
    both = lax.dot_general(jnp.concatenate([a_hi, a_lo], axis=0), b_hi, dn, preferred_element_type=F32)
    return (both[:m] + both[m:]) + lax.dot_general(a_hi, b_lo, dn, preferred_element_type=F32)


def _sigmoid(x):
    return 0.5 * jnp.tanh(0.5 * x) + 0.5


def _rmsnorm_mod(x, g, sh, sc):
    y = x * lax.rsqrt(jnp.mean(x * x, axis=-1, keepdims=True) + EPS)
    return (y * g) * (1.0 + sc) + sh


def _ada_kernel(c_ref, w_ref, b_ref, o_ref):
    c = c_ref[...]
    ca = c * jax.nn.sigmoid(c)
    o_ref[0] = _dot_hi(ca, w_ref[0], ((1,), (0,))) + b_ref[0]


def _ada_mod(c, ada_w, ada_b):
    L, D, D6 = ada_w.shape
    B = c.shape[0]
    bp = -(-B // SUBLANES) * SUBLANES
    cp = jnp.pad(c, ((0, bp - B), (0, 0)))
    tn = D6 // 4
    out = pl.pallas_call(
        _ada_kernel,
        grid=(L, D6 // tn),
        in_specs=[pl.BlockSpec((bp, D), lambda l, j: (0, 0)),
                  pl.BlockSpec((1, D, tn), lambda l, j: (l, 0, j)),
                  pl.BlockSpec((1, 1, tn), lambda l, j: (l, 0, j))],
        out_specs=pl.BlockSpec((1, bp, tn), lambda l, j: (l, 0, j)),
        out_shape=jax.ShapeDtypeStruct((L, bp, D6), F32),
        compiler_params=_cparams("ada_mod", ("parallel", "parallel")),
        name="ada_mod",
    )(cp, ada_w, ada_b.reshape(L, 1, D6))
    return out[:, :B]


def _mixin_kernel(x_ref, g_ref, sh_ref, sc_ref, w_ref, bg_ref,
                  q_ref, k_ref, v_ref, xr_ref, gy_ref, fr_ref, gate_ref, u_sc):
    j = pl.program_id(1)

    @pl.when(j == 0)
    def _():
        u_sc[...] = _rmsnorm_mod(x_ref[...], g_ref[...], sh_ref[0], sc_ref[0]).astype(BF16)

    tm = u_sc.shape[0]
    W = q_ref.shape[1]
    lo, hi = slice(0, W), slice(W, 2 * W)

    def per_chunk(store):
        for c in range(tm // MIX_ROW_CHUNK):
            rows = slice(c * MIX_ROW_CHUNK, (c + 1) * MIX_ROW_CHUNK)
            store(rows, jnp.dot(u_sc[rows, :], w_ref[...], preferred_element_type=F32))

    @pl.when(j == 0)
    def _():
        def store(rows, acc):
            q_ref[rows, :] = acc[:, lo].astype(BF16)
            k_ref[rows, :] = acc[:, hi].astype(BF16)
        per_chunk(store)

    @pl.when(j == 1)
    def _():
        def store(rows, acc):
            v_ref[rows, :] = acc[:, lo].astype(BF16)
            xr_ref[rows, :] = acc[:, hi]
        per_chunk(store)

    @pl.when(j == 2)
    def _():
        def store(rows, acc):
            gy_ref[rows, :] = jax.nn.gelu(acc[:, lo]).astype(BF16)
            fr_ref[rows, :] = acc[:, hi].astype(BF16)
        per_chunk(store)

    @pl.when(j >= 3)
    def _():
        def store(rows, acc):
            gate_ref[rows, :] = _sigmoid(acc + bg_ref[...]).astype(BF16)
        per_chunk(store)


def _mix_in(x2, g, sh, sc, wcat, bgate, B, T):
    N, D = x2.shape
    tm, tn = TM_MIX, TN_MIX
    tpb = T // tm
    W = tn // 2
    n_proj = 3
    n_gate = bgate.shape[1] // tn
    tok = lambda i, j: (i, 0)
    scan = lambda i, j: (i % tpb, i // tpb)
    per_b = lambda i, j: (i // tpb, 0, 0)
    out_shape = [jax.ShapeDtypeStruct((N, W), BF16)] * 3 + [
        jax.ShapeDtypeStruct((N, W), F32),
        jax.ShapeDtypeStruct((T, B * W), BF16),
        jax.ShapeDtypeStruct((N, W), BF16),
        jax.ShapeDtypeStruct((N, n_gate * tn), BF16)]
    out_specs = [pl.BlockSpec((tm, W), tok)] * 3 + [
        pl.BlockSpec((tm, W), tok),
        pl.BlockSpec((tm, W), scan),
        pl.BlockSpec((tm, W), tok),
        pl.BlockSpec((tm, tn), lambda i, j: (i, jnp.maximum(j - n_proj, 0)))]
    return pl.pallas_call(
        _mixin_kernel,
        grid=(N // tm, n_proj + n_gate),
        in_specs=[pl.BlockSpec((tm, D), tok),
                  pl.BlockSpec((1, D), lambda i, j: (0, 0)),
                  pl.BlockSpec((1, 1, D), per_b),
                  pl.BlockSpec((1, 1, D), per_b),
                  pl.BlockSpec((D, tn), lambda i, j: (0, j)),
                  pl.BlockSpec((1, tn), lambda i, j: (0, jnp.maximum(j - n_proj, 0)))],
        out_specs=out_specs,
        out_shape=out_shape,
        scratch_shapes=[pltpu.VMEM((tm, D), BF16)],
        compiler_params=_cparams("mix_in", ("parallel", "arbitrary")),
        name="mix_in",
    )(x2, g, sh, sc, wcat, bgate)


def _na_kernel(q_ref, kp_ref, kc_ref, kn_ref, vp_ref, vc_ref, vn_ref, bias_ref, o_ref, kb, vb):
    j = pl.program_id(1)
    nj = pl.num_programs(1)
    R = NA_ROWS_PER_STEP
    half = R // 2
    W = GRID_W
    hw = half * W
    tq = R * W
    win = NA_WIN_ROWS * W
    HT = NA_HEADS_PER_TILE
    lt = HT * NA_HEAD_DIM
    kb[0:hw] = kp_ref[...]
    kb[hw:hw + tq] = kc_ref[...]
    kb[hw + tq:2 * hw + tq] = kn_ref[...]
    vb[0:hw] = vp_ref[...]
    vb[hw:hw + tq] = vc_ref[...]
    vb[hw + tq:2 * hw + tq] = vn_ref[...]
    head_of_lane = lax.broadcasted_iota(I32, (W, lt), 1) // NA_HEAD_DIM
    n_tiles = q_ref.shape[1] // lt
    scale = jnp.asarray(NA_HEAD_DIM ** -0.5, BF16)
    for dr in range(R):
        lo = jnp.where(j == 0, max(dr, half), jnp.where(j == nj - 1, min(dr, half), dr))
        var = lo - dr + (half - 1)
        start = pl.multiple_of(lo * W, W)
        for ht in range(n_tiles):
            cs = slice(ht * lt, (ht + 1) * lt)
            qrow = q_ref[dr * W:(dr + 1) * W, cs] * scale
            qbd = jnp.concatenate(
                [jnp.where(head_of_lane == h, qrow, jnp.zeros_like(qrow)) for h in range(HT)], axis=0)
            kwin = kb[pl.ds(start, win), cs]
            s = lax.dot_general(qbd, kwin, (((1,), (1,)), ((), ())), preferred_element_type=F32)
            s = s + bias_ref[var, ht]
            m = jnp.max(s, axis=-1, keepdims=True)
            p = jnp.exp(s - m)
            l = jnp.sum(p, axis=-1, keepdims=True)
            vwin = vb[pl.ds(start, win), cs]
            o = jnp.dot(p.astype(BF16), vwin, preferred_element_type=F32) * (1.0 / l)
            orow = jnp.where(head_of_lane == 0, o[0:W], 0.0)
            for h in range(1, HT):
                orow = orow + jnp.where(head_of_lane == h, o[h * W:(h + 1) * W], 0.0)
            o_ref[dr * W:(dr + 1) * W, cs] = orow.astype(BF16)


def _na_bias_table(rpb):
    H = rpb.shape[0]
    W, KR, KC = GRID_W, NA_WIN_ROWS, NA_WIN_COLS
    w = np.arange(W)
    cstart = np.clip(w - KC // 2, 0, W - KC)
    wk = np.arange(W)
    col_ok = (wk[None, :] >= cstart[:, None]) & (wk[None, :] < cstart[:, None] + KC)
    cb = wk[None, :] - w[:, None] + (KC - 1)
    onehot = ((cb[None] == np.arange(2 * KC - 1)[:, None, None]) & col_ok[None]).astype(np.float32)
    toep = jnp.einsum('hac,cwx->hawx', rpb.astype(F32), jnp.asarray(onehot), precision=lax.Precision.HIGHEST)
    toep = jnp.where(jnp.asarray(col_ok)[None, None], toep, MASK_VALUE)
    tab = jnp.stack([toep[:, v:v + KR] for v in range(KR)])
    tab = tab.transpose(0, 1, 3, 2, 4)
    HT = NA_HEADS_PER_TILE
    return tab.reshape(KR, H // HT, HT * W, KR * W)


def _na(q, k, v, bias, B, T):
    N, C = q.shape
    R = NA_ROWS_PER_STEP
    tq = R * GRID_W
    hw = tq // 2
    nj = T // tq
    nh = T // hw
    qmap = lambda b, j: (b * nj + j, 0)
    pmap = lambda b, j: (b * nh + jnp.maximum(2 * j - 1, 0), 0)
    nmap = lambda b, j: (b * nh + jnp.minimum(2 * j + 2, nh - 1), 0)
    return pl.pallas_call(
        _na_kernel,
        grid=(B, nj),
        in_specs=[pl.BlockSpec((tq, C), qmap),
                  pl.BlockSpec((hw, C), pmap), pl.BlockSpec((tq, C), qmap), pl.BlockSpec((hw, C), nmap),
                  pl.BlockSpec((hw, C), pmap), pl.BlockSpec((tq, C), qmap), pl.BlockSpec((hw, C), nmap),
                  pl.BlockSpec(bias.shape, lambda b, j: (0, 0, 0, 0))],
        out_specs=pl.BlockSpec((tq, C), qmap),
        out_shape=jax.ShapeDtypeStruct((N, C), BF16),
        scratch_shapes=[pltpu.VMEM((2 * tq, C), BF16), pltpu.VMEM((2 * tq, C), BF16)],
        compiler_params=_cparams("na_attention", ("parallel", "parallel")),
        name="na_attention",
    )(q, k, k, k, v, v, v, bias)


def _lru_gate_kernel(x_ref, xp_ref, xn_ref, cw_ref, cb_ref, wg_ref, bg_ref, lam_ref,
                     af_ref, bf_ref, ab_ref, bb_ref, xpad, *, tpb):
    i = pl.program_id(0)
    tm, C = x_ref.shape
    H = SUBLANES
    first = (i % tpb) == 0
    last = (i % tpb) == tpb - 1
    xpad[0:H] = jnp.where(first, 0.0, xp_ref[...])
    xpad[H:H + tm] = x_ref[...]
    xpad[H + tm:2 * H + tm] = jnp.where(last, 0.0, xn_ref[...])
    left = CONV_W // 2
    xc = xpad[H - left:H - left + tm] * cw_ref[0:1]
    for t in range(1, CONV_W):
        xc = xc + xpad[H - left + t:H - left + t + tm] * cw_ref[t:t + 1]
    xc = xc + cb_ref[...]
    xcb = xc.astype(BF16)
    hx = 0.5 * xc
    for d, (a_ref, b_ref) in enumerate(((af_ref, bf_ref), (ab_ref, bb_ref))):
        t = jnp.tanh(jnp.dot(xcb, wg_ref[d], preferred_element_type=F32) + bg_ref[d])
        c = (-0.5 * LRU_C) * jax.nn.softplus(-lam_ref[d])
        log_a = c * t[:, :C] + c
        a = jnp.exp(log_a)
        a_ref[...] = a
        b_ref[...] = jnp.sqrt(1.0 - a * a) * (t[:, C:] * hx + hx)


def _lru_gates(xr, conv_w, conv_b, wg, bg, lam, B, T):
    N, C = xr.shape
    tm = TM_LRU
    tpb = T // tm
    nb8 = N // SUBLANES
    r8 = tm // SUBLANES
    scan = lambda i: (i % tpb, i // tpb)
    const2 = lambda i: (0, 0)
    const3 = lambda i: (0, 0, 0)
    outs = pl.pallas_call(
        functools.partial(_lru_gate_kernel, tpb=tpb),
        grid=(N // tm,),
        in_specs=[pl.BlockSpec((tm, C), lambda i: (i, 0)),
                  pl.BlockSpec((SUBLANES, C), lambda i: (jnp.maximum(i * r8 - 1, 0), 0)),
                  pl.BlockSpec((SUBLANES, C), lambda i: (jnp.minimum((i + 1) * r8, nb8 - 1), 0)),
                  pl.BlockSpec((CONV_W, C), const2),
                  pl.BlockSpec((1, C), const2),
                  pl.BlockSpec((2, C, 2 * C), const3),
                  pl.BlockSpec((2, 1, 2 * C), const3),
                  pl.BlockSpec((2, 1, C), const3)],
        out_specs=[pl.BlockSpec((tm, C), scan)] * 4,
        out_shape=[jax.ShapeDtypeStruct((T, B * C), F32)] * 4,
        scratch_shapes=[pltpu.VMEM((tm + 2 * SUBLANES, C), F32)],
        compiler_params=_cparams("lru_gates", ("parallel",)),
        name="lru_gates",
    )(xr, xr, xr, conv_w, conv_b, wg, bg, lam)
    return outs


def _scan_bwd_kernel(a_ref, b_ref, h_ref, carry):
    @pl.when(pl.program_id(0) == 0)
    def _():
        carry[...] = jnp.zeros_like(carry)

    tc = a_ref.shape[0]

    def body(s, h):
        t = tc - 1 - s
        h = a_ref[t] * h + b_ref[t]
        h_ref[t] = h
        return h

    carry[...] = lax.fori_loop(0, tc, body, carry[...], unroll=8)


def _scan_fwd_kernel(a_ref, b_ref, hb_ref, gy_ref, y_ref, carry):
    @pl.when(pl.program_id(0) == 0)
    def _():
        carry[...] = jnp.zeros_like(carry)

    tc = a_ref.shape[0]

    def body(t, h):
        h = a_ref[t] * h + b_ref[t]
        y_ref[t] = ((h + hb_ref[t]) * gy_ref[t].astype(F32)).astype(BF16)
        return h

    carry[...] = lax.fori_loop(0, tc, body, carry[...], unroll=8)


def _lru_scans(af, bf, ab, bb, gy):
    T, BC = af.shape
    S = BC // LANES
    tc = T_SCAN
    nc = T // tc
    v3 = lambda z: z.reshape(T, S, LANES)
    blk = (tc, S, LANES)
    fwd = lambda c: (c, 0, 0)
    rev = lambda c: (nc - 1 - c, 0, 0)
    hb = pl.pallas_call(
        _scan_bwd_kernel,
        grid=(nc,),
        in_specs=[pl.BlockSpec(blk, rev), pl.BlockSpec(blk, rev)],
        out_specs=pl.BlockSpec(blk, rev),
        out_shape=jax.ShapeDtypeStruct((T, S, LANES), F32),
        scratch_shapes=[pltpu.VMEM((S, LANES), F32)],
        compiler_params=_cparams("lru_scan", ("arbitrary",)),
        name="lru_scan_bwd",
    )(v3(ab), v3(bb))
    y = pl.pallas_call(
        _scan_fwd_kernel,
        grid=(nc,),
        in_specs=[pl.BlockSpec(blk, fwd)] * 4,
        out_specs=pl.BlockSpec(blk, fwd),
        out_shape=jax.ShapeDtypeStruct((T, S, LANES), BF16),
        scratch_shapes=[pltpu.VMEM((S, LANES), F32)],
        compiler_params=_cparams("lru_scan", ("arbitrary",)),
        name="lru_scan_fwd",
    )(v3(af), v3(bf), hb, v3(gy))
    return y.reshape(T, BC)


def _fnet_tables(T):
    T2 = FNET_GROUP_W
    T1 = T // T2
    k1 = np.arange(T1)[:, None]
    t1 = np.arange(T1)[None, :]
    stage1 = np.zeros((T2, 2 * T1, 2 * T1), np.float32)
    for t2 in range(T2):
        ph = 2.0 * np.pi * ((k1 * (T2 * t1 + t2)) % T) / T
        c, s = np.cos(ph), np.sin(ph)
        stage1[t2] = np.block([[c, s], [-s, c]])
    n = np.arange(T2)
    ph2 = 2.0 * np.pi * ((n[:, None] * n[None, :]) % T2) / T2
    c2, s2 = np.cos(ph2).astype(np.float32), np.sin(ph2).astype(np.float32)
    return stage1, np.concatenate([c2, s2], axis=1)


def _fnet_kernel(x_ref, m1_ref, cs_ref, o_ref, zr, zi, ar, ai, y8, yo, *, T1):
    T2 = FNET_GROUP_W
    T = T1 * T2
    S = SUBLANES
    cs = cs_ref[...]
    for t1 in range(T1):
        z = jnp.dot(x_ref[t1 * T2:(t1 + 1) * T2, :], cs, preferred_element_type=F32)
        zr[t1] = z[:, :T2]
        zi[t1] = -z[:, T2:]
    for g in range(T2 // S):
        grp = slice(g * S, (g + 1) * S)
        cr = jnp.swapaxes(zr[:, grp, :], 0, 1)
        ci = jnp.swapaxes(zi[:, grp, :], 0, 1)
        for j in range(S):
            t2 = g * S + j
            zin = jnp.concatenate([cr[j], ci[j]], axis=0).astype(BF16)
            a = jnp.dot(m1_ref[t2], zin, preferred_element_type=F32)
            ar[t2] = a[:T1]
            ai[t2] = a[T1:]
    inv = 1.0 / np.sqrt(float(T) * T2)
    for g in range(T1 // S):
        grp = slice(g * S, (g + 1) * S)
        cr = jnp.swapaxes(ar[:, grp, :], 0, 1)
        ci = jnp.swapaxes(ai[:, grp, :], 0, 1)
        for j in range(S):
            ain = jnp.concatenate([cr[j], ci[j]], axis=0).astype(BF16)
            y8[j] = jnp.dot(cs, ain, preferred_element_type=F32) * inv
        yo[:, grp, :] = jnp.swapaxes(y8[...], 0, 1)
    o_ref[...] = yo[...].reshape(T, T2).astype(BF16)


def _fnet(fr, m1, cs, B, T):
    N, C = fr.shape
    G = C // FNET_GROUP_W
    T1 = T // FNET_GROUP_W
    blk = pl.BlockSpec((T, FNET_GROUP_W), lambda b, g: (b, g))
    return pl.pallas_call(
        functools.partial(_fnet_kernel, T1=T1),
        grid=(B, G),
        in_specs=[blk,
                  pl.BlockSpec(m1.shape, lambda b, g: (0, 0, 0)),
                  pl.BlockSpec(cs.shape, lambda b, g: (0, 0))],
        out_specs=blk,
        out_shape=jax.ShapeDtypeStruct((N, C), BF16),
        scratch_shapes=[pltpu.VMEM((T1, FNET_GROUP_W, FNET_GROUP_W), F32)] * 2
        + [pltpu.VMEM((FNET_GROUP_W, T1, FNET_GROUP_W), F32)] * 2
        + [pltpu.VMEM((SUBLANES, FNET_GROUP_W, FNET_GROUP_W), F32),
           pltpu.VMEM((FNET_GROUP_W, T1, FNET_GROUP_W), F32)],
        compiler_params=_cparams("fnet", ("parallel", "parallel")),
        name="fnet",
    )(fr, m1, cs)


def _merge_kernel(gate_ref, yna_ref, ylru_ref, yf_ref, x_ref, wna_ref, wlru_ref, wf_ref, wout_ref,
                  gt_ref, g_ref, sh_ref, sc_ref, wr_ref, rb_ref, tri_ref,
                  x1_ref, u2_ref, ls_ref, rw_ref, tc8_ref, toff_ref, tg_ref, cnt_ref, cnt_sc):
    i = pl.program_id(0)

    @pl.when(i == 0)
    def _():
        cnt_sc[...] = jnp.zeros_like(cnt_sc)

    tm, D = x_ref.shape
    gates = gate_ref[...]
    merged = gates[:, 0:D].astype(F32) * jnp.dot(yna_ref[...], wna_ref[...], preferred_element_type=F32)
    merged = merged + gates[:, D:2 * D].astype(F32) * jnp.dot(ylru_ref[...], wlru_ref[...], preferred_element_type=F32)
    merged = merged + gates[:, 2 * D:3 * D].astype(F32) * jnp.dot(yf_ref[...], wf_ref[...], preferred_element_type=F32)
    out = jnp.dot(merged.astype(BF16), wout_ref[...], preferred_element_type=F32)
    x1 = x_ref[...] + gt_ref[0] * out
    x1_ref[...] = x1
    u2 = _rmsnorm_mod(x1, g_ref[...], sh_ref[0], sc_ref[0])
    u2_ref[...] = u2.astype(BF16)

    G, S = N_EXPERT_GROUPS, EXPERTS_PER_GROUP
    logits = _dot_hi(wr_ref[...], u2, ((1,), (1,)))
    score = jax.nn.sigmoid(logits)
    sel = score + rb_ref[...]
    sel_s = [sel[s * G:(s + 1) * G] for s in range(S)]
    sc_s = [score[s * G:(s + 1) * G] for s in range(S)]
    gscore = None
    for a in range(S):
        for b in range(a + 1, S):
            pair = sel_s[a] + sel_s[b]
            gscore = pair if gscore is None else jnp.maximum(gscore, pair)
    giota = lax.broadcasted_iota(I32, (G, tm), 0)
    gmax = jnp.max(gscore, axis=0, keepdims=True)
    gidx = jnp.min(jnp.where(gscore == gmax, giota, G), axis=0, keepdims=True)
    gm = giota == gidx
    cand = [jnp.sum(jnp.where(gm, sel_s[s], 0.0), axis=0, keepdims=True) for s in range(S)]
    raw = [jnp.sum(jnp.where(gm, sc_s[s], 0.0), axis=0, keepdims=True) for s in range(S)]

    def first_argmax(vals):
        m = vals[0]
        for s in range(1, S):
            m = jnp.maximum(m, vals[s])
        idx = jnp.full(m.shape, S - 1, I32)
        for s in range(S - 2, -1, -1):
            idx = jnp.where(vals[s] == m, s, idx)
        return idx

    l0 = first_argmax(cand)
    l1 = first_argmax([jnp.where(l0 == s, -jnp.inf, cand[s]) for s in range(S)])

    def pick(vals, idx):
        out = vals[S - 1]
        for s in range(S - 2, -1, -1):
            out = jnp.where(idx == s, vals[s], out)
        return out

    w0 = pick(raw, l0)
    w1 = pick(raw, l1)
    wsum = w0 + w1
    w0 = w0 / wsum
    w1 = w1 / wsum

    E = G * S
    eiota = lax.broadcasted_iota(I32, (E, tm), 0)
    oh0 = eiota == (l0 * G + gidx)
    oh1 = eiota == (l1 * G + gidx)
    oh = jnp.where(oh0 | oh1, 1.0, 0.0)
    incl = jnp.dot(oh.astype(BF16), tri_ref[...], preferred_element_type=F32)
    cnt = jnp.sum(oh, axis=1, keepdims=True)
    c8 = jnp.floor((cnt + (RUN_ALIGN - 1)) * (1.0 / RUN_ALIGN)) * RUN_ALIGN
    c8b = jnp.broadcast_to(c8, (E, LANES))
    below = (lax.broadcasted_iota(I32, (E, E), 1) < lax.broadcasted_iota(I32, (E, E), 0)).astype(BF16)
    off = jnp.dot(below, c8b.astype(BF16), preferred_element_type=F32)
    local = incl - oh + off[:, 0:1]
    s0 = jnp.sum(jnp.where(oh0, local, 0.0), axis=0, keepdims=True).astype(I32)
    s1 = jnp.sum(jnp.where(oh1, local, 0.0), axis=0, keepdims=True).astype(I32)
    tc8_ref[0] = c8b.astype(I32)
    toff_ref[0] = off.astype(I32)
    tg_ref[0] = cnt_sc[...].astype(I32)
    cnt_sc[...] = cnt_sc[...] + c8b
    cnt_ref[...] = cnt_sc[...].astype(I32)

    row = lax.broadcasted_iota(I32, (SUBLANES, tm), 0)
    ls_ref[...] = jnp.where(row == 0, s0, jnp.where(row == 1, s1, 0))
    rw_ref[...] = jnp.where(row == 0, w0, jnp.where(row == 1, w1, 0.0))


def _merge(gates, y_na, y_lru, y_f, x2, wna, wlru, wf, wout, gt, g, sh, sc, wr, rb, tri, B, T):
    N, D = x2.shape
    C = y_na.shape[1]
    tm = TM_MERGE
    tpb = T // tm
    E = wr.shape[0]
    tok = lambda i: (i, 0)
    scan = lambda i: (i % tpb, i // tpb)
    per_b = lambda i: (i // tpb, 0, 0)
    const2 = lambda i: (0, 0)
    return pl.pallas_call(
        _merge_kernel,
        grid=(N // tm,),
        in_specs=[pl.BlockSpec((tm, 3 * D), tok),
                  pl.BlockSpec((tm, C), tok),
                  pl.BlockSpec((tm, C), scan),
                  pl.BlockSpec((tm, C), tok),
                  pl.BlockSpec((tm, D), tok),
                  pl.BlockSpec((C, D), const2), pl.BlockSpec((C, D), const2), pl.BlockSpec((C, D), const2),
                  pl.BlockSpec((D, D), const2),
                  pl.BlockSpec((1, 1, D), per_b),
                  pl.BlockSpec((1, D), const2),
                  pl.BlockSpec((1, 1, D), per_b),
                  pl.BlockSpec((1, 1, D), per_b),
                  pl.BlockSpec((E, D), const2),
                  pl.BlockSpec((E, 1), const2),
                  pl.BlockSpec((tm, tm), const2)],
        out_specs=[pl.BlockSpec((tm, D), tok),
                   pl.BlockSpec((tm, D), tok),
                   pl.BlockSpec((SUBLANES, tm), lambda i: (0, i)),
                   pl.BlockSpec((SUBLANES, tm), lambda i: (0, i)),
                   pl.BlockSpec((1, E, LANES), lambda i: (i, 0, 0)),
                   pl.BlockSpec((1, E, LANES), lambda i: (i, 0, 0)),
                   pl.BlockSpec((1, E, LANES), lambda i: (i, 0, 0)),
                   pl.BlockSpec((E, LANES), const2)],
        out_shape=[jax.ShapeDtypeStruct((N, D), F32),
                   jax.ShapeDtypeStruct((N, D), BF16),
                   jax.ShapeDtypeStruct((SUBLANES, N), I32),
                   jax.ShapeDtypeStruct((SUBLANES, N), F32),
                   jax.ShapeDtypeStruct((N // tm, E, LANES), I32),
                   jax.ShapeDtypeStruct((N // tm, E, LANES), I32),
                   jax.ShapeDtypeStruct((N // tm, E, LANES), I32),
                   jax.ShapeDtypeStruct((E, LANES), I32)],
        scratch_shapes=[pltpu.VMEM((E, LANES), F32)],
        compiler_params=_cparams("merge_route", ("arbitrary",)),
        name="merge_route",
    )(gates, y_na, y_lru, y_f, x2, wna, wlru, wf, wout, gt, g, sh, sc, wr, rb, tri)


def _for_run_pieces(n_rows, fn):
    def chunk(j, c):
        fn(j * RUN_CHUNK, RUN_CHUNK)
        return c

    lax.fori_loop(0, n_rows // RUN_CHUNK, chunk, 0)
    size = RUN_CHUNK // 2
    while size >= RUN_ALIGN:
        pl.when((n_rows & size) != 0)(functools.partial(fn, (n_rows // (2 * size)) * (2 * size), size))
        size //= 2


def _run_copies(tc8_ref, toff_ref, tg_ref, ps_ref, tile, hbm, stage, sem, to_hbm, wait):
    for e in range(N_EXPERTS):
        idx = tile * N_EXPERTS + e
        loc0 = toff_ref[idx]
        hbm0 = ps_ref[e] + tg_ref[idx]

        def piece(off, size, loc0=loc0, hbm0=hbm0):
            loc = stage.at[pl.ds(pl.multiple_of(loc0 + off, RUN_ALIGN), size)]
            glob = hbm.at[pl.ds(pl.multiple_of(hbm0 + off, RUN_ALIGN), size)]
            cp = pltpu.make_async_copy(loc, glob, sem) if to_hbm else pltpu.make_async_copy(glob, loc, sem)
            if wait:
                cp.wait()
            else:
                cp.start()

        _for_run_pieces(tc8_ref[idx], piece)


def _dispatch_kernel(tc8_ref, toff_ref, tg_ref, ps_ref, pad_ref, nu_ref, ls_ref, u_ref, xs_out, stage, zbuf, sem,
                     *, n_tail):
    i = pl.program_id(0)
    tm = u_ref.shape[0]
    nb = xs_out.shape[0] // MOE_ROWS

    @pl.when(i == 0)
    def _():
        zbuf[...] = jnp.zeros_like(zbuf)

        def zero_copy(blk):
            return pltpu.make_async_copy(zbuf, xs_out.at[pl.ds(pl.multiple_of(blk * MOE_ROWS, MOE_ROWS), MOE_ROWS)],
                                         sem.at[0])

        def guarded(fn):
            for e in range(N_EXPERTS):
                pl.when(pad_ref[e] >= 0)(lambda e=e: fn(zero_copy(pad_ref[e])))
            for k in range(n_tail):
                pl.when(nb - 1 - k >= nu_ref[0])(lambda k=k: fn(zero_copy(nb - 1 - k)))

        guarded(lambda cp: cp.start())
        guarded(lambda cp: cp.wait())

    slots = ls_ref[...]
    cur = i % 2

    def sort_slots(lo, hi):
        riota = lax.broadcasted_iota(I32, (hi - lo, tm), 0) + lo
        hit = (riota == slots[0:1]) | (riota == slots[1:2])
        perm = jnp.where(hit, 1.0, 0.0).astype(BF16)
        stage[cur, lo:hi, :] = jnp.dot(perm, u_ref[...], preferred_element_type=F32).astype(BF16)

    n_slots = stage.shape[1]
    used = toff_ref[i * N_EXPERTS + N_EXPERTS - 1] + tc8_ref[i * N_EXPERTS + N_EXPERTS - 1]
    sort_slots(0, n_slots - MOE_STAGE_TAIL)
    pl.when(used > n_slots - MOE_STAGE_TAIL)(lambda: sort_slots(n_slots - MOE_STAGE_TAIL, n_slots))

    def runs(tile, buf, wait):
        _run_copies(tc8_ref, toff_ref, tg_ref, ps_ref, tile, xs_out, stage.at[buf], sem.at[buf], to_hbm=True, wait=wait)

    runs(i, cur, wait=False)
    pl.when(i > 0)(lambda: runs(i - 1, 1 - cur, wait=True))
    pl.when(i == pl.num_programs(0) - 1)(lambda: runs(i, cur, wait=True))


def _dispatch(tables, pstart, pad_blk, n_used, ls, u2, P, n_tail):
    N, D = u2.shape
    tm = TM_MOE
    n_pref = len(tables) + 3
    grid_spec = pltpu.PrefetchScalarGridSpec(
        num_scalar_prefetch=n_pref,
        grid=(N // tm,),
        in_specs=[pl.BlockSpec((SUBLANES, tm), lambda i, *_: (0, i)),
                  pl.BlockSpec((tm, D), lambda i, *_: (i, 0))],
        out_specs=pl.BlockSpec(memory_space=pl.ANY),
        scratch_shapes=[pltpu.VMEM((2, MOE_STAGE_ROWS, D), BF16), pltpu.VMEM((MOE_ROWS, D), BF16),
                        pltpu.SemaphoreType.DMA((2,))],
    )
    return pl.pallas_call(
        functools.partial(_dispatch_kernel, n_tail=n_tail),
        grid_spec=grid_spec,
        out_shape=jax.ShapeDtypeStruct((P, D), BF16),
        compiler_params=_cparams("moe_dispatch", ("arbitrary",)),
        name="moe_dispatch",
    )(*tables, pstart, pad_blk, n_used, ls, u2)


def _expert_kernel(be_ref, bv_ref, slot_ref, next_ref, xs_ref, wg_hbm, wu_hbm, wd_hbm, ys_ref,
                   wg_f32, wu_f32, wd_f32, wg_sc, wu_sc, wd_sc, sem, *, layer):
    i = pl.program_id(0)
    prev = be_ref[jnp.maximum(i - 1, 0)]
    fresh = (i == 0) | (be_ref[i] != prev)

    def fetch(e, buf):
        return [pltpu.make_async_copy(hbm.at[layer, e], dst.at[buf], sem.at[buf])
                for hbm, dst in ((wg_hbm, wg_f32), (wu_hbm, wu_f32), (wd_hbm, wd_f32))]

    @pl.when(i == 0)
    def _():
        for cp in fetch(be_ref[0], 0):
            cp.start()

    @pl.when(fresh)
    def _():
        buf = slot_ref[i]
        for cp in fetch(be_ref[i], buf):
            cp.wait()
        wg_sc[...] = wg_f32[buf].astype(BF16)
        wu_sc[...] = wu_f32[buf].astype(BF16)
        wd_sc[...] = wd_f32[buf].astype(BF16)

        @pl.when(next_ref[i] >= 0)
        def _():
            for cp in fetch(next_ref[i], 1 - buf):
                cp.start()

    @pl.when(bv_ref[i] > 0)
    def _():
        xb = xs_ref[...]
        hg = jnp.dot(xb, wg_sc[...], preferred_element_type=F32)
        hu = jnp.dot(xb, wu_sc[...], preferred_element_type=F32)
        h = (hg * _sigmoid(hg)) * hu
        ys_ref[...] = jnp.dot(h.astype(BF16), wd_sc[...], preferred_element_type=F32).astype(BF16)

    @pl.when(bv_ref[i] == 0)
    def _():
        ys_ref[...] = jnp.zeros_like(ys_ref)


def _experts(block_e, block_rows, xs, w_gate, w_up, w_down, layer):
    P, D = xs.shape
    _, E, _, DE = w_gate.shape
    nb = P // MOE_ROWS
    pos = jnp.arange(nb, dtype=I32)
    fresh = jnp.concatenate([jnp.ones((1,), bool), block_e[1:] != block_e[:-1]])
    slot = ((jnp.cumsum(fresh.astype(I32)) - 1) % 2).astype(I32)
    later = lax.cummin(jnp.where(fresh, pos, nb)[::-1])[::-1]
    nxt = jnp.concatenate([later[1:], jnp.full((1,), nb, I32)])
    next_e = jnp.where(nxt < nb, block_e[jnp.minimum(nxt, nb - 1)], -1).astype(I32)
    grid_spec = pltpu.PrefetchScalarGridSpec(
        num_scalar_prefetch=4,
        grid=(nb,),
        in_specs=[pl.BlockSpec((MOE_ROWS, D), lambda i, *_: (i, 0)),
                  pl.BlockSpec(memory_space=pl.ANY),
                  pl.BlockSpec(memory_space=pl.ANY),
                  pl.BlockSpec(memory_space=pl.ANY)],
        out_specs=pl.BlockSpec((MOE_ROWS, D), lambda i, *_: (i, 0)),
        scratch_shapes=[pltpu.VMEM((2, D, DE), F32), pltpu.VMEM((2, D, DE), F32), pltpu.VMEM((2, DE, D), F32),
                        pltpu.VMEM((D, DE), BF16), pltpu.VMEM((D, DE), BF16), pltpu.VMEM((DE, D), BF16),
                        pltpu.SemaphoreType.DMA((2,))],
    )
    return pl.pallas_call(
        functools.partial(_expert_kernel, layer=layer),
        grid_spec=grid_spec,
        out_shape=jax.ShapeDtypeStruct((P, D), BF16),
        compiler_params=_cparams("moe_experts", ("arbitrary",)),
        name="moe_experts",
    )(block_e, block_rows, slot, next_e, xs, w_gate, w_up, w_down)


def _combine_kernel(tc8_ref, toff_ref, tg_ref, ps_ref, ls_ref, rw_ref, x_ref, gt_ref, fg_ref, ys_ref, o_ref,
                    stage, sem, *, final):
    i = pl.program_id(0)
    tm = x_ref.shape[0]

    def runs(tile, buf, wait):
        _run_copies(tc8_ref, toff_ref, tg_ref, ps_ref, tile, ys_ref, stage.at[buf], sem.at[buf], to_hbm=False, wait=wait)

    @pl.when(i == 0)
    def _():
        stage[...] = jnp.zeros_like(stage)
        runs(0, 0, wait=False)

    cur = i % 2
    pl.when(i + 1 < pl.num_programs(0))(lambda: runs(i + 1, 1 - cur, wait=False))
    runs(i, cur, wait=True)
    slots = ls_ref[...]
    w = rw_ref[...]
    liota = lax.broadcasted_iota(I32, (tm, stage.shape[1]), 1)
    pick = jnp.where(liota == slots[:, 0:1], w[:, 0:1], 0.0)
    for k in range(1, TOP_K):
        pick = pick + jnp.where(liota == slots[:, k:k + 1], w[:, k:k + 1], 0.0)
    y = jnp.dot(pick.astype(BF16), stage[cur], preferred_element_type=F32)
    x2 = x_ref[...] + gt_ref[0] * y
    if final:
        x2 = (x2 * lax.rsqrt(jnp.mean(x2 * x2, axis=-1, keepdims=True) + EPS)) * fg_ref[...]
    o_ref[...] = x2


def _combine(tables, pstart, ls_t, rw_t, x1, gt, final_g, ys, T, final):
    N, D = x1.shape
    tm = TM_MOE
    tpb = T // tm
    grid_spec = pltpu.PrefetchScalarGridSpec(
        num_scalar_prefetch=len(tables) + 1,
        grid=(N // tm,),
        in_specs=[pl.BlockSpec((tm, SUBLANES), lambda i, *_: (i, 0)),
                  pl.BlockSpec((tm, SUBLANES), lambda i, *_: (i, 0)),
                  pl.BlockSpec((tm, D), lambda i, *_: (i, 0)),
                  pl.BlockSpec((1, 1, D), lambda i, *_: (i // tpb, 0, 0)),
                  pl.BlockSpec((1, D), lambda i, *_: (0, 0)),
                  pl.BlockSpec(memory_space=pl.ANY)],
        out_specs=pl.BlockSpec((tm, D), lambda i, *_: (i, 0)),
        scratch_shapes=[pltpu.VMEM((2, MOE_STAGE_ROWS, D), BF16), pltpu.SemaphoreType.DMA((2,))],
    )
    return pl.pallas_call(
        functools.partial(_combine_kernel, final=final),
        grid_spec=grid_spec,
        out_shape=jax.ShapeDtypeStruct((N, D), F32),
        compiler_params=_cparams("moe_combine", ("arbitrary",)),
        name="moe_combine",
    )(*tables, pstart, ls_t, rw_t, x1, gt, final_g, ys)


def _moe_plan(cnt, nb):
    G, S = N_EXPERT_GROUPS, EXPERTS_PER_GROUP
    counts = cnt[:, 0]
    padded = (counts + MOE_ROWS - 1) // MOE_ROWS * MOE_ROWS
    pend = jnp.cumsum(padded).astype(I32)
    pstart = pend - padded
    pad_blk = jnp.where(padded > 0, pend // MOE_ROWS - 1, -1).astype(I32)
    block_start = jnp.arange(nb, dtype=I32) * MOE_ROWS
    block_row = jnp.minimum(jnp.sum((pend[None, :] <= block_start[:, None]).astype(I32), axis=1), G * S - 1)
    block_e = (block_row % G) * S + block_row // G
    block_rows = jnp.clip((pstart + counts)[block_row] - block_start, 0, MOE_ROWS)
    n_used = pend[-1:] // MOE_ROWS
    return pstart, pad_blk, block_e.astype(I32), block_rows.astype(I32), n_used.astype(I32)


def _block_diag(w):
    nb, bw, _ = w.shape
    eye = jnp.eye(nb, dtype=w.dtype)
    return jnp.einsum('nij,nm->nimj', w, eye).reshape(nb * bw, nb * bw)


def kernel(x, c, ada_w, ada_b, norm_mix_g, norm_ffn_g, w_in, w_branch_gate, b_branch_gate,
           na_rpb, lru_conv_w, lru_conv_b, lru_w_r, lru_b_r, lru_w_i, lru_b_i, lru_lambda,
           w_proj_na, w_proj_lru, w_proj_fnet, w_out, w_router, router_bias,
           w_exp_gate, w_exp_up, w_exp_down, final_g):
    B, T, D = x.shape
    L = ada_w.shape[0]
    N = B * T
    E = N_EXPERTS
    G, S = N_EXPERT_GROUPS, EXPERTS_PER_GROUP
    A = N * TOP_K
    assert TM_MERGE == TM_MOE
    max_rows = A + (N // TM_MOE) * E * (RUN_ALIGN - 1) + E * (MOE_ROWS - 1)
    nb = -(-max_rows // MOE_ROWS)
    P = nb * MOE_ROWS

    mod = _ada_mod(c, ada_w, ada_b)
    m1_np, cs_np = _fnet_tables(T)
    m1 = jnp.asarray(m1_np).astype(BF16)
    cs = jnp.asarray(cs_np).astype(BF16)
    tri = (lax.broadcasted_iota(I32, (TM_MERGE, TM_MERGE), 0)
           <= lax.broadcasted_iota(I32, (TM_MERGE, TM_MERGE), 1)).astype(BF16)
    wr = w_router.T.reshape(G, S, D).transpose(1, 0, 2).reshape(E, D)
    rb = router_bias.reshape(G, S).T.reshape(E, 1)
    fg = final_g.reshape(1, D)

    x2 = x.reshape(N, D)
    for l in range(L):
        sh_mix, sc_mix, gt_mix, sh_ffn, sc_ffn, gt_ffn = [m.reshape(B, 1, D) for m in jnp.split(mod[l], 6, axis=-1)]
        wcat = jnp.concatenate([w_in[l], w_branch_gate[l]], axis=1).astype(BF16)
        q, k, v, xr, gy, fr, gates = _mix_in(x2, norm_mix_g[l].reshape(1, D), sh_mix, sc_mix, wcat,
                                             b_branch_gate[l].reshape(1, -1), B, T)
        y_na = _na(q, k, v, _na_bias_table(na_rpb[l]), B, T)

        wg = (0.5 * jnp.stack([jnp.concatenate([_block_diag(lru_w_r[l, d]), _block_diag(lru_w_i[l, d])], axis=1)
                               for d in range(2)])).astype(BF16)
        bgl = 0.5 * jnp.stack([jnp.concatenate([lru_b_r[l, d], lru_b_i[l, d]]) for d in range(2)])[:, None, :]
        af, bf, ab, bb = _lru_gates(xr, lru_conv_w[l], lru_conv_b[l].reshape(1, -1), wg, bgl,
                                    lru_lambda[l][:, None, :], B, T)
        y_lru = _lru_scans(af, bf, ab, bb, gy)
        y_f = _fnet(fr, m1, cs, B, T)

        x1, u2, ls, rw, tc8, toff, tg, cnt = _merge(
            gates, y_na, y_lru, y_f, x2,
            w_proj_na[l].astype(BF16), w_proj_lru[l].astype(BF16),
            w_proj_fnet[l].astype(BF16), w_out[l].astype(BF16),
            gt_mix, norm_ffn_g[l].reshape(1, D), sh_ffn, sc_ffn, wr, rb, tri, B, T)
        tables = [t[:, :, 0].reshape(-1) for t in (tc8, toff, tg)]
        pstart, pad_blk, block_e, block_rows, n_used = _moe_plan(cnt, nb)
        xs = _dispatch(tables, pstart, pad_blk, n_used, ls, u2, P, nb - A // MOE_ROWS)
        ys = _experts(block_e, block_rows, xs, w_exp_gate, w_exp_up, w_exp_down, l)
        x2 = _combine(tables, pstart, ls.T, rw.T, x1, gt_ffn, fg, ys, T, final=(l == L - 1))
    return x2.reshape(B, T, D)
```
